```python
import math
import jax, jax.numpy as jnp
from jax import lax
import numpy as np

D_MODEL = 1024
BATCH = 8
SEQ = 2048
DEPTH = 2

DIFF_HEADS = 4
DIFF_QK_DIM = 64
DIFF_V_DIM = 2 * DIFF_QK_DIM
DIFF_WIDTH = DIFF_HEADS * DIFF_V_DIM
GLA_HEADS = 4
GLA_WIDTH = D_MODEL - DIFF_WIDTH
GLA_V_DIM = GLA_WIDTH // GLA_HEADS
GLA_K_DIM = GLA_V_DIM // 2
GLA_KEY_WIDTH = GLA_HEADS * GLA_K_DIM
GLA_GATE_RANK = 16
GLA_GATE_TAU = 16.0
GLA_CHUNK = 64
D_FF = 4 * D_MODEL
N_BUCKETS = 32
MAX_DISTANCE = 128
Q_BLOCK = 128
LN_EPS = 1e-5
RMS_EPS = 1e-5
ALPHA = (2.0 * DEPTH) ** 0.25
BETA = (8.0 * DEPTH) ** -0.25
IN_SPLITS = (DIFF_WIDTH, DIFF_WIDTH, DIFF_WIDTH, GLA_KEY_WIDTH, GLA_KEY_WIDTH, GLA_WIDTH, GLA_WIDTH, GLA_GATE_RANK, GLA_GATE_RANK)
D_IN = sum(IN_SPLITS)
IN_OFFSETS = tuple(int(o) for o in np.cumsum(IN_SPLITS)[:-1])

kernel_name = "hybrid_diffattn_gla_deepnorm_encoder"


def layer_norm(x, g, b):
    xf = x.astype(jnp.float32)
    mu = jnp.mean(xf, axis=-1, keepdims=True)
    var = jnp.mean(jnp.square(xf - mu), axis=-1, keepdims=True)
    y = (xf - mu) * lax.rsqrt(var + LN_EPS)
    return (y * g.astype(jnp.float32) + b.astype(jnp.float32)).astype(x.dtype)


def rms_norm(x, w):
    xf = x.astype(jnp.float32)
    y = xf * lax.rsqrt(jnp.mean(jnp.square(xf), axis=-1, keepdims=True) + RMS_EPS)
    return (y * w.astype(jnp.float32)).astype(x.dtype)


def t5_bucket(rel):
    nb = N_BUCKETS // 2
    max_exact = nb // 2
    ret = jnp.where(rel > 0, nb, 0)
    n = jnp.abs(rel)
    large = max_exact + (jnp.log(jnp.maximum(n, 1).astype(jnp.float32) / max_exact)
                         / math.log(MAX_DISTANCE / max_exact) * (nb - max_exact)).astype(jnp.int32)
    large = jnp.minimum(large, nb - 1)
    return ret + jnp.where(n < max_exact, n, large)


def diff_attention(q1, q2, k1, k2, v, lam, table):
    B, H, S, dq = q1.shape
    dv = v.shape[-1]
    nb = S // Q_BLOCK

    def blocks(t):
        return t.reshape(B, H, nb, Q_BLOCK, dq).transpose(2, 0, 1, 3, 4)

    kpos = jnp.arange(S, dtype=jnp.int32)
    lam32 = lam.astype(jnp.float32)

    def one_block(args):
        q1b, q2b, start = args
        qpos = start + jnp.arange(Q_BLOCK, dtype=jnp.int32)
        bias = table[t5_bucket(kpos[None, :] - qpos[:, None])]
        bias = jnp.transpose(bias, (2, 0, 1))[None].astype(jnp.float32)
        s1 = jnp.einsum('bhqd,bhkd->bhqk', q1b, k1).astype(jnp.float32) + bias
        s2 = jnp.einsum('bhqd,bhkd->bhqk', q2b, k2).astype(jnp.float32) + bias
        a = (jax.nn.softmax(s1, axis=-1) - lam32 * jax.nn.softmax(s2, axis=-1)).astype(v.dtype)
        return jnp.einsum('bhqk,bhkd->bhqd', a, v)

    starts = jnp.arange(nb, dtype=jnp.int32) * Q_BLOCK
    out = lax.map(one_block, (blocks(q1), blocks(q2), starts))
    return out.transpose(1, 2, 0, 3, 4).reshape(B, H, S, dv)


def gla_direction(q, k, v, g, strict):
    B, H, S, dk = q.shape
    dv = v.shape[-1]
    C = GLA_CHUNK
    nc = S // C
    f32 = jnp.float32
    qc = q.astype(f32).reshape(B, H, nc, C, dk)
    kc = k.astype(f32).reshape(B, H, nc, C, dk)
    vc = v.astype(f32).reshape(B, H, nc, C, dv)
    b = jnp.cumsum(g.astype(f32).reshape(B, H, nc, C, dk), axis=3)
    b_last = b[..., -1:, :]
    q_t = qc * jnp.exp(b)
    k_t = kc * jnp.exp(-b)
    k_d = kc * jnp.exp(b_last - b)
    dec = jnp.exp(b_last[..., 0, :])
    mask = jnp.tril(jnp.ones((C, C), dtype=bool), k=-1 if strict else 0)
    scores = jnp.where(mask, jnp.einsum('bhncd,bhnsd->bhncs', q_t, k_t), 0.0)
    o_intra = jnp.einsum('bhncs,bhnse->bhnce', scores, vc)

    def step(state, inp):
        qn, kn, vn, dn = inp
        o = jnp.einsum('bhcd,bhde->bhce', qn, state)
        state = dn[..., None] * state + jnp.einsum('bhcd,bhce->bhde', kn, vn)
        return state, o

    xs = (jnp.moveaxis(q_t, 2, 0), jnp.moveaxis(k_d, 2, 0), jnp.moveaxis(vc, 2, 0), jnp.moveaxis(dec, 2, 0))
    _, o_inter = lax.scan(step, jnp.zeros((B, H, dk, dv), f32), xs)
    o = o_intra + jnp.moveaxis(o_inter, 0, 2)
    return o.reshape(B, H, S, dv)


def hybrid_layer(x, li, table, w_in, lq1, lk1, lq2, lk2, diff_norm_w, gate_up, gate_b,
                 gla_norm_w, w_o, ln1_g, ln1_b, w1, b1, w2, b2, ln2_g, ln2_b):
    B, S, _ = x.shape
    h = x @ w_in
    dq, dk_, dv_, gq, gk, gv, gr, gdf, gdb = jnp.split(h, IN_OFFSETS, axis=-1)

    dq = dq.reshape(B, S, DIFF_HEADS, 2, DIFF_QK_DIM) * (DIFF_QK_DIM ** -0.5)
    dk_ = dk_.reshape(B, S, DIFF_HEADS, 2, DIFF_QK_DIM)
    q1 = dq[..., 0, :].transpose(0, 2, 1, 3)
    q2 = dq[..., 1, :].transpose(0, 2, 1, 3)
    k1 = dk_[..., 0, :].transpose(0, 2, 1, 3)
    k2 = dk_[..., 1, :].transpose(0, 2, 1, 3)
    vd = dv_.reshape(B, S, DIFF_HEADS, DIFF_V_DIM).transpose(0, 2, 1, 3)
    lam_init = 0.8 - 0.6 * math.exp(-0.3 * li)
    lam = (jnp.exp(jnp.sum(lq1.astype(jnp.float32) * lk1.astype(jnp.float32)))
           - jnp.exp(jnp.sum(lq2.astype(jnp.float32) * lk2.astype(jnp.float32))) + lam_init)
    d_out = diff_attention(q1, q2, k1, k2, vd, lam, table)
    d_out = rms_norm(d_out, diff_norm_w) * (1.0 - lam_init)
    d_out = d_out.transpose(0, 2, 1, 3).reshape(B, S, DIFF_WIDTH)

    def heads(t, d):
        return t.reshape(B, S, GLA_HEADS, d).transpose(0, 2, 1, 3)
    gq = heads(gq, GLA_K_DIM) * (GLA_K_DIM ** -0.5)
    gk = heads(gk, GLA_K_DIM)
    gv = heads(gv, GLA_V_DIM)
    g_f = heads(jax.nn.log_sigmoid((gdf @ gate_up[0] + gate_b[0]).astype(jnp.float32)) / GLA_GATE_TAU, GLA_K_DIM)
    g_b = heads(jax.nn.log_sigmoid((gdb @ gate_up[1] + gate_b[1]).astype(jnp.float32)) / GLA_GATE_TAU, GLA_K_DIM)
    o_f = gla_direction(gq, gk, gv, g_f, False)
    flip = lambda t: jnp.flip(t, axis=2)
    o_b = flip(gla_direction(flip(gq), flip(gk), flip(gv), flip(g_b), True))
    g_out = rms_norm((o_f + o_b).astype(x.dtype), gla_norm_w)
    g_out = g_out.transpose(0, 2, 1, 3).reshape(B, S, GLA_WIDTH) * jax.nn.silu(gr)

    mix = jnp.concatenate([d_out, g_out], axis=-1) @ w_o
    x = layer_norm(ALPHA * x + mix, ln1_g, ln1_b)

    f = jnp.square(jax.nn.relu(x @ w1 + b1)) @ w2 + b2
    return layer_norm(ALPHA * x + f, ln2_g, ln2_b)


def setup_inputs(seed: int = 0) -> dict:
    key = jax.random.key(seed)
    ks = jax.random.split(key, 24)
    n = lambda k, shape, s: jax.random.normal(k, shape, jnp.float32) * s
    return {
        "x": n(ks[0], (BATCH, SEQ, D_MODEL), 1.0),
        "ln_emb_g": 1.0 + n(ks[1], (D_MODEL,), 0.02),
        "ln_emb_b": n(ks[2], (D_MODEL,), 0.02),
        "rel_bias_table": n(ks[3], (N_BUCKETS, DIFF_HEADS), 0.3),
        "w_in": n(ks[4], (DEPTH, D_MODEL, D_IN), D_MODEL ** -0.5),
        "lambda_q1": n(ks[5], (DEPTH, DIFF_QK_DIM), 0.1),
        "lambda_k1": n(ks[6], (DEPTH, DIFF_QK_DIM), 0.1),
        "lambda_q2": n(ks[7], (DEPTH, DIFF_QK_DIM), 0.1),
        "lambda_k2": n(ks[8], (DEPTH, DIFF_QK_DIM), 0.1),
        "diff_norm_w": 1.0 + n(ks[9], (DEPTH, DIFF_V_DIM), 0.02),
        "gla_gate_up": n(ks[10], (DEPTH, 2, GLA_GATE_RANK, GLA_KEY_WIDTH), GLA_GATE_RANK ** -0.5),
        "gla_gate_bias": n(ks[11], (DEPTH, 2, GLA_KEY_WIDTH), 0.1),
        "gla_norm_w": 1.0 + n(ks[12], (DEPTH, GLA_V_DIM), 0.02),
        "w_o": n(ks[13], (DEPTH, D_MODEL, D_MODEL), BETA * D_MODEL ** -0.5),
        "ln1_g": 1.0 + n(ks[14], (DEPTH, D_MODEL), 0.02),
        "ln1_b": n(ks[15], (DEPTH, D_MODEL), 0.02),
        "w_ffn1": n(ks[16], (DEPTH, D_MODEL, D_FF), D_MODEL ** -0.5),
        "b_ffn1": n(ks[17], (DEPTH, D_FF), 0.02),
        "w_ffn2": n(ks[18], (DEPTH, D_FF, D_MODEL), BETA * D_FF ** -0.5),
        "b_ffn2": n(ks[19], (DEPTH, D_MODEL), 0.02),
        "ln2_g": 1.0 + n(ks[20], (DEPTH, D_MODEL), 0.02),
        "ln2_b": n(ks[21], (DEPTH, D_MODEL), 0.02),
    }


def reference(x, ln_emb_g, ln_emb_b, rel_bias_table, w_in, lambda_q1, lambda_k1, lambda_q2, lambda_k2,
              diff_norm_w, gla_gate_up, gla_gate_bias, gla_norm_w, w_o, ln1_g, ln1_b,
              w_ffn1, b_ffn1, w_ffn2, b_ffn2, ln2_g, ln2_b):
    h = layer_norm(x, ln_emb_g, ln_emb_b)
    for li in range(DEPTH):
        h = hybrid_layer(h, li, rel_bias_table, w_in[li], lambda_q1[li], lambda_k1[li],
                         lambda_q2[li], lambda_k2[li], diff_norm_w[li], gla_gate_up[li],
                         gla_gate_bias[li], gla_norm_w[li], w_o[li], ln1_g[li], ln1_b[li],
                         w_ffn1[li], b_ffn1[li], w_ffn2[li], b_ffn2[li], ln2_g[li], ln2_b[li])
    return h
```

```python
import functools
import math

import jax
import jax.numpy as jnp
from jax import lax
from jax.experimental import pallas as pl
from jax.experimental.pallas import tpu as pltpu

D_MODEL = 1024
DEPTH = 2
DIFF_HEADS = 4
DIFF_QK_DIM = 64
DIFF_V_DIM = 128
DIFF_WIDTH = 512
GLA_HEADS = 4
GLA_WIDTH = 512
GLA_V_DIM = 128
GLA_K_DIM = 64
GLA_KEY_WIDTH = 256
GLA_GATE_RANK = 16
GLA_GATE_TAU = 16.0
GLA_CHUNK = 64
D_FF = 4096
N_BUCKETS = 32
LN_EPS = 1e-5
RMS_EPS = 1e-5
ALPHA = (2.0 * DEPTH) ** 0.25
D_IN = 3104
D_IN_PAD = 3200

V7X_VMEM_LIMIT_BYTES = 56 * 1024 * 1024

BF16 = jnp.bfloat16
F32 = jnp.float32

_NT = (((1,), (1,)), ((), ()))
_TN = (((0,), (0,)), ((), ()))


def _dot(a, b):
    return jnp.dot(a, b, preferred_element_type=F32)


def _layer_norm(y, g, b):
    mu = jnp.mean(y, axis=-1, keepdims=True)
    d = y - mu
    var = jnp.mean(d * d, axis=-1, keepdims=True)
    return d * lax.rsqrt(var + LN_EPS) * g + b


def _band_kernel(table_ref, band_ref, *, tq, seq):
    h = pl.program_id(0)
    width = 2 * seq - tq
    r = lax.broadcasted_iota(jnp.int32, (tq, width), 0)
    m = lax.broadcasted_iota(jnp.int32, (tq, width), 1)
    rel = m - (seq - tq) - r
    n = jnp.abs(rel)
    n2 = n * n
    large = jnp.full_like(n, 8)
    for p in range(7, 14):
        large = large + jnp.where(n2 >= (1 << p), 1, 0)
    bucket = jnp.where(n < 8, n, large) + jnp.where(rel > 0, 16, 0)
    acc = jnp.zeros((tq, width), F32)
    for i in range(N_BUCKETS):
        acc = jnp.where(bucket == i, table_ref[i * DIFF_HEADS + h], acc)
    band_ref[0] = acc


def _bias_band(table, tq, seq):
    width = 2 * seq - tq
    return pl.pallas_call(
        functools.partial(_band_kernel, tq=tq, seq=seq),
        out_shape=jax.ShapeDtypeStruct((DIFF_HEADS, tq, width), F32),
        grid=(DIFF_HEADS,),
        in_specs=[pl.BlockSpec(memory_space=pltpu.SMEM)],
        out_specs=pl.BlockSpec((1, tq, width), lambda h: (h, 0, 0)),
        name="bias_band",
    )(table.reshape(-1))


def _inproj_kernel(*refs, apply_ln):
    if apply_ln:
        (x_ref, g_ref, b_ref, w_ref, h0_ref,
         qd_ref, kd_ref, vd_ref, gq_ref, gk_ref, gv_ref, gr_ref, gd_ref) = refs
        xn = _layer_norm(x_ref[...], g_ref[...], b_ref[...])
        h0_ref[...] = xn
    else:
        (x_ref, w_ref,
         qd_ref, kd_ref, vd_ref, gq_ref, gk_ref, gv_ref, gr_ref, gd_ref) = refs
        xn = x_ref[...]
    xb = xn.astype(BF16)
    qd_ref[...] = (_dot(xb, w_ref[:, 0:512]) * (DIFF_QK_DIM ** -0.5)).astype(BF16)
    kd_ref[...] = _dot(xb, w_ref[:, 512:1024]).astype(BF16)
    vd_ref[...] = _dot(xb, w_ref[:, 1024:1536]).astype(BF16)
    gq_ref[...] = _dot(xb, w_ref[:, 1536:1792]) * (GLA_K_DIM ** -0.5)
    gk_ref[...] = _dot(xb, w_ref[:, 1792:2048])
    gv_ref[...] = _dot(xb, w_ref[:, 2048:2560]).astype(BF16)
    gr_ref[...] = _dot(xb, w_ref[:, 2560:3072])
    gd_ref[...] = _dot(xb, w_ref[:, 3072:3200])[:, :2 * GLA_GATE_RANK]


def _inproj(x2d, w_pad, ln_g=None, ln_b=None, *, tm=512):
    t = x2d.shape[0]
    apply_ln = ln_g is not None
    row = lambda n: pl.BlockSpec((tm, n), lambda i: (i, 0))
    const = lambda shape: pl.BlockSpec(shape, lambda i: (0,) * len(shape))
    in_specs = [row(D_MODEL)]
    args = [x2d]
    if apply_ln:
        in_specs += [const((1, D_MODEL)), const((1, D_MODEL))]
        args += [ln_g.reshape(1, -1), ln_b.reshape(1, -1)]
    in_specs.append(const((D_MODEL, D_IN_PAD)))
    args.append(w_pad)
    widths = [(512, BF16), (512, BF16), (512, BF16), (256, F32), (256, F32),
              (512, BF16), (512, F32), (2 * GLA_GATE_RANK, F32)]
    out_shape = [jax.ShapeDtypeStruct((t, n), dt) for n, dt in widths]
    out_specs = [row(n) for n, _ in widths]
    if apply_ln:
        out_shape = [jax.ShapeDtypeStruct((t, D_MODEL), F32)] + out_shape
        out_specs = [row(D_MODEL)] + out_specs
    return pl.pallas_call(
        functools.partial(_inproj_kernel, apply_ln=apply_ln),
        out_shape=out_shape,
        grid=(t // tm,),
        in_specs=in_specs,
        out_specs=out_specs,
        compiler_params=pltpu.CompilerParams(
            dimension_semantics=("arbitrary",), vmem_limit_bytes=V7X_VMEM_LIMIT_BYTES),
        name="ln_inproj" if apply_ln else "inproj",
    )(*args)


def _attn_kernel(lq1_ref, lk1_ref, lq2_ref, lk2_ref, nw_ref, q_ref, k_ref, v_ref, band_ref,
                 o_ref, *, lam_init, tq, seq):
    qi = pl.program_id(2)
    nq = seq // tq
    lam = (jnp.exp(jnp.sum(lq1_ref[...] * lk1_ref[...], axis=-1, keepdims=True))
           - jnp.exp(jnp.sum(lq2_ref[...] * lk2_ref[...], axis=-1, keepdims=True)) + lam_init)
    q = q_ref[0]
    k = k_ref[0]
    v = v_ref[0]
    first = lax.broadcasted_iota(jnp.int32, (1, 2 * DIFF_QK_DIM), 1) < DIFF_QK_DIM
    zero = jnp.zeros_like(q)
    q1 = jnp.where(first, q, zero)
    q2 = jnp.where(first, zero, q)
    off = pl.multiple_of((nq - 1 - qi) * tq, 128)
    bias = band_ref[0, :, pl.ds(off, seq)]
    s1 = lax.dot_general(q1, k, _NT, preferred_element_type=F32) + bias
    s2 = lax.dot_general(q2, k, _NT, preferred_element_type=F32) + bias
    e1 = jnp.exp(s1 - jnp.max(s1, axis=-1, keepdims=True))
    e2 = jnp.exp(s2 - jnp.max(s2, axis=-1, keepdims=True))
    r1 = 1.0 / jnp.sum(e1, axis=-1, keepdims=True)
    r2 = lam / jnp.sum(e2, axis=-1, keepdims=True)
    a = (e1 * r1 - e2 * r2).astype(BF16)
    o = _dot(a, v)
    ms = jnp.mean(o * o, axis=-1, keepdims=True)
    y = o * lax.rsqrt(ms + RMS_EPS) * nw_ref[...] * (1.0 - lam_init)
    o_ref[0] = y.astype(o_ref.dtype)


def _diff_attention(qd, kd, vd, band, lq1, lk1, lq2, lk2, norm_w, *, lam_init, tq):
    b, seq, _ = qd.shape
    nq = seq // tq
    vec = lambda n: pl.BlockSpec((1, n), lambda h, bi, qi: (0, 0))
    return pl.pallas_call(
        functools.partial(_attn_kernel, lam_init=lam_init, tq=tq, seq=seq),
        out_shape=jax.ShapeDtypeStruct((b, seq, DIFF_WIDTH), BF16),
        grid=(DIFF_HEADS, b, nq),
        in_specs=[vec(DIFF_QK_DIM), vec(DIFF_QK_DIM), vec(DIFF_QK_DIM), vec(DIFF_QK_DIM),
                  vec(DIFF_V_DIM),
                  pl.BlockSpec((1, tq, 128), lambda h, bi, qi: (bi, qi, h)),
                  pl.BlockSpec((1, seq, 128), lambda h, bi, qi: (bi, 0, h)),
                  pl.BlockSpec((1, seq, 128), lambda h, bi, qi: (bi, 0, h)),
                  pl.BlockSpec((1, tq, 2 * seq - tq), lambda h, bi, qi: (h, 0, 0))],
        out_specs=pl.BlockSpec((1, tq, 128), lambda h, bi, qi: (bi, qi, h)),
        compiler_params=pltpu.CompilerParams(
            dimension_semantics=("arbitrary", "arbitrary", "arbitrary"),
            vmem_limit_bytes=V7X_VMEM_LIMIT_BYTES),
        name="diff_attn",
    )(lq1.reshape(1, -1), lk1.reshape(1, -1), lq2.reshape(1, -1), lk2.reshape(1, -1),
      norm_w.reshape(1, -1), qd, kd, vd, band)


def _chunk_scan(x, row, *, reverse):
    n = x.shape[0]
    d = 1
    while d < GLA_CHUNK:
        if reverse:
            x = x + jnp.where(row < GLA_CHUNK - d, pltpu.roll(x, n - d, 0), 0.0)
        else:
            x = x + jnp.where(row >= d, pltpu.roll(x, d, 0), 0.0)
        d *= 2
    return x


def _gla_kernel(gd_ref, up_ref, gb_ref, q_ref, k_ref, v_ref, r_ref, nw_ref, o_ref,
                qf_s, kf_s, qb_s, kb_s, bf_s, bb_s, acc_s, *, seq):
    c = GLA_CHUNK
    nc = seq // c
    z = _dot(gd_ref[0].astype(BF16), up_ref[0, 0]) + gb_ref[0, 0]
    g = (jnp.minimum(z, 0.0) - jnp.log1p(jnp.exp(-jnp.abs(z)))) / GLA_GATE_TAU
    row = lax.broadcasted_iota(jnp.int32, (seq, 1), 0) & (c - 1)
    bf = _chunk_scan(g[:, :128], row, reverse=False)
    bb = _chunk_scan(g[:, 128:], row, reverse=True)
    q = q_ref[0]
    k = k_ref[0]
    qf_s[...] = (q * jnp.exp(bf)).astype(BF16)
    kf_s[...] = (k * jnp.exp(-bf)).astype(BF16)
    qb_s[...] = (q * jnp.exp(bb)).astype(BF16)
    kb_s[...] = (k * jnp.exp(-bb)).astype(BF16)
    bf_s[...] = bf
    bb_s[...] = bb

    lane_k = lax.broadcasted_iota(jnp.int32, (1, 128), 1)
    head0_k = lane_k < GLA_K_DIM
    lane_v = lax.broadcasted_iota(jnp.int32, (1, 256), 1)
    head0_v = lane_v < GLA_V_DIM
    sr = lax.broadcasted_iota(jnp.int32, (256, 128), 0) < GLA_V_DIM
    sc = lax.broadcasted_iota(jnp.int32, (256, 128), 1) < GLA_K_DIM
    same_head = sr == sc
    ci = lax.broadcasted_iota(jnp.int32, (c, 128), 0)
    si = lax.broadcasted_iota(jnp.int32, (c, 128), 1) & (c - 1)
    causal = si <= ci
    anti = si > ci

    def chunk_out(n, qs, ks, mask, state):
        r0 = pl.multiple_of(n * c, c)
        qn = qs[pl.ds(r0, c), :]
        kn = ks[pl.ds(r0, c), :]
        vn = v_ref[0, pl.ds(r0, c), :]
        zk = jnp.zeros_like(kn)
        kbd = jnp.concatenate([jnp.where(head0_k, kn, zk), jnp.where(head0_k, zk, kn)], axis=0)
        scores = lax.dot_general(qn, kbd, _NT, preferred_element_type=F32)
        scores = jnp.where(mask, scores, 0.0).astype(BF16)
        zv = jnp.zeros_like(vn)
        vbd = jnp.concatenate([jnp.where(head0_v, vn, zv), jnp.where(head0_v, zv, vn)], axis=0)
        o = _dot(scores, vbd)
        o = o + lax.dot_general(qn, state.astype(BF16), _NT, preferred_element_type=F32)
        kv = lax.dot_general(vn, kn, _TN, preferred_element_type=F32)
        return r0, o, jnp.where(same_head, kv, 0.0)

    def fwd_body(n, state):
        r0, o, kv = chunk_out(n, qf_s, kf_s, causal, state)
        acc_s[pl.ds(r0, c), :] = o
        dec = jnp.exp(bf_s[pl.ds(r0 + (c - 1), 1), :])
        return dec * (state + kv)

    def bwd_body(i, state):
        n = nc - 1 - i
        r0, o, kv = chunk_out(n, qb_s, kb_s, anti, state)
        acc_s[pl.ds(r0, c), :] += o
        dec = jnp.exp(bb_s[pl.ds(r0, 1), :])
        return dec * (state + kv)

    zero_state = jnp.zeros((256, 128), F32)
    lax.fori_loop(0, nc, fwd_body, zero_state)
    lax.fori_loop(0, nc, bwd_body, zero_state)

    nw = nw_ref[...]
    for hh in range(2):
        sl = slice(hh * GLA_V_DIM, (hh + 1) * GLA_V_DIM)
        oh = acc_s[:, sl]
        y = oh * lax.rsqrt(jnp.mean(oh * oh, axis=-1, keepdims=True) + RMS_EPS) * nw
        gate = r_ref[0, :, sl]
        o_ref[0, :, sl] = (y * (gate * jax.nn.sigmoid(gate))).astype(o_ref.dtype)


def _gla(gd, up_bd, gb_bd, gq, gk, gv, gr, norm_w):
    b, seq, _ = gq.shape
    hp = GLA_HEADS // 2
    return pl.pallas_call(
        functools.partial(_gla_kernel, seq=seq),
        out_shape=jax.ShapeDtypeStruct((b, seq, GLA_WIDTH), BF16),
        grid=(b, hp),
        in_specs=[pl.BlockSpec((1, seq, 2 * GLA_GATE_RANK), lambda bi, p: (bi, 0, 0)),
                  pl.BlockSpec((1, 1, 2 * GLA_GATE_RANK, 256), lambda bi, p: (p, 0, 0, 0)),
                  pl.BlockSpec((1, 1, 1, 256), lambda bi, p: (p, 0, 0, 0)),
                  pl.BlockSpec((1, seq, 128), lambda bi, p: (bi, 0, p)),
                  pl.BlockSpec((1, seq, 128), lambda bi, p: (bi, 0, p)),
                  pl.BlockSpec((1, seq, 256), lambda bi, p: (bi, 0, p)),
                  pl.BlockSpec((1, seq, 256), lambda bi, p: (bi, 0, p)),
                  pl.BlockSpec((1, GLA_V_DIM), lambda bi, p: (0, 0))],
        out_specs=pl.BlockSpec((1, seq, 256), lambda bi, p: (bi, 0, p)),
        scratch_shapes=[pltpu.VMEM((seq, 128), BF16), pltpu.VMEM((seq, 128), BF16),
                        pltpu.VMEM((seq, 128), BF16), pltpu.VMEM((seq, 128), BF16),
                        pltpu.VMEM((seq, 128), F32), pltpu.VMEM((seq, 128), F32),
                        pltpu.VMEM((seq, 256), F32)],
        compiler_params=pltpu.CompilerParams(
            dimension_semantics=("arbitrary", "arbitrary"),
            vmem_limit_bytes=V7X_VMEM_LIMIT_BYTES),
        name="gla",
    )(gd, up_bd, gb_bd, gq, gk, gv, gr, norm_w.reshape(1, -1))


def _outproj_kernel(d_ref, g_ref, x_ref, w_ref, lg_ref, lb_ref, o_ref):
    mix = _dot(d_ref[...], w_ref[0:DIFF_WIDTH, :]) + _dot(g_ref[...], w_ref[DIFF_WIDTH:D_MODEL, :])
    o_ref[...] = _layer_norm(ALPHA * x_ref[...] + mix, lg_ref[...], lb_ref[...])


def _outproj(d2d, g2d, x2d, w_o, ln_g, ln_b, *, tm=512):
    t = x2d.shape[0]
    row = lambda n: pl.BlockSpec((tm, n), lambda i: (i, 0))
    const = lambda shape: pl.BlockSpec(shape, lambda i: (0,) * len(shape))
    return pl.pallas_call(
        _outproj_kernel,
        out_shape=jax.ShapeDtypeStruct((t, D_MODEL), F32),
        grid=(t // tm,),
        in_specs=[row(DIFF_WIDTH), row(GLA_WIDTH), row(D_MODEL), const((D_MODEL, D_MODEL)),
                  const((1, D_MODEL)), const((1, D_MODEL))],
        out_specs=row(D_MODEL),
        compiler_params=pltpu.CompilerParams(
            dimension_semantics=("arbitrary",), vmem_limit_bytes=V7X_VMEM_LIMIT_BYTES),
        name="outproj_ln",
    )(d2d, g2d, x2d, w_o, ln_g.reshape(1, -1), ln_b.reshape(1, -1))


def _mlp_kernel(x_ref, w1_ref, b1_ref, w2_ref, b2_ref, lg_ref, lb_ref, o_ref, *, ff_chunk):
    x = x_ref[...]
    xb = x.astype(BF16)
    acc = jnp.zeros(x.shape, F32)
    for c0 in range(0, D_FF, ff_chunk):
        h = _dot(xb, w1_ref[:, c0:c0 + ff_chunk]) + b1_ref[:, c0:c0 + ff_chunk]
        h = jnp.square(jnp.maximum(h, 0.0)).astype(BF16)
        acc = acc + _dot(h, w2_ref[c0:c0 + ff_chunk, :])
    o_ref[...] = _layer_norm(ALPHA * x + (acc + b2_ref[...]), lg_ref[...], lb_ref[...])


def _mlp(x2d, w1, b1, w2, b2, ln_g, ln_b, *, tm=512, ff_chunk=1024):
    t = x2d.shape[0]
    row = lambda n: pl.BlockSpec((tm, n), lambda i: (i, 0))
    const = lambda shape: pl.BlockSpec(shape, lambda i: (0,) * len(shape))
    return pl.pallas_call(
        functools.partial(_mlp_kernel, ff_chunk=ff_chunk),
        out_shape=jax.ShapeDtypeStruct((t, D_MODEL), F32),
        grid=(t // tm,),
        in_specs=[row(D_MODEL), const((D_MODEL, D_FF)), const((1, D_FF)), const((D_FF, D_MODEL)),
                  const((1, D_MODEL)), const((1, D_MODEL)), const((1, D_MODEL))],
        out_specs=row(D_MODEL),
        compiler_params=pltpu.CompilerParams(
            dimension_semantics=("arbitrary",), vmem_limit_bytes=V7X_VMEM_LIMIT_BYTES),
        name="mlp_ln",
    )(x2d, w1, b1.reshape(1, -1), w2, b2.reshape(1, -1), ln_g.reshape(1, -1), ln_b.reshape(1, -1))


def _gate_weights(gate_up, gate_bias):
    hp = GLA_HEADS // 2
    up = gate_up.reshape(2, GLA_GATE_RANK, hp, 128)
    z = jnp.zeros((GLA_GATE_RANK, hp, 128), gate_up.dtype)
    top = jnp.concatenate([up[0], z], axis=-1)
    bot = jnp.concatenate([z, up[1]], axis=-1)
    up_bd = jnp.concatenate([top, bot], axis=0)
    up_bd = up_bd.transpose(1, 0, 2)[:, None].astype(BF16)
    gb = gate_bias.reshape(2, hp, 128)
    gb_bd = jnp.concatenate([gb[0], gb[1]], axis=-1)[:, None, None]
    return up_bd, gb_bd


def kernel(x, ln_emb_g, ln_emb_b, rel_bias_table, w_in, lambda_q1, lambda_k1, lambda_q2, lambda_k2,
           diff_norm_w, gla_gate_up, gla_gate_bias, gla_norm_w, w_o, ln1_g, ln1_b,
           w_ffn1, b_ffn1, w_ffn2, b_ffn2, ln2_g, ln2_b):
    b, seq, _ = x.shape
    t = b * seq
    tq = 128
    band = _bias_band(rel_bias_table, tq, seq)
    w_in_b = jnp.pad(w_in.astype(BF16), ((0, 0), (0, 0), (0, D_IN_PAD - D_IN)))
    w_o_b = w_o.astype(BF16)
    w1_b = w_ffn1.astype(BF16)
    w2_b = w_ffn2.astype(BF16)

    h = x.reshape(t, D_MODEL)
    for li in range(DEPTH):
        if li == 0:
            h, qd, kd, vd, gq, gk, gv, gr, gd = _inproj(h, w_in_b[li], ln_emb_g, ln_emb_b)
        else:
            qd, kd, vd, gq, gk, gv, gr, gd = _inproj(h, w_in_b[li])
        r3 = lambda a: a.reshape(b, seq, a.shape[-1])
        lam_init = 0.8 - 0.6 * math.exp(-0.3 * li)
        d_out = _diff_attention(r3(qd), r3(kd), r3(vd), band, lambda_q1[li], lambda_k1[li],
                                lambda_q2[li], lambda_k2[li], diff_norm_w[li],
                                lam_init=lam_init, tq=tq)
        up_bd, gb_bd = _gate_weights(gla_gate_up[li], gla_gate_bias[li])
        g_out = _gla(r3(gd), up_bd, gb_bd, r3(gq), r3(gk), r3(gv), r3(gr), gla_norm_w[li])
        h = _outproj(d_out.reshape(t, DIFF_WIDTH), g_out.reshape(t, GLA_WIDTH), h, w_o_b[li],
                     ln1_g[li], ln1_b[li])
        h = _mlp(h, w1_b[li], b_ffn1[li], w2_b[li], b_ffn2[li], ln2_g[li], ln2_b[li])
    return h.reshape(b, seq, D_MODEL)
```

```python
import functools
import math

import jax
import jax.numpy as jnp
from jax import lax
from jax.experimental import pallas as pl
from jax.experimental.pallas import tpu as pltpu

D_MODEL = 1024
DEPTH = 2
DIFF_HEADS = 4
DIFF_QK_DIM = 64
DIFF_V_DIM = 128
DIFF_WIDTH = 512
GLA_HEADS = 4
GLA_WIDTH = 512
GLA_V_DIM = 128
GLA_K_DIM = 64
GLA_KEY_WIDTH = 256
GLA_GATE_RANK = 16
GLA_GATE_TAU = 16.0
GLA_CHUNK = 64
D_FF = 4096
N_BUCKETS = 32
LN_EPS = 1e-5
RMS_EPS = 1e-5
ALPHA = (2.0 * DEPTH) ** 0.25
LOG2E = math.log2(math.e)
D_IN = 3104
D_IN_PAD = 3200

V7X_VMEM_LIMIT_BYTES = 56 * 1024 * 1024

BF16 = jnp.bfloat16
F32 = jnp.float32

_NT = (((1,), (1,)), ((), ()))
_TN = (((0,), (0,)), ((), ()))


def _dot(a, b):
    return jnp.dot(a, b, preferred_element_type=F32)


def _layer_norm(y, g, b):
    mu = jnp.mean(y, axis=-1, keepdims=True)
    d = y - mu
    var = jnp.mean(d * d, axis=-1, keepdims=True)
    return d * lax.rsqrt(var + LN_EPS) * g + b


def _band_kernel(table_ref, band_ref, *, tq, seq):
    h = pl.program_id(0)
    width = 2 * seq - tq
    r = lax.broadcasted_iota(jnp.int32, (tq, width), 0)
    m = lax.broadcasted_iota(jnp.int32, (tq, width), 1)
    rel = m - (seq - tq) - r
    n = jnp.abs(rel)
    n2 = n * n
    large = jnp.full_like(n, 8)
    for p in range(7, 14):
        large = large + jnp.where(n2 >= (1 << p), 1, 0)
    bucket = jnp.where(n < 8, n, large) + jnp.where(rel > 0, 16, 0)
    acc = jnp.zeros((tq, width), F32)
    for i in range(N_BUCKETS):
        acc = jnp.where(bucket == i, table_ref[i * DIFF_HEADS + h], acc)
    band_ref[0] = acc * LOG2E


def _bias_band(table, tq, seq):
    width = 2 * seq - tq
    return pl.pallas_call(
        functools.partial(_band_kernel, tq=tq, seq=seq),
        out_shape=jax.ShapeDtypeStruct((DIFF_HEADS, tq, width), F32),
        grid=(DIFF_HEADS,),
        in_specs=[pl.BlockSpec(memory_space=pltpu.SMEM)],
        out_specs=pl.BlockSpec((1, tq, width), lambda h: (h, 0, 0)),
        name="bias_band",
    )(table.reshape(-1))


def _inproj_kernel(*refs, apply_ln):
    if apply_ln:
        (x_ref, g_ref, b_ref, w_ref, h0_ref,
         qd_ref, kd_ref, vd_ref, gq_ref, gk_ref, gv_ref, gr_ref, gd_ref) = refs
        xn = _layer_norm(x_ref[...], g_ref[...], b_ref[...])
        h0_ref[...] = xn
    else:
        (x_ref, w_ref,
         qd_ref, kd_ref, vd_ref, gq_ref, gk_ref, gv_ref, gr_ref, gd_ref) = refs
        xn = x_ref[...]
    xb = xn.astype(BF16)
    qd_ref[...] = (_dot(xb, w_ref[:, 0:512]) * (DIFF_QK_DIM ** -0.5 * LOG2E)).astype(BF16)
    kd_ref[...] = _dot(xb, w_ref[:, 512:1024]).astype(BF16)
    tm = xb.shape[0]
    e0 = (lax.broadcasted_iota(jnp.int32, (tm, DIFF_V_DIM), 1) == 0).astype(BF16)
    v = _dot(xb, w_ref[:, 1024:1536]).astype(BF16)
    for hd in range(DIFF_HEADS):
        vd_ref[:, 2 * hd * DIFF_V_DIM:(2 * hd + 1) * DIFF_V_DIM] = v[:, hd * DIFF_V_DIM:(hd + 1) * DIFF_V_DIM]
        vd_ref[:, (2 * hd + 1) * DIFF_V_DIM:(2 * hd + 2) * DIFF_V_DIM] = e0
    gq_ref[...] = _dot(xb, w_ref[:, 1536:1792]) * (GLA_K_DIM ** -0.5)
    gk_ref[...] = _dot(xb, w_ref[:, 1792:2048])
    gv_ref[...] = _dot(xb, w_ref[:, 2048:2560]).astype(BF16)
    gr_ref[...] = _dot(xb, w_ref[:, 2560:3072])
    gd_ref[...] = _dot(xb, w_ref[:, 3072:3200])[:, :2 * GLA_GATE_RANK]


def _inproj(x2d, w_pad, ln_g=None, ln_b=None, *, tm=512):
    t = x2d.shape[0]
    apply_ln = ln_g is not None
    row = lambda n: pl.BlockSpec((tm, n), lambda i: (i, 0))
    const = lambda shape: pl.BlockSpec(shape, lambda i: (0,) * len(shape))
    in_specs = [row(D_MODEL)]
    args = [x2d]
    if apply_ln:
        in_specs += [const((1, D_MODEL)), const((1, D_MODEL))]
        args += [ln_g.reshape(1, -1), ln_b.reshape(1, -1)]
    in_specs.append(const((D_MODEL, D_IN_PAD)))
    args.append(w_pad)
    widths = [(512, BF16), (512, BF16), (2 * DIFF_WIDTH, BF16), (256, F32), (256, F32),
              (512, BF16), (512, F32), (2 * GLA_GATE_RANK, F32)]
    out_shape = [jax.ShapeDtypeStruct((t, n), dt) for n, dt in widths]
    out_specs = [row(n) for n, _ in widths]
    if apply_ln:
        out_shape = [jax.ShapeDtypeStruct((t, D_MODEL), F32)] + out_shape
        out_specs = [row(D_MODEL)] + out_specs
    return pl.pallas_call(
        functools.partial(_inproj_kernel, apply_ln=apply_ln),
        out_shape=out_shape,
        grid=(t // tm,),
        in_specs=in_specs,
        out_specs=out_specs,
        compiler_params=pltpu.CompilerParams(
            dimension_semantics=("arbitrary",), vmem_limit_bytes=V7X_VMEM_LIMIT_BYTES),
        name="ln_inproj" if apply_ln else "inproj",
    )(*args)


def _attn_kernel(lq1_ref, lk1_ref, lq2_ref, lk2_ref, nw_ref, q_ref, k_ref, v_ref, band_ref,
                 o_ref, s_s, e_s, *, lam_init, tq, sub, seq, tk, rc):
    qi = pl.program_id(2)
    lam = (jnp.exp(jnp.sum(lq1_ref[...] * lk1_ref[...], axis=-1, keepdims=True))
           - jnp.exp(jnp.sum(lq2_ref[...] * lk2_ref[...], axis=-1, keepdims=True)) + lam_init)
    first = lax.broadcasted_iota(jnp.int32, (1, 2 * DIFF_QK_DIM), 1) < DIFF_QK_DIM
    for sb in range(tq // sub):
        rows = slice(sb * sub, (sb + 1) * sub)
        q = q_ref[0, rows, :]
        zero = jnp.zeros_like(q)
        qq = jnp.concatenate([jnp.where(first, q, zero), jnp.where(first, zero, q)], axis=0)
        off = pl.multiple_of(seq - sub - (qi * tq + sb * sub), 128)
        for j in range(seq // tk):
            cols = slice(j * tk, (j + 1) * tk)
            s_s[sb, :, cols] = lax.dot_general(qq, k_ref[0, cols, :], _NT, preferred_element_type=F32)
        oo = []
        for mp in range(2):
            base = mp * sub
            ms = []
            for r in range(sub // rc):
                bias = band_ref[0, r * rc:(r + 1) * rc, pl.ds(off, seq)]
                sbias = s_s[sb, base + r * rc:base + (r + 1) * rc, :] + bias
                ms.append(jnp.max(sbias, axis=-1, keepdims=True))
            for r in range(sub // rc):
                bias = band_ref[0, r * rc:(r + 1) * rc, pl.ds(off, seq)]
                rr = slice(base + r * rc, base + (r + 1) * rc)
                e_s[sb, rr, :] = jnp.exp2((s_s[sb, rr, :] - ms[r]) + bias).astype(BF16)
            oo.append(_dot(e_s[sb, base:base + sub, :], v_ref[0]))
        r1 = 1.0 / oo[0][:, DIFF_V_DIM:DIFF_V_DIM + 1]
        r2 = lam / oo[1][:, DIFF_V_DIM:DIFF_V_DIM + 1]
        o = oo[0][:, :DIFF_V_DIM] * r1 - oo[1][:, :DIFF_V_DIM] * r2
        ms = jnp.mean(o * o, axis=-1, keepdims=True)
        y = o * lax.rsqrt(ms + RMS_EPS) * nw_ref[...] * (1.0 - lam_init)
        o_ref[0, rows, :] = y.astype(o_ref.dtype)


def _diff_attention(qd, kd, vd, band, lq1, lk1, lq2, lk2, norm_w, *, lam_init, tq, sub):
    b, seq, _ = qd.shape
    nq = seq // tq
    vec = lambda n: pl.BlockSpec((1, n), lambda h, bi, qi: (0, 0))
    return pl.pallas_call(
        functools.partial(_attn_kernel, lam_init=lam_init, tq=tq, sub=sub, seq=seq, tk=512, rc=16),
        out_shape=jax.ShapeDtypeStruct((b, seq, DIFF_WIDTH), BF16),
        grid=(DIFF_HEADS, b, nq),
        in_specs=[vec(DIFF_QK_DIM), vec(DIFF_QK_DIM), vec(DIFF_QK_DIM), vec(DIFF_QK_DIM),
                  vec(DIFF_V_DIM),
                  pl.BlockSpec((1, tq, 128), lambda h, bi, qi: (bi, qi, h)),
                  pl.BlockSpec((1, seq, 128), lambda h, bi, qi: (bi, 0, h)),
                  pl.BlockSpec((1, seq, 256), lambda h, bi, qi: (bi, 0, h)),
                  pl.BlockSpec((1, sub, 2 * seq - sub), lambda h, bi, qi: (h, 0, 0))],
        out_specs=pl.BlockSpec((1, tq, 128), lambda h, bi, qi: (bi, qi, h)),
        scratch_shapes=[pltpu.VMEM((tq // sub, 2 * sub, seq), F32),
                        pltpu.VMEM((tq // sub, 2 * sub, seq), BF16)],
        compiler_params=pltpu.CompilerParams(
            dimension_semantics=("arbitrary", "arbitrary", "arbitrary"),
            vmem_limit_bytes=V7X_VMEM_LIMIT_BYTES),
        name="diff_attn",
    )(lq1.reshape(1, -1), lk1.reshape(1, -1), lq2.reshape(1, -1), lk2.reshape(1, -1),
      norm_w.reshape(1, -1), qd, kd, vd, band)


def _chunk_scan(x, row, *, reverse):
    n = x.shape[0]
    d = 1
    while d < GLA_CHUNK:
        if reverse:
            x = x + jnp.where(row < GLA_CHUNK - d, pltpu.roll(x, n - d, 0), 0.0)
        else:
            x = x + jnp.where(row >= d, pltpu.roll(x, d, 0), 0.0)
        d *= 2
    return x


def _gla_kernel(gd_ref, up_ref, gb_ref, q_ref, k_ref, v_ref, r_ref, nw_ref, o_ref,
                qf_s, kf_s, qb_s, kb_s, bf_s, bb_s, acc_s, *, seq):
    c = GLA_CHUNK
    nc = seq // c
    z = _dot(gd_ref[0].astype(BF16), up_ref[0, 0]) + gb_ref[0, 0]
    g = (jnp.minimum(z, 0.0) - jnp.log1p(jnp.exp(-jnp.abs(z)))) / GLA_GATE_TAU
    row = lax.broadcasted_iota(jnp.int32, (seq, 1), 0) & (c - 1)
    bf = _chunk_scan(g[:, :128], row, reverse=False)
    bb = _chunk_scan(g[:, 128:], row, reverse=True)
    q = q_ref[0]
    k = k_ref[0]
    qf_s[...] = (q * jnp.exp(bf)).astype(BF16)
    kf_s[...] = (k * jnp.exp(-bf)).astype(BF16)
    qb_s[...] = (q * jnp.exp(bb)).astype(BF16)
    kb_s[...] = (k * jnp.exp(-bb)).astype(BF16)
    bf_s[...] = bf
    bb_s[...] = bb

    lane_k = lax.broadcasted_iota(jnp.int32, (1, 128), 1)
    head0_k = lane_k < GLA_K_DIM
    lane_v = lax.broadcasted_iota(jnp.int32, (1, 256), 1)
    head0_v = lane_v < GLA_V_DIM
    sr = lax.broadcasted_iota(jnp.int32, (256, 128), 0) < GLA_V_DIM
    sc = lax.broadcasted_iota(jnp.int32, (256, 128), 1) < GLA_K_DIM
    same_head = sr == sc
    ci = lax.broadcasted_iota(jnp.int32, (c, 128), 0)
    si = lax.broadcasted_iota(jnp.int32, (c, 128), 1) & (c - 1)
    causal = si <= ci
    anti = si > ci

    def chunk_out(n, qs, ks, mask, state):
        r0 = pl.multiple_of(n * c, c)
        qn = qs[pl.ds(r0, c), :]
        kn = ks[pl.ds(r0, c), :]
        vn = v_ref[0, pl.ds(r0, c), :]
        zk = jnp.zeros_like(kn)
        kbd = jnp.concatenate([jnp.where(head0_k, kn, zk), jnp.where(head0_k, zk, kn)], axis=0)
        scores = lax.dot_general(qn, kbd, _NT, preferred_element_type=F32)
        scores = jnp.where(mask, scores, 0.0).astype(BF16)
        zv = jnp.zeros_like(vn)
        vbd = jnp.concatenate([jnp.where(head0_v, vn, zv), jnp.where(head0_v, zv, vn)], axis=0)
        o = _dot(scores, vbd)
        o = o + lax.dot_general(qn, state.astype(BF16), _NT, preferred_element_type=F32)
        kv = lax.dot_general(vn, kn, _TN, preferred_element_type=F32)
        return r0, o, jnp.where(same_head, kv, 0.0)

    def fwd_body(n, state):
        r0, o, kv = chunk_out(n, qf_s, kf_s, causal, state)
        acc_s[pl.ds(r0, c), :] = o
        dec = jnp.exp(bf_s[pl.ds(r0 + (c - 1), 1), :])
        return dec * (state + kv)

    def bwd_body(i, state):
        n = nc - 1 - i
        r0, o, kv = chunk_out(n, qb_s, kb_s, anti, state)
        acc_s[pl.ds(r0, c), :] += o
        dec = jnp.exp(bb_s[pl.ds(r0, 1), :])
        return dec * (state + kv)

    zero_state = jnp.zeros((256, 128), F32)
    lax.fori_loop(0, nc, fwd_body, zero_state)
    lax.fori_loop(0, nc, bwd_body, zero_state)

    nw = nw_ref[...]
    for hh in range(2):
        sl = slice(hh * GLA_V_DIM, (hh + 1) * GLA_V_DIM)
        oh = acc_s[:, sl]
        y = oh * lax.rsqrt(jnp.mean(oh * oh, axis=-1, keepdims=True) + RMS_EPS) * nw
        gate = r_ref[0, :, sl]
        o_ref[0, :, sl] = (y * (gate * jax.nn.sigmoid(gate))).astype(o_ref.dtype)


def _gla(gd, up_bd, gb_bd, gq, gk, gv, gr, norm_w):
    b, seq, _ = gq.shape
    hp = GLA_HEADS // 2
    return pl.pallas_call(
        functools.partial(_gla_kernel, seq=seq),
        out_shape=jax.ShapeDtypeStruct((b, seq, GLA_WIDTH), BF16),
        grid=(b, hp),
        in_specs=[pl.BlockSpec((1, seq, 2 * GLA_GATE_RANK), lambda bi, p: (bi, 0, 0)),
                  pl.BlockSpec((1, 1, 2 * GLA_GATE_RANK, 256), lambda bi, p: (p, 0, 0, 0)),
                  pl.BlockSpec((1, 1, 1, 256), lambda bi, p: (p, 0, 0, 0)),
                  pl.BlockSpec((1, seq, 128), lambda bi, p: (bi, 0, p)),
                  pl.BlockSpec((1, seq, 128), lambda bi, p: (bi, 0, p)),
                  pl.BlockSpec((1, seq, 256), lambda bi, p: (bi, 0, p)),
                  pl.BlockSpec((1, seq, 256), lambda bi, p: (bi, 0, p)),
                  pl.BlockSpec((1, GLA_V_DIM), lambda bi, p: (0, 0))],
        out_specs=pl.BlockSpec((1, seq, 256), lambda bi, p: (bi, 0, p)),
        scratch_shapes=[pltpu.VMEM((seq, 128), BF16), pltpu.VMEM((seq, 128), BF16),
                        pltpu.VMEM((seq, 128), BF16), pltpu.VMEM((seq, 128), BF16),
                        pltpu.VMEM((seq, 128), F32), pltpu.VMEM((seq, 128), F32),
                        pltpu.VMEM((seq, 256), F32)],
        compiler_params=pltpu.CompilerParams(
            dimension_semantics=("arbitrary", "arbitrary"),
            vmem_limit_bytes=V7X_VMEM_LIMIT_BYTES),
        name="gla",
    )(gd, up_bd, gb_bd, gq, gk, gv, gr, norm_w.reshape(1, -1))


def _outproj_kernel(d_ref, g_ref, x_ref, w_ref, lg_ref, lb_ref, o_ref):
    mix = _dot(d_ref[...], w_ref[0:DIFF_WIDTH, :]) + _dot(g_ref[...], w_ref[DIFF_WIDTH:D_MODEL, :])
    o_ref[...] = _layer_norm(ALPHA * x_ref[...] + mix, lg_ref[...], lb_ref[...])


def _outproj(d2d, g2d, x2d, w_o, ln_g, ln_b, *, tm=512):
    t = x2d.shape[0]
    row = lambda n: pl.BlockSpec((tm, n), lambda i: (i, 0))
    const = lambda shape: pl.BlockSpec(shape, lambda i: (0,) * len(shape))
    return pl.pallas_call(
        _outproj_kernel,
        out_shape=jax.ShapeDtypeStruct((t, D_MODEL), F32),
        grid=(t // tm,),
        in_specs=[row(DIFF_WIDTH), row(GLA_WIDTH), row(D_MODEL), const((D_MODEL, D_MODEL)),
                  const((1, D_MODEL)), const((1, D_MODEL))],
        out_specs=row(D_MODEL),
        compiler_params=pltpu.CompilerParams(
            dimension_semantics=("arbitrary",), vmem_limit_bytes=V7X_VMEM_LIMIT_BYTES),
        name="outproj_ln",
    )(d2d, g2d, x2d, w_o, ln_g.reshape(1, -1), ln_b.reshape(1, -1))


def _mlp_kernel(x_ref, w1_ref, b1_ref, w2_ref, b2_ref, lg_ref, lb_ref, o_ref, *, ff_chunk):
    x = x_ref[...]
    xb = x.astype(BF16)
    acc = jnp.zeros(x.shape, F32)
    for c0 in range(0, D_FF, ff_chunk):
        h = _dot(xb, w1_ref[:, c0:c0 + ff_chunk]) + b1_ref[:, c0:c0 + ff_chunk]
        h = jnp.square(jnp.maximum(h, 0.0)).astype(BF16)
        acc = acc + _dot(h, w2_ref[c0:c0 + ff_chunk, :])
    o_ref[...] = _layer_norm(ALPHA * x + (acc + b2_ref[...]), lg_ref[...], lb_ref[...])


def _mlp(x2d, w1, b1, w2, b2, ln_g, ln_b, *, tm=512, ff_chunk=1024):
    t = x2d.shape[0]
    row = lambda n: pl.BlockSpec((tm, n), lambda i: (i, 0))
    const = lambda shape: pl.BlockSpec(shape, lambda i: (0,) * len(shape))
    return pl.pallas_call(
        functools.partial(_mlp_kernel, ff_chunk=ff_chunk),
        out_shape=jax.ShapeDtypeStruct((t, D_MODEL), F32),
        grid=(t // tm,),
        in_specs=[row(D_MODEL), const((D_MODEL, D_FF)), const((1, D_FF)), const((D_FF, D_MODEL)),
                  const((1, D_MODEL)), const((1, D_MODEL)), const((1, D_MODEL))],
        out_specs=row(D_MODEL),
        compiler_params=pltpu.CompilerParams(
            dimension_semantics=("arbitrary",), vmem_limit_bytes=V7X_VMEM_LIMIT_BYTES),
        name="mlp_ln",
    )(x2d, w1, b1.reshape(1, -1), w2, b2.reshape(1, -1), ln_g.reshape(1, -1), ln_b.reshape(1, -1))


def _gate_weights(gate_up, gate_bias):
    hp = GLA_HEADS // 2
    up = gate_up.reshape(2, GLA_GATE_RANK, hp, 128)
    z = jnp.zeros((GLA_GATE_RANK, hp, 128), gate_up.dtype)
    top = jnp.concatenate([up[0], z], axis=-1)
    bot = jnp.concatenate([z, up[1]], axis=-1)
    up_bd = jnp.concatenate([top, bot], axis=0)
    up_bd = up_bd.transpose(1, 0, 2)[:, None].astype(BF16)
    gb = gate_bias.reshape(2, hp, 128)
    gb_bd = jnp.concatenate([gb[0], gb[1]], axis=-1)[:, None, None]
    return up_bd, gb_bd


def kernel(x, ln_emb_g, ln_emb_b, rel_bias_table, w_in, lambda_q1, lambda_k1, lambda_q2, lambda_k2,
           diff_norm_w, gla_gate_up, gla_gate_bias, gla_norm_w, w_o, ln1_g, ln1_b,
           w_ffn1, b_ffn1, w_ffn2, b_ffn2, ln2_g, ln2_b):
    b, seq, _ = x.shape
    t = b * seq
    tq, sub = 512, 256
    band = _bias_band(rel_bias_table, sub, seq)
    w_in_b = jnp.pad(w_in.astype(BF16), ((0, 0), (0, 0), (0, D_IN_PAD - D_IN)))
    w_o_b = w_o.astype(BF16)
    w1_b = w_ffn1.astype(BF16)
    w2_b = w_ffn2.astype(BF16)

    h = x.reshape(t, D_MODEL)
    for li in range(DEPTH):
        if li == 0:
            h, qd, kd, vd, gq, gk, gv, gr, gd = _inproj(h, w_in_b[li], ln_emb_g, ln_emb_b)
        else:
            qd, kd, vd, gq, gk, gv, gr, gd = _inproj(h, w_in_b[li])
        r3 = lambda a: a.reshape(b, seq, a.shape[-1])
        lam_init = 0.8 - 0.6 * math.exp(-0.3 * li)
        d_out = _diff_attention(r3(qd), r3(kd), r3(vd), band, lambda_q1[li], lambda_k1[li],
                                lambda_q2[li], lambda_k2[li], diff_norm_w[li],
                                lam_init=lam_init, tq=tq, sub=sub)
        up_bd, gb_bd = _gate_weights(gla_gate_up[li], gla_gate_bias[li])
        g_out = _gla(r3(gd), up_bd, gb_bd, r3(gq), r3(gk), r3(gv), r3(gr), gla_norm_w[li])
        h = _outproj(d_out.reshape(t, DIFF_WIDTH), g_out.reshape(t, GLA_WIDTH), h, w_o_b[li],
                     ln1_g[li], ln1_b[li])
        h = _mlp(h, w1_b[li], b_ffn1[li], w2_b[li], b_ffn2[li], ln2_g[li], ln2_b[li])
    return h.reshape(b, seq, D_MODEL)
```

```python
import functools
import math

import jax
import jax.numpy as jnp
from jax import lax
from jax.experimental import pallas as pl
from jax.experimental.pallas import tpu as pltpu

D_MODEL = 1024
DEPTH = 2
DIFF_HEADS = 4
DIFF_QK_DIM = 64
DIFF_V_DIM = 128
DIFF_WIDTH = 512
GLA_HEADS = 4
GLA_WIDTH = 512
GLA_V_DIM = 128
GLA_K_DIM = 64
GLA_KEY_WIDTH = 256
GLA_GATE_RANK = 16
GLA_GATE_TAU = 16.0
GLA_CHUNK = 64
D_FF = 4096
N_BUCKETS = 32
LN_EPS = 1e-5
RMS_EPS = 1e-5
ALPHA = (2.0 * DEPTH) ** 0.25
LOG2E = math.log2(math.e)
D_IN = 3104
D_IN_PAD = 3200

V7X_VMEM_LIMIT_BYTES = 56 * 1024 * 1024

BF16 = jnp.bfloat16
F32 = jnp.float32

_NT = (((1,), (1,)), ((), ()))
_TN = (((0,), (0,)), ((), ()))


def _dot(a, b):
    return jnp.dot(a, b, preferred_element_type=F32)


def _layer_norm(y, g, b):
    mu = jnp.mean(y, axis=-1, keepdims=True)
    d = y - mu
    var = jnp.mean(d * d, axis=-1, keepdims=True)
    return d * lax.rsqrt(var + LN_EPS) * g + b


def _band_kernel(table_ref, band_ref, *, tq, seq):
    h = pl.program_id(0)
    width = 2 * seq - tq
    r = lax.broadcasted_iota(jnp.int32, (tq, width), 0)
    m = lax.broadcasted_iota(jnp.int32, (tq, width), 1)
    rel = m - (seq - tq) - r
    n = jnp.abs(rel)
    n2 = n * n
    large = jnp.full_like(n, 8)
    for p in range(7, 14):
        large = large + jnp.where(n2 >= (1 << p), 1, 0)
    bucket = jnp.where(n < 8, n, large) + jnp.where(rel > 0, 16, 0)
    acc = jnp.zeros((tq, width), F32)
    for i in range(N_BUCKETS):
        acc = jnp.where(bucket == i, table_ref[i * DIFF_HEADS + h], acc)
    band_ref[0] = acc * LOG2E


def _bias_band(table, tq, seq):
    width = 2 * seq - tq
    return pl.pallas_call(
        functools.partial(_band_kernel, tq=tq, seq=seq),
        out_shape=jax.ShapeDtypeStruct((DIFF_HEADS, tq, width), F32),
        grid=(DIFF_HEADS,),
        in_specs=[pl.BlockSpec(memory_space=pltpu.SMEM)],
        out_specs=pl.BlockSpec((1, tq, width), lambda h: (h, 0, 0)),
        name="bias_band",
    )(table.reshape(-1))


def _inproj_kernel(*refs, apply_ln):
    if apply_ln:
        (x_ref, g_ref, b_ref, w_ref, h0_ref,
         qd_ref, kd_ref, vd_ref, gq_ref, gk_ref, gv_ref, gr_ref, gd_ref) = refs
        xn = _layer_norm(x_ref[...], g_ref[...], b_ref[...])
        h0_ref[...] = xn
    else:
        (x_ref, w_ref,
         qd_ref, kd_ref, vd_ref, gq_ref, gk_ref, gv_ref, gr_ref, gd_ref) = refs
        xn = x_ref[...]
    xb = xn.astype(BF16)
    qd_ref[...] = (_dot(xb, w_ref[:, 0:512]) * (DIFF_QK_DIM ** -0.5 * LOG2E)).astype(BF16)
    kd_ref[...] = _dot(xb, w_ref[:, 512:1024]).astype(BF16)
    tm = xb.shape[0]
    e0 = (lax.broadcasted_iota(jnp.int32, (tm, DIFF_V_DIM), 1) == 0).astype(BF16)
    v = _dot(xb, w_ref[:, 1024:1536]).astype(BF16)
    for hd in range(DIFF_HEADS):
        vd_ref[:, 2 * hd * DIFF_V_DIM:(2 * hd + 1) * DIFF_V_DIM] = v[:, hd * DIFF_V_DIM:(hd + 1) * DIFF_V_DIM]
        vd_ref[:, (2 * hd + 1) * DIFF_V_DIM:(2 * hd + 2) * DIFF_V_DIM] = e0
    gq_ref[...] = _dot(xb, w_ref[:, 1536:1792]) * (GLA_K_DIM ** -0.5)
    gk_ref[...] = _dot(xb, w_ref[:, 1792:2048])
    gv_ref[...] = _dot(xb, w_ref[:, 2048:2560]).astype(BF16)
    gr_ref[...] = _dot(xb, w_ref[:, 2560:3072])
    gd_ref[...] = _dot(xb, w_ref[:, 3072:3200])[:, :2 * GLA_GATE_RANK]


def _inproj(x2d, w_pad, ln_g=None, ln_b=None, *, tm=512):
    t = x2d.shape[0]
    apply_ln = ln_g is not None
    row = lambda n: pl.BlockSpec((tm, n), lambda i: (i, 0))
    const = lambda shape: pl.BlockSpec(shape, lambda i: (0,) * len(shape))
    in_specs = [row(D_MODEL)]
    args = [x2d]
    if apply_ln:
        in_specs += [const((1, D_MODEL)), const((1, D_MODEL))]
        args += [ln_g.reshape(1, -1), ln_b.reshape(1, -1)]
    in_specs.append(const((D_MODEL, D_IN_PAD)))
    args.append(w_pad)
    widths = [(512, BF16), (512, BF16), (2 * DIFF_WIDTH, BF16), (256, F32), (256, F32),
              (512, BF16), (512, F32), (2 * GLA_GATE_RANK, F32)]
    out_shape = [jax.ShapeDtypeStruct((t, n), dt) for n, dt in widths]
    out_specs = [row(n) for n, _ in widths]
    if apply_ln:
        out_shape = [jax.ShapeDtypeStruct((t, D_MODEL), F32)] + out_shape
        out_specs = [row(D_MODEL)] + out_specs
    return pl.pallas_call(
        functools.partial(_inproj_kernel, apply_ln=apply_ln),
        out_shape=out_shape,
        grid=(t // tm,),
        in_specs=in_specs,
        out_specs=out_specs,
        compiler_params=pltpu.CompilerParams(
            dimension_semantics=("arbitrary",), vmem_limit_bytes=V7X_VMEM_LIMIT_BYTES),
        name="ln_inproj" if apply_ln else "inproj",
    )(*args)


def _attn_kernel(lq1_ref, lk1_ref, lq2_ref, lk2_ref, nw_ref, q_ref, k_ref, v_ref, band_ref,
                 o_ref, s_s, e_s, *, lam_init, tq, sub, seq, tk, rc):
    qi = pl.program_id(2)
    lam = (jnp.exp(jnp.sum(lq1_ref[...] * lk1_ref[...], axis=-1, keepdims=True))
           - jnp.exp(jnp.sum(lq2_ref[...] * lk2_ref[...], axis=-1, keepdims=True)) + lam_init)
    first = lax.broadcasted_iota(jnp.int32, (1, 2 * DIFF_QK_DIM), 1) < DIFF_QK_DIM
    for sb in range(tq // sub):
        rows = slice(sb * sub, (sb + 1) * sub)
        q = q_ref[0, rows, :]
        zero = jnp.zeros_like(q)
        qq = jnp.concatenate([jnp.where(first, q, zero), jnp.where(first, zero, q)], axis=0)
        off = pl.multiple_of(seq - sub - (qi * tq + sb * sub), 128)
        for j in range(seq // tk):
            cols = slice(j * tk, (j + 1) * tk)
            s_s[sb, :, cols] = lax.dot_general(qq, k_ref[0, cols, :], _NT, preferred_element_type=F32)
        oo = []
        for mp in range(2):
            base = mp * sub
            ms = []
            for r in range(sub // rc):
                bias = band_ref[0, r * rc:(r + 1) * rc, pl.ds(off, seq)]
                sbias = s_s[sb, base + r * rc:base + (r + 1) * rc, :] + bias
                ms.append(jnp.max(sbias, axis=-1, keepdims=True))
            for r in range(sub // rc):
                bias = band_ref[0, r * rc:(r + 1) * rc, pl.ds(off, seq)]
                rr = slice(base + r * rc, base + (r + 1) * rc)
                e_s[sb, rr, :] = jnp.exp2((s_s[sb, rr, :] - ms[r]) + bias).astype(BF16)
            oo.append(_dot(e_s[sb, base:base + sub, :], v_ref[0]))
        r1 = 1.0 / oo[0][:, DIFF_V_DIM:DIFF_V_DIM + 1]
        r2 = lam / oo[1][:, DIFF_V_DIM:DIFF_V_DIM + 1]
        o = oo[0][:, :DIFF_V_DIM] * r1 - oo[1][:, :DIFF_V_DIM] * r2
        ms = jnp.mean(o * o, axis=-1, keepdims=True)
        y = o * lax.rsqrt(ms + RMS_EPS) * nw_ref[...] * (1.0 - lam_init)
        o_ref[0, rows, :] = y.astype(o_ref.dtype)


def _diff_attention(qd, kd, vd, band, lq1, lk1, lq2, lk2, norm_w, *, lam_init, tq, sub):
    b, seq, _ = qd.shape
    nq = seq // tq
    vec = lambda n: pl.BlockSpec((1, n), lambda h, bi, qi: (0, 0))
    return pl.pallas_call(
        functools.partial(_attn_kernel, lam_init=lam_init, tq=tq, sub=sub, seq=seq, tk=512, rc=16),
        out_shape=jax.ShapeDtypeStruct((b, seq, DIFF_WIDTH), BF16),
        grid=(DIFF_HEADS, b, nq),
        in_specs=[vec(DIFF_QK_DIM), vec(DIFF_QK_DIM), vec(DIFF_QK_DIM), vec(DIFF_QK_DIM),
                  vec(DIFF_V_DIM),
                  pl.BlockSpec((1, tq, 128), lambda h, bi, qi: (bi, qi, h)),
                  pl.BlockSpec((1, seq, 128), lambda h, bi, qi: (bi, 0, h)),
                  pl.BlockSpec((1, seq, 256), lambda h, bi, qi: (bi, 0, h)),
                  pl.BlockSpec((1, sub, 2 * seq - sub), lambda h, bi, qi: (h, 0, 0))],
        out_specs=pl.BlockSpec((1, tq, 128), lambda h, bi, qi: (bi, qi, h)),
        scratch_shapes=[pltpu.VMEM((tq // sub, 2 * sub, seq), F32),
                        pltpu.VMEM((tq // sub, 2 * sub, seq), BF16)],
        compiler_params=pltpu.CompilerParams(
            dimension_semantics=("arbitrary", "arbitrary", "arbitrary"),
            vmem_limit_bytes=V7X_VMEM_LIMIT_BYTES),
        name="diff_attn",
    )(lq1.reshape(1, -1), lk1.reshape(1, -1), lq2.reshape(1, -1), lk2.reshape(1, -1),
      norm_w.reshape(1, -1), qd, kd, vd, band)


def _chunk_scan(x, row, *, reverse):
    n = x.shape[0]
    d = 1
    while d < GLA_CHUNK:
        if reverse:
            x = x + jnp.where(row < GLA_CHUNK - d, pltpu.roll(x, n - d, 0), 0.0)
        else:
            x = x + jnp.where(row >= d, pltpu.roll(x, d, 0), 0.0)
        d *= 2
    return x


def _gla_kernel(gd_ref, up_ref, gb_ref, q_ref, k_ref, v_ref, r_ref, nw_ref, o_ref,
                qq_s, kk_s, bf_s, bb_s, kv_s, st_s, acc_s, *, seq):
    c = GLA_CHUNK
    nc = seq // c
    z = _dot(gd_ref[0].astype(BF16), up_ref[0, 0]) + gb_ref[0, 0]
    g = (jnp.minimum(z, 0.0) - jnp.log(1.0 + jnp.exp(-jnp.abs(z)))) * (1.0 / GLA_GATE_TAU)
    row = lax.broadcasted_iota(jnp.int32, (seq, 1), 0) & (c - 1)
    bf = _chunk_scan(g[:, :128], row, reverse=False)
    bb = _chunk_scan(g[:, 128:], row, reverse=True)
    q = q_ref[0]
    k = k_ref[0]
    qq_s[:, 0:128] = (q * jnp.exp(bf)).astype(BF16)
    kk_s[:, 0:128] = (k * jnp.exp(-bf)).astype(BF16)
    qq_s[:, 128:256] = (q * jnp.exp(bb)).astype(BF16)
    kk_s[:, 128:256] = (k * jnp.exp(-bb)).astype(BF16)
    bf_s[...] = bf
    bb_s[...] = bb

    for n in range(nc):
        rows = slice(n * c, (n + 1) * c)
        kv_s[n] = lax.dot_general(v_ref[0, rows, :], kk_s[rows, :], _TN, preferred_element_type=F32)

    sr = lax.broadcasted_iota(jnp.int32, (256, 128), 0) < GLA_V_DIM
    sc = lax.broadcasted_iota(jnp.int32, (256, 128), 1) < GLA_K_DIM
    same_head = sr == sc
    sf = jnp.zeros((256, 128), F32)
    sb = jnp.zeros((256, 128), F32)
    for i in range(nc):
        st_s[i, :, 0:128] = sf.astype(BF16)
        dec_f = jnp.exp(bf_s[i * c + c - 1:i * c + c, :])
        sf = dec_f * (sf + jnp.where(same_head, kv_s[i, :, 0:128], 0.0))
        n = nc - 1 - i
        st_s[n, :, 128:256] = sb.astype(BF16)
        dec_b = jnp.exp(bb_s[n * c:n * c + 1, :])
        sb = dec_b * (sb + jnp.where(same_head, kv_s[n, :, 128:256], 0.0))

    lane = lax.broadcasted_iota(jnp.int32, (1, 256), 1)
    head0_v = lane < GLA_V_DIM
    quarter = [(lane >= i * GLA_K_DIM) & (lane < (i + 1) * GLA_K_DIM) for i in range(4)]
    ci = lax.broadcasted_iota(jnp.int32, (c, 128), 0)
    si = lax.broadcasted_iota(jnp.int32, (c, 128), 1) & (c - 1)
    causal = si <= ci
    anti = si > ci
    for n in range(nc):
        rows = slice(n * c, (n + 1) * c)
        qn = qq_s[rows, :]
        kn = kk_s[rows, :]
        vn = v_ref[0, rows, :]
        zk = jnp.zeros_like(kn)
        kbd = jnp.concatenate([jnp.where(m, kn, zk) for m in quarter], axis=0)
        scores = lax.dot_general(qn, kbd, _NT, preferred_element_type=F32)
        p = (jnp.where(causal, scores[:, 0:128], 0.0)
             + jnp.where(anti, scores[:, 128:256], 0.0)).astype(BF16)
        zv = jnp.zeros_like(vn)
        vbd = jnp.concatenate([jnp.where(head0_v, vn, zv), jnp.where(head0_v, zv, vn)], axis=0)
        acc_s[rows, :] = (_dot(p, vbd)
                          + lax.dot_general(qn, st_s[n], _NT, preferred_element_type=F32))

    nw = nw_ref[...]
    for hh in range(2):
        sl = slice(hh * GLA_V_DIM, (hh + 1) * GLA_V_DIM)
        oh = acc_s[:, sl]
        y = oh * lax.rsqrt(jnp.mean(oh * oh, axis=-1, keepdims=True) + RMS_EPS) * nw
        gate = r_ref[0, :, sl]
        o_ref[0, :, sl] = (y * (gate * jax.nn.sigmoid(gate))).astype(o_ref.dtype)


def _gla(gd, up_bd, gb_bd, gq, gk, gv, gr, norm_w):
    b, seq, _ = gq.shape
    hp = GLA_HEADS // 2
    return pl.pallas_call(
        functools.partial(_gla_kernel, seq=seq),
        out_shape=jax.ShapeDtypeStruct((b, seq, GLA_WIDTH), BF16),
        grid=(b, hp),
        in_specs=[pl.BlockSpec((1, seq, 2 * GLA_GATE_RANK), lambda bi, p: (bi, 0, 0)),
                  pl.BlockSpec((1, 1, 2 * GLA_GATE_RANK, 256), lambda bi, p: (p, 0, 0, 0)),
                  pl.BlockSpec((1, 1, 1, 256), lambda bi, p: (p, 0, 0, 0)),
                  pl.BlockSpec((1, seq, 128), lambda bi, p: (bi, 0, p)),
                  pl.BlockSpec((1, seq, 128), lambda bi, p: (bi, 0, p)),
                  pl.BlockSpec((1, seq, 256), lambda bi, p: (bi, 0, p)),
                  pl.BlockSpec((1, seq, 256), lambda bi, p: (bi, 0, p)),
                  pl.BlockSpec((1, GLA_V_DIM), lambda bi, p: (0, 0))],
        out_specs=pl.BlockSpec((1, seq, 256), lambda bi, p: (bi, 0, p)),
        scratch_shapes=[pltpu.VMEM((seq, 256), BF16), pltpu.VMEM((seq, 256), BF16),
                        pltpu.VMEM((seq, 128), F32), pltpu.VMEM((seq, 128), F32),
                        pltpu.VMEM((seq // GLA_CHUNK, 256, 256), F32),
                        pltpu.VMEM((seq // GLA_CHUNK, 256, 256), BF16),
                        pltpu.VMEM((seq, 256), F32)],
        compiler_params=pltpu.CompilerParams(
            dimension_semantics=("arbitrary", "arbitrary"),
            vmem_limit_bytes=V7X_VMEM_LIMIT_BYTES),
        name="gla",
    )(gd, up_bd, gb_bd, gq, gk, gv, gr, norm_w.reshape(1, -1))


def _outproj_kernel(d_ref, g_ref, x_ref, w_ref, lg_ref, lb_ref, o_ref):
    mix = _dot(d_ref[...], w_ref[0:DIFF_WIDTH, :]) + _dot(g_ref[...], w_ref[DIFF_WIDTH:D_MODEL, :])
    o_ref[...] = _layer_norm(ALPHA * x_ref[...] + mix, lg_ref[...], lb_ref[...])


def _outproj(d2d, g2d, x2d, w_o, ln_g, ln_b, *, tm=512):
    t = x2d.shape[0]
    row = lambda n: pl.BlockSpec((tm, n), lambda i: (i, 0))
    const = lambda shape: pl.BlockSpec(shape, lambda i: (0,) * len(shape))
    return pl.pallas_call(
        _outproj_kernel,
        out_shape=jax.ShapeDtypeStruct((t, D_MODEL), F32),
        grid=(t // tm,),
        in_specs=[row(DIFF_WIDTH), row(GLA_WIDTH), row(D_MODEL), const((D_MODEL, D_MODEL)),
                  const((1, D_MODEL)), const((1, D_MODEL))],
        out_specs=row(D_MODEL),
        compiler_params=pltpu.CompilerParams(
            dimension_semantics=("arbitrary",), vmem_limit_bytes=V7X_VMEM_LIMIT_BYTES),
        name="outproj_ln",
    )(d2d, g2d, x2d, w_o, ln_g.reshape(1, -1), ln_b.reshape(1, -1))


def _mlp_kernel(x_ref, w1_ref, b1_ref, w2_ref, b2_ref, lg_ref, lb_ref, o_ref, *, ff_chunk):
    x = x_ref[...]
    xb = x.astype(BF16)
    acc = jnp.zeros(x.shape, F32)
    for c0 in range(0, D_FF, ff_chunk):
        h = _dot(xb, w1_ref[:, c0:c0 + ff_chunk]) + b1_ref[:, c0:c0 + ff_chunk]
        h = jnp.square(jnp.maximum(h, 0.0)).astype(BF16)
        acc = acc + _dot(h, w2_ref[c0:c0 + ff_chunk, :])
    o_ref[...] = _layer_norm(ALPHA * x + (acc + b2_ref[...]), lg_ref[...], lb_ref[...])


def _mlp(x2d, w1, b1, w2, b2, ln_g, ln_b, *, tm=512, ff_chunk=1024):
    t = x2d.shape[0]
    row = lambda n: pl.BlockSpec((tm, n), lambda i: (i, 0))
    const = lambda shape: pl.BlockSpec(shape, lambda i: (0,) * len(shape))
    return pl.pallas_call(
        functools.partial(_mlp_kernel, ff_chunk=ff_chunk),
        out_shape=jax.ShapeDtypeStruct((t, D_MODEL), F32),
        grid=(t // tm,),
        in_specs=[row(D_MODEL), const((D_MODEL, D_FF)), const((1, D_FF)), const((D_FF, D_MODEL)),
                  const((1, D_MODEL)), const((1, D_MODEL)), const((1, D_MODEL))],
        out_specs=row(D_MODEL),
        compiler_params=pltpu.CompilerParams(
            dimension_semantics=("arbitrary",), vmem_limit_bytes=V7X_VMEM_LIMIT_BYTES),
        name="mlp_ln",
    )(x2d, w1, b1.reshape(1, -1), w2, b2.reshape(1, -1), ln_g.reshape(1, -1), ln_b.reshape(1, -1))


def _gate_weights(gate_up, gate_bias):
    hp = GLA_HEADS // 2
    up = gate_up.reshape(2, GLA_GATE_RANK, hp, 128)
    z = jnp.zeros((GLA_GATE_RANK, hp, 128), gate_up.dtype)
    top = jnp.concatenate([up[0], z], axis=-1)
    bot = jnp.concatenate([z, up[1]], axis=-1)
    up_bd = jnp.concatenate([top, bot], axis=0)
    up_bd = up_bd.transpose(1, 0, 2)[:, None].astype(BF16)
    gb = gate_bias.reshape(2, hp, 128)
    gb_bd = jnp.concatenate([gb[0], gb[1]], axis=-1)[:, None, None]
    return up_bd, gb_bd


def kernel(x, ln_emb_g, ln_emb_b, rel_bias_table, w_in, lambda_q1, lambda_k1, lambda_q2, lambda_k2,
           diff_norm_w, gla_gate_up, gla_gate_bias, gla_norm_w, w_o, ln1_g, ln1_b,
           w_ffn1, b_ffn1, w_ffn2, b_ffn2, ln2_g, ln2_b):
    b, seq, _ = x.shape
    t = b * seq
    tq, sub = 512, 256
    band = _bias_band(rel_bias_table, sub, seq)
    w_in_b = jnp.pad(w_in.astype(BF16), ((0, 0), (0, 0), (0, D_IN_PAD - D_IN)))
    w_o_b = w_o.astype(BF16)
    w1_b = w_ffn1.astype(BF16)
    w2_b = w_ffn2.astype(BF16)

    h = x.reshape(t, D_MODEL)
    for li in range(DEPTH):
        if li == 0:
            h, qd, kd, vd, gq, gk, gv, gr, gd = _inproj(h, w_in_b[li], ln_emb_g, ln_emb_b)
        else:
            qd, kd, vd, gq, gk, gv, gr, gd = _inproj(h, w_in_b[li])
        r3 = lambda a: a.reshape(b, seq, a.shape[-1])
        lam_init = 0.8 - 0.6 * math.exp(-0.3 * li)
        d_out = _diff_attention(r3(qd), r3(kd), r3(vd), band, lambda_q1[li], lambda_k1[li],
                                lambda_q2[li], lambda_k2[li], diff_norm_w[li],
                                lam_init=lam_init, tq=tq, sub=sub)
        up_bd, gb_bd = _gate_weights(gla_gate_up[li], gla_gate_bias[li])
        g_out = _gla(r3(gd), up_bd, gb_bd, r3(gq), r3(gk), r3(gv), r3(gr), gla_norm_w[li])
        h = _outproj(d_out.reshape(t, DIFF_WIDTH), g_out.reshape(t, GLA_WIDTH), h, w_o_b[li],
                     ln1_g[li], ln1_b[li])
        h = _mlp(h, w1_b[li], b_ffn1[li], w2_b[li], b_ffn2[li], ln2_g[li], ln2_b[li])
    return h.reshape(b, seq, D_MODEL)
```

```python
import functools
import math

import jax
import jax.numpy as jnp
from jax import lax
from jax.experimental import pallas as pl
from jax.experimental.pallas import tpu as pltpu

D_MODEL = 1024
DEPTH = 2
DIFF_HEADS = 4
DIFF_QK_DIM = 64
DIFF_V_DIM = 128
DIFF_WIDTH = 512
GLA_HEADS = 4
GLA_WIDTH = 512
GLA_V_DIM = 128
GLA_K_DIM = 64
GLA_KEY_WIDTH = 256
GLA_GATE_RANK = 16
GLA_GATE_TAU = 16.0
GLA_CHUNK = 64
D_FF = 4096
N_BUCKETS = 32
LN_EPS = 1e-5
RMS_EPS = 1e-5
ALPHA = (2.0 * DEPTH) ** 0.25
LOG2E = math.log2(math.e)
D_IN = 3104
D_IN_PAD = 3200

V7X_VMEM_LIMIT_BYTES = 56 * 1024 * 1024

BF16 = jnp.bfloat16
F32 = jnp.float32

_NT = (((1,), (1,)), ((), ()))
_TN = (((0,), (0,)), ((), ()))


def _dot(a, b):
    return jnp.dot(a, b, preferred_element_type=F32)


def _layer_norm(y, g, b):
    mu = jnp.mean(y, axis=-1, keepdims=True)
    d = y - mu
    var = jnp.mean(d * d, axis=-1, keepdims=True)
    return d * lax.rsqrt(var + LN_EPS) * g + b


def _band_kernel(table_ref, band_ref, *, tq, seq):
    h = pl.program_id(0)
    width = 2 * seq - tq
    lo, hi = seq - tq - 128, seq + 128
    near = hi - lo
    r = lax.broadcasted_iota(jnp.int32, (tq, near), 0)
    m = lax.broadcasted_iota(jnp.int32, (tq, near), 1) + lo
    rel = m - (seq - tq) - r
    n = jnp.abs(rel)
    n2 = n * n
    large = jnp.full_like(n, 8)
    for p in range(7, 14):
        large = large + jnp.where(n2 >= (1 << p), 1, 0)
    bucket = jnp.where(n < 8, n, large) + jnp.where(rel > 0, 16, 0)
    acc = jnp.zeros((tq, near), F32)
    for i in range(N_BUCKETS):
        acc = jnp.where(bucket == i, table_ref[i * DIFF_HEADS + h], acc)
    half = N_BUCKETS // 2
    band_ref[0, :, 0:lo] = jnp.full((tq, lo), table_ref[(half - 1) * DIFF_HEADS + h] * LOG2E, F32)
    band_ref[0, :, lo:hi] = acc * LOG2E
    band_ref[0, :, hi:width] = jnp.full((tq, width - hi), table_ref[(N_BUCKETS - 1) * DIFF_HEADS + h] * LOG2E, F32)


def _bias_band(table, tq, seq):
    width = 2 * seq - tq
    return pl.pallas_call(
        functools.partial(_band_kernel, tq=tq, seq=seq),
        out_shape=jax.ShapeDtypeStruct((DIFF_HEADS, tq, width), F32),
        grid=(DIFF_HEADS,),
        in_specs=[pl.BlockSpec(memory_space=pltpu.SMEM)],
        out_specs=pl.BlockSpec((1, tq, width), lambda h: (h, 0, 0)),
        name="bias_band",
    )(table.reshape(-1))


def _inproj_kernel(*refs, apply_ln):
    if apply_ln:
        (x_ref, g_ref, b_ref, w_ref, h0_ref,
         qd_ref, kd_ref, vd_ref, gq_ref, gk_ref, gv_ref, gr_ref, gd_ref) = refs
        xn = _layer_norm(x_ref[...], g_ref[...], b_ref[...])
        h0_ref[...] = xn
    else:
        (x_ref, w_ref,
         qd_ref, kd_ref, vd_ref, gq_ref, gk_ref, gv_ref, gr_ref, gd_ref) = refs
        xn = x_ref[...]
    xb = xn.astype(BF16)
    qd_ref[...] = (_dot(xb, w_ref[:, 0:512]) * (DIFF_QK_DIM ** -0.5 * LOG2E)).astype(BF16)
    kd_ref[...] = _dot(xb, w_ref[:, 512:1024]).astype(BF16)
    tm = xb.shape[0]
    e0 = (lax.broadcasted_iota(jnp.int32, (tm, DIFF_V_DIM), 1) == 0).astype(BF16)
    v = _dot(xb, w_ref[:, 1024:1536]).astype(BF16)
    for hd in range(DIFF_HEADS):
        vd_ref[:, 2 * hd * DIFF_V_DIM:(2 * hd + 1) * DIFF_V_DIM] = v[:, hd * DIFF_V_DIM:(hd + 1) * DIFF_V_DIM]
        vd_ref[:, (2 * hd + 1) * DIFF_V_DIM:(2 * hd + 2) * DIFF_V_DIM] = e0
    gq_ref[...] = _dot(xb, w_ref[:, 1536:1792]) * (GLA_K_DIM ** -0.5)
    gk_ref[...] = _dot(xb, w_ref[:, 1792:2048])
    gv_ref[...] = _dot(xb, w_ref[:, 2048:2560]).astype(BF16)
    gr_ref[...] = _dot(xb, w_ref[:, 2560:3072])
    gd_ref[...] = _dot(xb, w_ref[:, 3072:3200])[:, :2 * GLA_GATE_RANK]


def _inproj(x2d, w_pad, ln_g=None, ln_b=None, *, tm=512):
    t = x2d.shape[0]
    apply_ln = ln_g is not None
    row = lambda n: pl.BlockSpec((tm, n), lambda i: (i, 0))
    const = lambda shape: pl.BlockSpec(shape, lambda i: (0,) * len(shape))
    in_specs = [row(D_MODEL)]
    args = [x2d]
    if apply_ln:
        in_specs += [const((1, D_MODEL)), const((1, D_MODEL))]
        args += [ln_g.reshape(1, -1), ln_b.reshape(1, -1)]
    in_specs.append(const((D_MODEL, D_IN_PAD)))
    args.append(w_pad)
    widths = [(512, BF16), (512, BF16), (2 * DIFF_WIDTH, BF16), (256, F32), (256, F32),
              (512, BF16), (512, F32), (2 * GLA_GATE_RANK, F32)]
    out_shape = [jax.ShapeDtypeStruct((t, n), dt) for n, dt in widths]
    out_specs = [row(n) for n, _ in widths]
    if apply_ln:
        out_shape = [jax.ShapeDtypeStruct((t, D_MODEL), F32)] + out_shape
        out_specs = [row(D_MODEL)] + out_specs
    return pl.pallas_call(
        functools.partial(_inproj_kernel, apply_ln=apply_ln),
        out_shape=out_shape,
        grid=(t // tm,),
        in_specs=in_specs,
        out_specs=out_specs,
        compiler_params=pltpu.CompilerParams(
            dimension_semantics=("arbitrary",), vmem_limit_bytes=V7X_VMEM_LIMIT_BYTES),
        name="ln_inproj" if apply_ln else "inproj",
    )(*args)


def _attn_kernel(lq1_ref, lk1_ref, lq2_ref, lk2_ref, nw_ref, q_ref, k_ref, v_ref, band_ref,
                 o_ref, s_s, e_s, *, lam_init, tq, sub, seq, tk, rc):
    qi = pl.program_id(2)
    lam = (jnp.exp(jnp.sum(lq1_ref[...] * lk1_ref[...], axis=-1, keepdims=True))
           - jnp.exp(jnp.sum(lq2_ref[...] * lk2_ref[...], axis=-1, keepdims=True)) + lam_init)
    first = lax.broadcasted_iota(jnp.int32, (1, 2 * DIFF_QK_DIM), 1) < DIFF_QK_DIM
    nsb = tq // sub

    def logits(sb):
        q = q_ref[0, sb * sub:(sb + 1) * sub, :]
        zero = jnp.zeros_like(q)
        qq = jnp.concatenate([jnp.where(first, q, zero), jnp.where(first, zero, q)], axis=0)
        for j in range(seq // tk):
            cols = slice(j * tk, (j + 1) * tk)
            s_s[sb, :, cols] = lax.dot_general(qq, k_ref[0, cols, :], _NT, preferred_element_type=F32)

    def softmax_pv(sb):
        off = pl.multiple_of(seq - sub - (qi * tq + sb * sub), 128)
        ms = []
        for r in range(sub // rc):
            bias = band_ref[0, r * rc:(r + 1) * rc, pl.ds(off, seq)]
            for mp in range(2):
                rr = slice(mp * sub + r * rc, mp * sub + (r + 1) * rc)
                ms.append(jnp.max(s_s[sb, rr, :] + bias, axis=-1, keepdims=True))
        for r in range(sub // rc):
            bias = band_ref[0, r * rc:(r + 1) * rc, pl.ds(off, seq)]
            for mp in range(2):
                rr = slice(mp * sub + r * rc, mp * sub + (r + 1) * rc)
                e_s[sb, rr, :] = jnp.exp2((s_s[sb, rr, :] - ms[2 * r + mp]) + bias).astype(BF16)
        oo = [_dot(e_s[sb, mp * sub:(mp + 1) * sub, :], v_ref[0]) for mp in range(2)]
        r1 = 1.0 / oo[0][:, DIFF_V_DIM:DIFF_V_DIM + 1]
        r2 = lam / oo[1][:, DIFF_V_DIM:DIFF_V_DIM + 1]
        o = oo[0][:, :DIFF_V_DIM] * r1 - oo[1][:, :DIFF_V_DIM] * r2
        y = o * lax.rsqrt(jnp.mean(o * o, axis=-1, keepdims=True) + RMS_EPS) * nw_ref[...] * (1.0 - lam_init)
        o_ref[0, sb * sub:(sb + 1) * sub, :] = y.astype(o_ref.dtype)

    logits(0)
    for sb in range(nsb):
        if sb + 1 < nsb:
            logits(sb + 1)
        softmax_pv(sb)


def _diff_attention(qd, kd, vd, band, lq1, lk1, lq2, lk2, norm_w, *, lam_init, tq, sub):
    b, seq, _ = qd.shape
    nq = seq // tq
    vec = lambda n: pl.BlockSpec((1, n), lambda h, bi, qi: (0, 0))
    return pl.pallas_call(
        functools.partial(_attn_kernel, lam_init=lam_init, tq=tq, sub=sub, seq=seq, tk=512, rc=16),
        out_shape=jax.ShapeDtypeStruct((b, seq, DIFF_WIDTH), BF16),
        grid=(DIFF_HEADS, b, nq),
        in_specs=[vec(DIFF_QK_DIM), vec(DIFF_QK_DIM), vec(DIFF_QK_DIM), vec(DIFF_QK_DIM),
                  vec(DIFF_V_DIM),
                  pl.BlockSpec((1, tq, 128), lambda h, bi, qi: (bi, qi, h)),
                  pl.BlockSpec((1, seq, 128), lambda h, bi, qi: (bi, 0, h)),
                  pl.BlockSpec((1, seq, 256), lambda h, bi, qi: (bi, 0, h)),
                  pl.BlockSpec((1, sub, 2 * seq - sub), lambda h, bi, qi: (h, 0, 0))],
        out_specs=pl.BlockSpec((1, tq, 128), lambda h, bi, qi: (bi, qi, h)),
        scratch_shapes=[pltpu.VMEM((tq // sub, 2 * sub, seq), F32),
                        pltpu.VMEM((tq // sub, 2 * sub, seq), BF16)],
        compiler_params=pltpu.CompilerParams(
            dimension_semantics=("arbitrary", "arbitrary", "arbitrary"),
            vmem_limit_bytes=V7X_VMEM_LIMIT_BYTES),
        name="diff_attn",
    )(lq1.reshape(1, -1), lk1.reshape(1, -1), lq2.reshape(1, -1), lk2.reshape(1, -1),
      norm_w.reshape(1, -1), qd, kd, vd, band)


def _chunk_scan(x, row, *, reverse):
    n = x.shape[0]
    d = 1
    while d < GLA_CHUNK:
        if reverse:
            x = x + jnp.where(row < GLA_CHUNK - d, pltpu.roll(x, n - d, 0), 0.0)
        else:
            x = x + jnp.where(row >= d, pltpu.roll(x, d, 0), 0.0)
        d *= 2
    return x


def _gla_kernel(gd_ref, up_ref, gb_ref, q_ref, k_ref, v_ref, r_ref, nw_ref, o_ref,
                qq_s, kk_s, bf_s, bb_s, kv_s, st_s, acc_s, *, seq):
    c = GLA_CHUNK
    nc = seq // c
    z = _dot(gd_ref[0].astype(BF16), up_ref[0, 0]) + gb_ref[0, 0]
    g = (jnp.minimum(z, 0.0) - jnp.log(1.0 + jnp.exp(-jnp.abs(z)))) * (1.0 / GLA_GATE_TAU)
    row = lax.broadcasted_iota(jnp.int32, (seq, 1), 0) & (c - 1)
    bf = _chunk_scan(g[:, :128], row, reverse=False)
    bb = _chunk_scan(g[:, 128:], row, reverse=True)
    q = q_ref[0]
    k = k_ref[0]
    qq_s[:, 0:128] = (q * jnp.exp(bf)).astype(BF16)
    kk_s[:, 0:128] = (k * jnp.exp(-bf)).astype(BF16)
    qq_s[:, 128:256] = (q * jnp.exp(bb)).astype(BF16)
    kk_s[:, 128:256] = (k * jnp.exp(-bb)).astype(BF16)
    bf_s[...] = bf
    bb_s[...] = bb

    for n in range(nc):
        rows = slice(n * c, (n + 1) * c)
        kv_s[n] = lax.dot_general(v_ref[0, rows, :], kk_s[rows, :], _TN, preferred_element_type=F32)

    sr = lax.broadcasted_iota(jnp.int32, (256, 128), 0) < GLA_V_DIM
    sc = lax.broadcasted_iota(jnp.int32, (256, 128), 1) < GLA_K_DIM
    same_head = sr == sc
    sf = jnp.zeros((256, 128), F32)
    sb = jnp.zeros((256, 128), F32)
    for i in range(nc):
        st_s[i, :, 0:128] = sf.astype(BF16)
        dec_f = jnp.exp(bf_s[i * c + c - 1:i * c + c, :])
        sf = dec_f * (sf + jnp.where(same_head, kv_s[i, :, 0:128], 0.0))
        n = nc - 1 - i
        st_s[n, :, 128:256] = sb.astype(BF16)
        dec_b = jnp.exp(bb_s[n * c:n * c + 1, :])
        sb = dec_b * (sb + jnp.where(same_head, kv_s[n, :, 128:256], 0.0))

    lane = lax.broadcasted_iota(jnp.int32, (1, 256), 1)
    head0_v = lane < GLA_V_DIM
    quarter = [(lane >= i * GLA_K_DIM) & (lane < (i + 1) * GLA_K_DIM) for i in range(4)]
    ci = lax.broadcasted_iota(jnp.int32, (c, 128), 0)
    si = lax.broadcasted_iota(jnp.int32, (c, 128), 1) & (c - 1)
    causal = si <= ci
    anti = si > ci
    for n in range(nc):
        rows = slice(n * c, (n + 1) * c)
        qn = qq_s[rows, :]
        kn = kk_s[rows, :]
        vn = v_ref[0, rows, :]
        zk = jnp.zeros_like(kn)
        kbd = jnp.concatenate([jnp.where(m, kn, zk) for m in quarter], axis=0)
        scores = lax.dot_general(qn, kbd, _NT, preferred_element_type=F32)
        p = (jnp.where(causal, scores[:, 0:128], 0.0)
             + jnp.where(anti, scores[:, 128:256], 0.0)).astype(BF16)
        zv = jnp.zeros_like(vn)
        vbd = jnp.concatenate([jnp.where(head0_v, vn, zv), jnp.where(head0_v, zv, vn)], axis=0)
        acc_s[rows, :] = (_dot(p, vbd)
                          + lax.dot_general(qn, st_s[n], _NT, preferred_element_type=F32))

    nw = nw_ref[...]
    for hh in range(2):
        sl = slice(hh * GLA_V_DIM, (hh + 1) * GLA_V_DIM)
        oh = acc_s[:, sl]
        y = oh * lax.rsqrt(jnp.mean(oh * oh, axis=-1, keepdims=True) + RMS_EPS) * nw
        gate = r_ref[0, :, sl]
        o_ref[0, :, sl] = (y * (gate * jax.nn.sigmoid(gate))).astype(o_ref.dtype)


def _gla(gd, up_bd, gb_bd, gq, gk, gv, gr, norm_w):
    b, seq, _ = gq.shape
    hp = GLA_HEADS // 2
    return pl.pallas_call(
        functools.partial(_gla_kernel, seq=seq),
        out_shape=jax.ShapeDtypeStruct((b, seq, GLA_WIDTH), BF16),
        grid=(b, hp),
        in_specs=[pl.BlockSpec((1, seq, 2 * GLA_GATE_RANK), lambda bi, p: (bi, 0, 0)),
                  pl.BlockSpec((1, 1, 2 * GLA_GATE_RANK, 256), lambda bi, p: (p, 0, 0, 0)),
                  pl.BlockSpec((1, 1, 1, 256), lambda bi, p: (p, 0, 0, 0)),
                  pl.BlockSpec((1, seq, 128), lambda bi, p: (bi, 0, p)),
                  pl.BlockSpec((1, seq, 128), lambda bi, p: (bi, 0, p)),
                  pl.BlockSpec((1, seq, 256), lambda bi, p: (bi, 0, p)),
                  pl.BlockSpec((1, seq, 256), lambda bi, p: (bi, 0, p)),
                  pl.BlockSpec((1, GLA_V_DIM), lambda bi, p: (0, 0))],
        out_specs=pl.BlockSpec((1, seq, 256), lambda bi, p: (bi, 0, p)),
        scratch_shapes=[pltpu.VMEM((seq, 256), BF16), pltpu.VMEM((seq, 256), BF16),
                        pltpu.VMEM((seq, 128), F32), pltpu.VMEM((seq, 128), F32),
                        pltpu.VMEM((seq // GLA_CHUNK, 256, 256), F32),
                        pltpu.VMEM((seq // GLA_CHUNK, 256, 256), BF16),
                        pltpu.VMEM((seq, 256), F32)],
        compiler_params=pltpu.CompilerParams(
            dimension_semantics=("arbitrary", "arbitrary"),
            vmem_limit_bytes=V7X_VMEM_LIMIT_BYTES),
        name="gla",
    )(gd, up_bd, gb_bd, gq, gk, gv, gr, norm_w.reshape(1, -1))


def _outproj_kernel(d_ref, g_ref, x_ref, w_ref, lg_ref, lb_ref, o_ref):
    mix = _dot(d_ref[...], w_ref[0:DIFF_WIDTH, :]) + _dot(g_ref[...], w_ref[DIFF_WIDTH:D_MODEL, :])
    o_ref[...] = _layer_norm(ALPHA * x_ref[...] + mix, lg_ref[...], lb_ref[...])


def _outproj(d2d, g2d, x2d, w_o, ln_g, ln_b, *, tm=512):
    t = x2d.shape[0]
    row = lambda n: pl.BlockSpec((tm, n), lambda i: (i, 0))
    const = lambda shape: pl.BlockSpec(shape, lambda i: (0,) * len(shape))
    return pl.pallas_call(
        _outproj_kernel,
        out_shape=jax.ShapeDtypeStruct((t, D_MODEL), F32),
        grid=(t // tm,),
        in_specs=[row(DIFF_WIDTH), row(GLA_WIDTH), row(D_MODEL), const((D_MODEL, D_MODEL)),
                  const((1, D_MODEL)), const((1, D_MODEL))],
        out_specs=row(D_MODEL),
        compiler_params=pltpu.CompilerParams(
            dimension_semantics=("arbitrary",), vmem_limit_bytes=V7X_VMEM_LIMIT_BYTES),
        name="outproj_ln",
    )(d2d, g2d, x2d, w_o, ln_g.reshape(1, -1), ln_b.reshape(1, -1))


def _mlp_kernel(x_ref, w1_ref, b1_ref, w2_ref, b2_ref, lg_ref, lb_ref, o_ref, *, ff_chunk):
    x = x_ref[...]
    xb = x.astype(BF16)
    acc = jnp.zeros(x.shape, F32)
    for c0 in range(0, D_FF, ff_chunk):
        h = _dot(xb, w1_ref[:, c0:c0 + ff_chunk]) + b1_ref[:, c0:c0 + ff_chunk]
        h = jnp.square(jnp.maximum(h, 0.0)).astype(BF16)
        acc = acc + _dot(h, w2_ref[c0:c0 + ff_chunk, :])
    o_ref[...] = _layer_norm(ALPHA * x + (acc + b2_ref[...]), lg_ref[...], lb_ref[...])


def _mlp(x2d, w1, b1, w2, b2, ln_g, ln_b, *, tm=512, ff_chunk=1024):
    t = x2d.shape[0]
    row = lambda n: pl.BlockSpec((tm, n), lambda i: (i, 0))
    const = lambda shape: pl.BlockSpec(shape, lambda i: (0,) * len(shape))
    return pl.pallas_call(
        functools.partial(_mlp_kernel, ff_chunk=ff_chunk),
        out_shape=jax.ShapeDtypeStruct((t, D_MODEL), F32),
        grid=(t // tm,),
        in_specs=[row(D_MODEL), const((D_MODEL, D_FF)), const((1, D_FF)), const((D_FF, D_MODEL)),
                  const((1, D_MODEL)), const((1, D_MODEL)), const((1, D_MODEL))],
        out_specs=row(D_MODEL),
        compiler_params=pltpu.CompilerParams(
            dimension_semantics=("arbitrary",), vmem_limit_bytes=V7X_VMEM_LIMIT_BYTES),
        name="mlp_ln",
    )(x2d, w1, b1.reshape(1, -1), w2, b2.reshape(1, -1), ln_g.reshape(1, -1), ln_b.reshape(1, -1))


def _gate_weights(gate_up, gate_bias):
    hp = GLA_HEADS // 2
    up = gate_up.reshape(2, GLA_GATE_RANK, hp, 128)
    z = jnp.zeros((GLA_GATE_RANK, hp, 128), gate_up.dtype)
    top = jnp.concatenate([up[0], z], axis=-1)
    bot = jnp.concatenate([z, up[1]], axis=-1)
    up_bd = jnp.concatenate([top, bot], axis=0)
    up_bd = up_bd.transpose(1, 0, 2)[:, None].astype(BF16)
    gb = gate_bias.reshape(2, hp, 128)
    gb_bd = jnp.concatenate([gb[0], gb[1]], axis=-1)[:, None, None]
    return up_bd, gb_bd


def kernel(x, ln_emb_g, ln_emb_b, rel_bias_table, w_in, lambda_q1, lambda_k1, lambda_q2, lambda_k2,
           diff_norm_w, gla_gate_up, gla_gate_bias, gla_norm_w, w_o, ln1_g, ln1_b,
           w_ffn1, b_ffn1, w_ffn2, b_ffn2, ln2_g, ln2_b):
    b, seq, _ = x.shape
    t = b * seq
    tq, sub = 1024, 256
    band = _bias_band(rel_bias_table, sub, seq)
    w_in_b = jnp.pad(w_in.astype(BF16), ((0, 0), (0, 0), (0, D_IN_PAD - D_IN)))
    w_o_b = w_o.astype(BF16)
    w1_b = w_ffn1.astype(BF16)
    w2_b = w_ffn2.astype(BF16)

    h = x.reshape(t, D_MODEL)
    for li in range(DEPTH):
        if li == 0:
            h, qd, kd, vd, gq, gk, gv, gr, gd = _inproj(h, w_in_b[li], ln_emb_g, ln_emb_b)
        else:
            qd, kd, vd, gq, gk, gv, gr, gd = _inproj(h, w_in_b[li])
        r3 = lambda a: a.reshape(b, seq, a.shape[-1])
        lam_init = 0.8 - 0.6 * math.exp(-0.3 * li)
        d_out = _diff_attention(r3(qd), r3(kd), r3(vd), band, lambda_q1[li], lambda_k1[li],
                                lambda_q2[li], lambda_k2[li], diff_norm_w[li],
                                lam_init=lam_init, tq=tq, sub=sub)
        up_bd, gb_bd = _gate_weights(gla_gate_up[li], gla_gate_bias[li])
        g_out = _gla(r3(gd), up_bd, gb_bd, r3(gq), r3(gk), r3(gv), r3(gr), gla_norm_w[li])
        h = _outproj(d_out.reshape(t, DIFF_WIDTH), g_out.reshape(t, GLA_WIDTH), h, w_o_b[li],
                     ln1_g[li], ln1_b[li])
        h = _mlp(h, w1_b[li], b_ffn1[li], w2_b[li], b_ffn2[li], ln2_g[li], ln2_b[li])
    return h.reshape(b, seq, D_MODEL)
```

```python
import functools
import math

import jax
import jax.numpy as jnp
from jax import lax
from jax.experimental import pallas as pl
from jax.experimental.pallas import tpu as pltpu

D_MODEL = 1024
DEPTH = 2
DIFF_HEADS = 4
DIFF_QK_DIM = 64
DIFF_V_DIM = 128
DIFF_WIDTH = 512
GLA_HEADS = 4
GLA_WIDTH = 512
GLA_V_DIM = 128
GLA_K_DIM = 64
GLA_KEY_WIDTH = 256
GLA_GATE_RANK = 16
GLA_GATE_TAU = 16.0
GLA_CHUNK = 64
D_FF = 4096
N_BUCKETS = 32
LN_EPS = 1e-5
RMS_EPS = 1e-5
ALPHA = (2.0 * DEPTH) ** 0.25
LOG2E = math.log2(math.e)
D_IN = 3104
D_IN_PAD = 3200

V7X_VMEM_LIMIT_BYTES = 56 * 1024 * 1024

BF16 = jnp.bfloat16
F32 = jnp.float32

_NT = (((1,), (1,)), ((), ()))
_TN = (((0,), (0,)), ((), ()))


def _dot(a, b):
    return jnp.dot(a, b, preferred_element_type=F32)


def _layer_norm(y, g, b):
    mu = jnp.mean(y, axis=-1, keepdims=True)
    d = y - mu
    var = jnp.mean(d * d, axis=-1, keepdims=True)
    return d * lax.rsqrt(var + LN_EPS) * g + b


def _band_kernel(table_ref, band_ref, *, tq, seq):
    h = pl.program_id(0)
    width = 2 * seq - tq
    lo, hi = seq - tq - 128, seq + 128
    near = hi - lo
    r = lax.broadcasted_iota(jnp.int32, (tq, near), 0)
    m = lax.broadcasted_iota(jnp.int32, (tq, near), 1) + lo
    rel = m - (seq - tq) - r
    n = jnp.abs(rel)
    n2 = n * n
    large = jnp.full_like(n, 8)
    for p in range(7, 14):
        large = large + jnp.where(n2 >= (1 << p), 1, 0)
    bucket = jnp.where(n < 8, n, large) + jnp.where(rel > 0, 16, 0)
    acc = jnp.zeros((tq, near), F32)
    for i in range(N_BUCKETS):
        acc = jnp.where(bucket == i, table_ref[i * DIFF_HEADS + h], acc)
    half = N_BUCKETS // 2
    band_ref[0, :, 0:lo] = jnp.full((tq, lo), table_ref[(half - 1) * DIFF_HEADS + h] * LOG2E, F32)
    band_ref[0, :, lo:hi] = acc * LOG2E
    band_ref[0, :, hi:width] = jnp.full((tq, width - hi), table_ref[(N_BUCKETS - 1) * DIFF_HEADS + h] * LOG2E, F32)


def _bias_band(table, tq, seq):
    width = 2 * seq - tq
    return pl.pallas_call(
        functools.partial(_band_kernel, tq=tq, seq=seq),
        out_shape=jax.ShapeDtypeStruct((DIFF_HEADS, tq, width), F32),
        grid=(DIFF_HEADS,),
        in_specs=[pl.BlockSpec(memory_space=pltpu.SMEM)],
        out_specs=pl.BlockSpec((1, tq, width), lambda h: (h, 0, 0)),
        name="bias_band",
    )(table.reshape(-1))


def _inproj_kernel(*refs, apply_ln):
    if apply_ln:
        (x_ref, g_ref, b_ref, w_ref, h0_ref,
         qd_ref, kd_ref, vd_ref, gq_ref, gk_ref, gv_ref, gr_ref, gd_ref) = refs
        xn = _layer_norm(x_ref[...], g_ref[...], b_ref[...])
        h0_ref[...] = xn
    else:
        (x_ref, w_ref,
         qd_ref, kd_ref, vd_ref, gq_ref, gk_ref, gv_ref, gr_ref, gd_ref) = refs
        xn = x_ref[...]
    xb = xn.astype(BF16)
    qd_ref[...] = (_dot(xb, w_ref[0, :,0:512]) * (DIFF_QK_DIM ** -0.5 * LOG2E)).astype(BF16)
    kd_ref[...] = _dot(xb, w_ref[0, :,512:1024]).astype(BF16)
    tm = xb.shape[0]
    e0 = (lax.broadcasted_iota(jnp.int32, (tm, DIFF_V_DIM), 1) == 0).astype(BF16)
    v = _dot(xb, w_ref[0, :,1024:1536]).astype(BF16)
    for hd in range(DIFF_HEADS):
        vd_ref[:, 2 * hd * DIFF_V_DIM:(2 * hd + 1) * DIFF_V_DIM] = v[:, hd * DIFF_V_DIM:(hd + 1) * DIFF_V_DIM]
        vd_ref[:, (2 * hd + 1) * DIFF_V_DIM:(2 * hd + 2) * DIFF_V_DIM] = e0
    gq_ref[...] = _dot(xb, w_ref[0, :,1536:1792]) * (GLA_K_DIM ** -0.5)
    gk_ref[...] = _dot(xb, w_ref[0, :,1792:2048])
    gv_ref[...] = _dot(xb, w_ref[0, :,2048:2560]).astype(BF16)
    gr_ref[...] = _dot(xb, w_ref[0, :,2560:3072])
    gd_ref[...] = _dot(xb, w_ref[0, :,3072:3200])[:, :2 * GLA_GATE_RANK]


def _inproj(li, x2d, w_pad, ln_g=None, ln_b=None, *, tm=512):
    t = x2d.shape[0]
    apply_ln = ln_g is not None
    row = lambda n: pl.BlockSpec((tm, n), lambda i: (i, 0))
    const = lambda shape: pl.BlockSpec(shape, lambda i: (0,) * len(shape))
    in_specs = [row(D_MODEL)]
    args = [x2d]
    if apply_ln:
        in_specs += [const((1, D_MODEL)), const((1, D_MODEL))]
        args += [ln_g.reshape(1, -1), ln_b.reshape(1, -1)]
    in_specs.append(pl.BlockSpec((1, D_MODEL, D_IN_PAD), lambda i: (li, 0, 0), pipeline_mode=pl.Buffered(1)))
    args.append(w_pad)
    widths = [(512, BF16), (512, BF16), (2 * DIFF_WIDTH, BF16), (256, F32), (256, F32),
              (512, BF16), (512, F32), (2 * GLA_GATE_RANK, F32)]
    out_shape = [jax.ShapeDtypeStruct((t, n), dt) for n, dt in widths]
    out_specs = [row(n) for n, _ in widths]
    if apply_ln:
        out_shape = [jax.ShapeDtypeStruct((t, D_MODEL), F32)] + out_shape
        out_specs = [row(D_MODEL)] + out_specs
    return pl.pallas_call(
        functools.partial(_inproj_kernel, apply_ln=apply_ln),
        out_shape=out_shape,
        grid=(t // tm,),
        in_specs=in_specs,
        out_specs=out_specs,
        compiler_params=pltpu.CompilerParams(
            dimension_semantics=("arbitrary",), vmem_limit_bytes=V7X_VMEM_LIMIT_BYTES),
        name="ln_inproj" if apply_ln else "inproj",
    )(*args)


def _attn_kernel(lq1_ref, lk1_ref, lq2_ref, lk2_ref, nw_ref, q_ref, k_ref, v_ref, band_ref,
                 o_ref, s_s, e_s, *, lam_init, tq, sub, seq, tk, rc):
    qi = pl.program_id(2)
    lam = (jnp.exp(jnp.sum(lq1_ref[...] * lk1_ref[...], axis=-1, keepdims=True))
           - jnp.exp(jnp.sum(lq2_ref[...] * lk2_ref[...], axis=-1, keepdims=True)) + lam_init)
    first = lax.broadcasted_iota(jnp.int32, (1, 2 * DIFF_QK_DIM), 1) < DIFF_QK_DIM
    nsb = tq // sub

    def logits(sb):
        q = q_ref[0, sb * sub:(sb + 1) * sub, :]
        zero = jnp.zeros_like(q)
        qq = jnp.concatenate([jnp.where(first, q, zero), jnp.where(first, zero, q)], axis=0)
        for j in range(seq // tk):
            cols = slice(j * tk, (j + 1) * tk)
            s_s[sb, :, cols] = lax.dot_general(qq, k_ref[0, cols, :], _NT, preferred_element_type=F32)

    def softmax_pv(sb):
        off = pl.multiple_of(seq - sub - (qi * tq + sb * sub), 128)
        ms = []
        for r in range(sub // rc):
            bias = band_ref[0, r * rc:(r + 1) * rc, pl.ds(off, seq)]
            for mp in range(2):
                rr = slice(mp * sub + r * rc, mp * sub + (r + 1) * rc)
                ms.append(jnp.max(s_s[sb, rr, :] + bias, axis=-1, keepdims=True))
        for r in range(sub // rc):
            bias = band_ref[0, r * rc:(r + 1) * rc, pl.ds(off, seq)]
            for mp in range(2):
                rr = slice(mp * sub + r * rc, mp * sub + (r + 1) * rc)
                e_s[sb, rr, :] = jnp.exp2((s_s[sb, rr, :] - ms[2 * r + mp]) + bias).astype(BF16)
        oo = [_dot(e_s[sb, mp * sub:(mp + 1) * sub, :], v_ref[0]) for mp in range(2)]
        r1 = 1.0 / oo[0][:, DIFF_V_DIM:DIFF_V_DIM + 1]
        r2 = lam / oo[1][:, DIFF_V_DIM:DIFF_V_DIM + 1]
        o = oo[0][:, :DIFF_V_DIM] * r1 - oo[1][:, :DIFF_V_DIM] * r2
        y = o * lax.rsqrt(jnp.mean(o * o, axis=-1, keepdims=True) + RMS_EPS) * nw_ref[...] * (1.0 - lam_init)
        o_ref[0, sb * sub:(sb + 1) * sub, :] = y.astype(o_ref.dtype)

    logits(0)
    for sb in range(nsb):
        if sb + 1 < nsb:
            logits(sb + 1)
        softmax_pv(sb)


def _diff_attention(qd, kd, vd, band, lq1, lk1, lq2, lk2, norm_w, *, lam_init, tq, sub):
    b, seq, _ = qd.shape
    nq = seq // tq
    vec = lambda n: pl.BlockSpec((1, n), lambda h, bi, qi: (0, 0))
    return pl.pallas_call(
        functools.partial(_attn_kernel, lam_init=lam_init, tq=tq, sub=sub, seq=seq, tk=512, rc=16),
        out_shape=jax.ShapeDtypeStruct((b, seq, DIFF_WIDTH), BF16),
        grid=(DIFF_HEADS, b, nq),
        in_specs=[vec(DIFF_QK_DIM), vec(DIFF_QK_DIM), vec(DIFF_QK_DIM), vec(DIFF_QK_DIM),
                  vec(DIFF_V_DIM),
                  pl.BlockSpec((1, tq, 128), lambda h, bi, qi: (bi, qi, h)),
                  pl.BlockSpec((1, seq, 128), lambda h, bi, qi: (bi, 0, h)),
                  pl.BlockSpec((1, seq, 256), lambda h, bi, qi: (bi, 0, h)),
                  pl.BlockSpec((1, sub, 2 * seq - sub), lambda h, bi, qi: (h, 0, 0))],
        out_specs=pl.BlockSpec((1, tq, 128), lambda h, bi, qi: (bi, qi, h)),
        scratch_shapes=[pltpu.VMEM((tq // sub, 2 * sub, seq), F32),
                        pltpu.VMEM((tq // sub, 2 * sub, seq), BF16)],
        compiler_params=pltpu.CompilerParams(
            dimension_semantics=("arbitrary", "arbitrary", "arbitrary"),
            vmem_limit_bytes=V7X_VMEM_LIMIT_BYTES),
        name="diff_attn",
    )(lq1.reshape(1, -1), lk1.reshape(1, -1), lq2.reshape(1, -1), lk2.reshape(1, -1),
      norm_w.reshape(1, -1), qd, kd, vd, band)


def _chunk_scan(x, row, *, reverse):
    n = x.shape[0]
    d = 1
    while d < GLA_CHUNK:
        if reverse:
            x = x + jnp.where(row < GLA_CHUNK - d, pltpu.roll(x, n - d, 0), 0.0)
        else:
            x = x + jnp.where(row >= d, pltpu.roll(x, d, 0), 0.0)
        d *= 2
    return x


def _gla_kernel(gd_ref, up_ref, gb_ref, q_ref, k_ref, v_ref, r_ref, nw_ref, o_ref,
                qq_s, kk_s, bf_s, bb_s, kv_s, st_s, acc_s, *, seq):
    c = GLA_CHUNK
    nc = seq // c
    z = _dot(gd_ref[0].astype(BF16), up_ref[0, 0]) + gb_ref[0, 0]
    g = (jnp.minimum(z, 0.0) - jnp.log(1.0 + jnp.exp(-jnp.abs(z)))) * (1.0 / GLA_GATE_TAU)
    row = lax.broadcasted_iota(jnp.int32, (seq, 1), 0) & (c - 1)
    bf = _chunk_scan(g[:, :128], row, reverse=False)
    bb = _chunk_scan(g[:, 128:], row, reverse=True)
    q = q_ref[0]
    k = k_ref[0]
    qq_s[:, 0:128] = (q * jnp.exp(bf)).astype(BF16)
    kk_s[:, 0:128] = (k * jnp.exp(-bf)).astype(BF16)
    qq_s[:, 128:256] = (q * jnp.exp(bb)).astype(BF16)
    kk_s[:, 128:256] = (k * jnp.exp(-bb)).astype(BF16)
    bf_s[...] = bf
    bb_s[...] = bb

    for n in range(nc):
        rows = slice(n * c, (n + 1) * c)
        kv_s[n] = lax.dot_general(v_ref[0, rows, :], kk_s[rows, :], _TN, preferred_element_type=F32)

    sr = lax.broadcasted_iota(jnp.int32, (256, 128), 0) < GLA_V_DIM
    sc = lax.broadcasted_iota(jnp.int32, (256, 128), 1) < GLA_K_DIM
    same_head = sr == sc
    sf = jnp.zeros((256, 128), F32)
    sb = jnp.zeros((256, 128), F32)
    for i in range(nc):
        st_s[i, :, 0:128] = sf.astype(BF16)
        dec_f = jnp.exp(bf_s[i * c + c - 1:i * c + c, :])
        sf = dec_f * (sf + jnp.where(same_head, kv_s[i, :, 0:128], 0.0))
        n = nc - 1 - i
        st_s[n, :, 128:256] = sb.astype(BF16)
        dec_b = jnp.exp(bb_s[n * c:n * c + 1, :])
        sb = dec_b * (sb + jnp.where(same_head, kv_s[n, :, 128:256], 0.0))

    lane = lax.broadcasted_iota(jnp.int32, (1, 256), 1)
    head0_v = lane < GLA_V_DIM
    quarter = [(lane >= i * GLA_K_DIM) & (lane < (i + 1) * GLA_K_DIM) for i in range(4)]
    ci = lax.broadcasted_iota(jnp.int32, (c, 128), 0)
    si = lax.broadcasted_iota(jnp.int32, (c, 128), 1) & (c - 1)
    causal = si <= ci
    anti = si > ci
    for n in range(nc):
        rows = slice(n * c, (n + 1) * c)
        qn = qq_s[rows, :]
        kn = kk_s[rows, :]
        vn = v_ref[0, rows, :]
        zk = jnp.zeros_like(kn)
        kbd = jnp.concatenate([jnp.where(m, kn, zk) for m in quarter], axis=0)
        scores = lax.dot_general(qn, kbd, _NT, preferred_element_type=F32)
        p = (jnp.where(causal, scores[:, 0:128], 0.0)
             + jnp.where(anti, scores[:, 128:256], 0.0)).astype(BF16)
        zv = jnp.zeros_like(vn)
        vbd = jnp.concatenate([jnp.where(head0_v, vn, zv), jnp.where(head0_v, zv, vn)], axis=0)
        acc_s[rows, :] = (_dot(p, vbd)
                          + lax.dot_general(qn, st_s[n], _NT, preferred_element_type=F32))

    nw = nw_ref[...]
    for hh in range(2):
        sl = slice(hh * GLA_V_DIM, (hh + 1) * GLA_V_DIM)
        oh = acc_s[:, sl]
        y = oh * lax.rsqrt(jnp.mean(oh * oh, axis=-1, keepdims=True) + RMS_EPS) * nw
        gate = r_ref[0, :, sl]
        o_ref[0, :, sl] = (y * (gate * jax.nn.sigmoid(gate))).astype(o_ref.dtype)


def _gla(gd, up_bd, gb_bd, gq, gk, gv, gr, norm_w):
    b, seq, _ = gq.shape
    hp = GLA_HEADS // 2
    return pl.pallas_call(
        functools.partial(_gla_kernel, seq=seq),
        out_shape=jax.ShapeDtypeStruct((b, seq, GLA_WIDTH), BF16),
        grid=(b, hp),
        in_specs=[pl.BlockSpec((1, seq, 2 * GLA_GATE_RANK), lambda bi, p: (bi, 0, 0)),
                  pl.BlockSpec((1, 1, 2 * GLA_GATE_RANK, 256), lambda bi, p: (p, 0, 0, 0)),
                  pl.BlockSpec((1, 1, 1, 256), lambda bi, p: (p, 0, 0, 0)),
                  pl.BlockSpec((1, seq, 128), lambda bi, p: (bi, 0, p)),
                  pl.BlockSpec((1, seq, 128), lambda bi, p: (bi, 0, p)),
                  pl.BlockSpec((1, seq, 256), lambda bi, p: (bi, 0, p)),
                  pl.BlockSpec((1, seq, 256), lambda bi, p: (bi, 0, p)),
                  pl.BlockSpec((1, GLA_V_DIM), lambda bi, p: (0, 0))],
        out_specs=pl.BlockSpec((1, seq, 256), lambda bi, p: (bi, 0, p)),
        scratch_shapes=[pltpu.VMEM((seq, 256), BF16), pltpu.VMEM((seq, 256), BF16),
                        pltpu.VMEM((seq, 128), F32), pltpu.VMEM((seq, 128), F32),
                        pltpu.VMEM((seq // GLA_CHUNK, 256, 256), F32),
                        pltpu.VMEM((seq // GLA_CHUNK, 256, 256), BF16),
                        pltpu.VMEM((seq, 256), F32)],
        compiler_params=pltpu.CompilerParams(
            dimension_semantics=("arbitrary", "arbitrary"),
            vmem_limit_bytes=V7X_VMEM_LIMIT_BYTES),
        name="gla",
    )(gd, up_bd, gb_bd, gq, gk, gv, gr, norm_w.reshape(1, -1))


def _mix_mlp_kernel(d_ref, g_ref, x_ref, wo_ref, l1g_ref, l1b_ref, w1_ref, b1_ref, w2_ref, b2_ref,
                    l2g_ref, l2b_ref, o_ref, *, ff_chunk):
    mix = (_dot(d_ref[...], wo_ref[0, 0:DIFF_WIDTH, :])
           + _dot(g_ref[...], wo_ref[0, DIFF_WIDTH:D_MODEL, :]))
    x1 = _layer_norm(ALPHA * x_ref[...] + mix, l1g_ref[0], l1b_ref[0])
    xb = x1.astype(BF16)
    acc = jnp.zeros(x1.shape, F32)
    for c0 in range(0, D_FF, ff_chunk):
        h = _dot(xb, w1_ref[0, :, c0:c0 + ff_chunk]) + b1_ref[0, :, c0:c0 + ff_chunk]
        h = jnp.square(jnp.maximum(h, 0.0)).astype(BF16)
        acc = acc + _dot(h, w2_ref[0, c0:c0 + ff_chunk, :])
    o_ref[...] = _layer_norm(ALPHA * x1 + (acc + b2_ref[0]), l2g_ref[0], l2b_ref[0])


def _mix_mlp(li, d2d, g2d, x2d, w_o, ln1_g, ln1_b, w1, b1, w2, b2, ln2_g, ln2_b, *, tm=512, ff_chunk=1024):
    t = x2d.shape[0]
    row = lambda n: pl.BlockSpec((tm, n), lambda i: (i, 0))
    layer = lambda a: pl.BlockSpec((1,) + a.shape[1:], lambda i: (li,) + (0,) * (a.ndim - 1),
                                   pipeline_mode=pl.Buffered(1))
    vecs = [v.reshape(DEPTH, 1, -1) for v in (ln1_g, ln1_b, b1, b2, ln2_g, ln2_b)]
    l1g, l1b, b1r, b2r, l2g, l2b = vecs
    args = (d2d, g2d, x2d, w_o, l1g, l1b, w1, b1r, w2, b2r, l2g, l2b)
    in_specs = [row(DIFF_WIDTH), row(GLA_WIDTH), row(D_MODEL)] + [layer(a) for a in args[3:]]
    return pl.pallas_call(
        functools.partial(_mix_mlp_kernel, ff_chunk=ff_chunk),
        out_shape=jax.ShapeDtypeStruct((t, D_MODEL), F32),
        grid=(t // tm,),
        in_specs=in_specs,
        out_specs=row(D_MODEL),
        compiler_params=pltpu.CompilerParams(
            dimension_semantics=("arbitrary",), vmem_limit_bytes=V7X_VMEM_LIMIT_BYTES),
        name="mix_mlp",
    )(*args)


def _gate_weights(gate_up, gate_bias):
    hp = GLA_HEADS // 2
    up = gate_up.reshape(2, GLA_GATE_RANK, hp, 128)
    z = jnp.zeros((GLA_GATE_RANK, hp, 128), gate_up.dtype)
    top = jnp.concatenate([up[0], z], axis=-1)
    bot = jnp.concatenate([z, up[1]], axis=-1)
    up_bd = jnp.concatenate([top, bot], axis=0)
    up_bd = up_bd.transpose(1, 0, 2)[:, None].astype(BF16)
    gb = gate_bias.reshape(2, hp, 128)
    gb_bd = jnp.concatenate([gb[0], gb[1]], axis=-1)[:, None, None]
    return up_bd, gb_bd


def kernel(x, ln_emb_g, ln_emb_b, rel_bias_table, w_in, lambda_q1, lambda_k1, lambda_q2, lambda_k2,
           diff_norm_w, gla_gate_up, gla_gate_bias, gla_norm_w, w_o, ln1_g, ln1_b,
           w_ffn1, b_ffn1, w_ffn2, b_ffn2, ln2_g, ln2_b):
    b, seq, _ = x.shape
    t = b * seq
    tq, sub = 1024, 256
    band = _bias_band(rel_bias_table, sub, seq)
    w_in_b = jnp.pad(w_in.astype(BF16), ((0, 0), (0, 0), (0, D_IN_PAD - D_IN)))
    w_o_b = w_o.astype(BF16)
    w1_b = w_ffn1.astype(BF16)
    w2_b = w_ffn2.astype(BF16)

    h = x.reshape(t, D_MODEL)
    for li in range(DEPTH):
        if li == 0:
            h, qd, kd, vd, gq, gk, gv, gr, gd = _inproj(li, h, w_in_b, ln_emb_g, ln_emb_b)
        else:
            qd, kd, vd, gq, gk, gv, gr, gd = _inproj(li, h, w_in_b)
        r3 = lambda a: a.reshape(b, seq, a.shape[-1])
        lam_init = 0.8 - 0.6 * math.exp(-0.3 * li)
        d_out = _diff_attention(r3(qd), r3(kd), r3(vd), band, lambda_q1[li], lambda_k1[li],
                                lambda_q2[li], lambda_k2[li], diff_norm_w[li],
                                lam_init=lam_init, tq=tq, sub=sub)
        up_bd, gb_bd = _gate_weights(gla_gate_up[li], gla_gate_bias[li])
        g_out = _gla(r3(gd), up_bd, gb_bd, r3(gq), r3(gk), r3(gv), r3(gr), gla_norm_w[li])
        h = _mix_mlp(li, d_out.reshape(t, DIFF_WIDTH), g_out.reshape(t, GLA_WIDTH), h, w_o_b,
                     ln1_g, ln1_b, w1_b, b_ffn1, w2_b, b_ffn2, ln2_g, ln2_b)
    return h.reshape(b, seq, D_MODEL)
```

```python
import functools
import math

import jax
import jax.numpy as jnp
from jax import lax
from jax.experimental import pallas as pl
from jax.experimental.pallas import tpu as pltpu

D_MODEL = 1024
DEPTH = 2
DIFF_HEADS = 4
DIFF_QK_DIM = 64
DIFF_V_DIM = 128
DIFF_WIDTH = 512
GLA_HEADS = 4
GLA_WIDTH = 512
GLA_V_DIM = 128
GLA_K_DIM = 64
GLA_KEY_WIDTH = 256
GLA_GATE_RANK = 16
GLA_GATE_TAU = 16.0
GLA_CHUNK = 64
D_FF = 4096
N_BUCKETS = 32
LN_EPS = 1e-5
RMS_EPS = 1e-5
ALPHA = (2.0 * DEPTH) ** 0.25
LOG2E = math.log2(math.e)
VT_ROWS = DIFF_V_DIM + 16
D_IN = 3104
D_IN_PAD = 3200

V7X_VMEM_LIMIT_BYTES = 56 * 1024 * 1024

BF16 = jnp.bfloat16
F32 = jnp.float32

_NT = (((1,), (1,)), ((), ()))
_TN = (((0,), (0,)), ((), ()))


def _dot(a, b):
    return jnp.dot(a, b, preferred_element_type=F32)


def _layer_norm(y, g, b):
    mu = jnp.mean(y, axis=-1, keepdims=True)
    d = y - mu
    var = jnp.mean(d * d, axis=-1, keepdims=True)
    return d * lax.rsqrt(var + LN_EPS) * g + b


def _band_kernel(table_ref, band_ref, *, tq, seq):
    h = pl.program_id(0)
    width = 2 * seq - tq
    lo, hi = seq - tq - 128, seq + 128
    near = hi - lo
    r = lax.broadcasted_iota(jnp.int32, (near, tq), 1)
    m = lax.broadcasted_iota(jnp.int32, (near, tq), 0) + lo
    rel = m - (seq - tq) - r
    n = jnp.abs(rel)
    n2 = n * n
    large = jnp.full_like(n, 8)
    for p in range(7, 14):
        large = large + jnp.where(n2 >= (1 << p), 1, 0)
    bucket = jnp.where(n < 8, n, large) + jnp.where(rel > 0, 16, 0)
    acc = jnp.zeros((near, tq), F32)
    for i in range(N_BUCKETS):
        acc = jnp.where(bucket == i, table_ref[i * DIFF_HEADS + h], acc)
    half = N_BUCKETS // 2
    band_ref[0, 0:lo, :] = jnp.full((lo, tq), table_ref[(half - 1) * DIFF_HEADS + h] * LOG2E, F32)
    band_ref[0, lo:hi, :] = acc * LOG2E
    band_ref[0, hi:width, :] = jnp.full((width - hi, tq), table_ref[(N_BUCKETS - 1) * DIFF_HEADS + h] * LOG2E, F32)


def _bias_band(table, tq, seq):
    width = 2 * seq - tq
    return pl.pallas_call(
        functools.partial(_band_kernel, tq=tq, seq=seq),
        out_shape=jax.ShapeDtypeStruct((DIFF_HEADS, width, tq), F32),
        grid=(DIFF_HEADS,),
        in_specs=[pl.BlockSpec(memory_space=pltpu.SMEM)],
        out_specs=pl.BlockSpec((1, width, tq), lambda h: (h, 0, 0)),
        name="bias_band",
    )(table.reshape(-1))


def _inproj_kernel(*refs, apply_ln):
    if apply_ln:
        (x_ref, g_ref, b_ref, w_ref, h0_ref,
         qd_ref, kd_ref, vd_ref, gq_ref, gk_ref, gv_ref, gr_ref, gd_ref) = refs
        xn = _layer_norm(x_ref[...], g_ref[...], b_ref[...])
        h0_ref[...] = xn
    else:
        (x_ref, w_ref,
         qd_ref, kd_ref, vd_ref, gq_ref, gk_ref, gv_ref, gr_ref, gd_ref) = refs
        xn = x_ref[...]
    xb = xn.astype(BF16)
    qd_ref[...] = (_dot(xb, w_ref[0, :,0:512]) * (DIFF_QK_DIM ** -0.5 * LOG2E)).astype(BF16)
    kd_ref[...] = _dot(xb, w_ref[0, :,512:1024]).astype(BF16)
    tm = xb.shape[0]
    pad_rows = VT_ROWS - DIFF_V_DIM
    ones_row = (lax.broadcasted_iota(jnp.int32, (pad_rows, tm), 0) == 0).astype(BF16)
    v = _dot(xb, w_ref[0, :,1024:1536])
    for hd in range(DIFF_HEADS):
        vd_ref[0, hd, 0:DIFF_V_DIM, :] = v[:, hd * DIFF_V_DIM:(hd + 1) * DIFF_V_DIM].T.astype(BF16)
        vd_ref[0, hd, DIFF_V_DIM:VT_ROWS, :] = ones_row
    gq_ref[...] = _dot(xb, w_ref[0, :,1536:1792]) * (GLA_K_DIM ** -0.5)
    gk_ref[...] = _dot(xb, w_ref[0, :,1792:2048])
    gv_ref[...] = _dot(xb, w_ref[0, :,2048:2560]).astype(BF16)
    gr_ref[...] = _dot(xb, w_ref[0, :,2560:3072])
    gd_ref[...] = _dot(xb, w_ref[0, :,3072:3200])[:, :2 * GLA_GATE_RANK]


def _inproj(li, x2d, w_pad, seq, ln_g=None, ln_b=None, *, tm=512):
    t = x2d.shape[0]
    apply_ln = ln_g is not None
    row = lambda n: pl.BlockSpec((tm, n), lambda i: (i, 0))
    const = lambda shape: pl.BlockSpec(shape, lambda i: (0,) * len(shape))
    in_specs = [row(D_MODEL)]
    args = [x2d]
    if apply_ln:
        in_specs += [const((1, D_MODEL)), const((1, D_MODEL))]
        args += [ln_g.reshape(1, -1), ln_b.reshape(1, -1)]
    in_specs.append(pl.BlockSpec((1, D_MODEL, D_IN_PAD), lambda i: (li, 0, 0), pipeline_mode=pl.Buffered(1)))
    args.append(w_pad)
    widths = [(512, BF16), (512, BF16), None, (256, F32), (256, F32),
              (512, BF16), (512, F32), (2 * GLA_GATE_RANK, F32)]
    out_shape = [jax.ShapeDtypeStruct((t, w[0]), w[1]) if w else None for w in widths]
    out_specs = [row(w[0]) if w else None for w in widths]
    tiles = seq // tm
    out_shape[2] = jax.ShapeDtypeStruct((t // seq, DIFF_HEADS, VT_ROWS, seq), BF16)
    out_specs[2] = pl.BlockSpec((1, DIFF_HEADS, VT_ROWS, tm), lambda i: (i // tiles, 0, 0, i % tiles))
    if apply_ln:
        out_shape = [jax.ShapeDtypeStruct((t, D_MODEL), F32)] + out_shape
        out_specs = [row(D_MODEL)] + out_specs
    return pl.pallas_call(
        functools.partial(_inproj_kernel, apply_ln=apply_ln),
        out_shape=out_shape,
        grid=(t // tm,),
        in_specs=in_specs,
        out_specs=out_specs,
        compiler_params=pltpu.CompilerParams(
            dimension_semantics=("arbitrary",), vmem_limit_bytes=V7X_VMEM_LIMIT_BYTES),
        name="ln_inproj" if apply_ln else "inproj",
    )(*args)


def _attn_kernel(lq1_ref, lk1_ref, lq2_ref, lk2_ref, nw_ref, q_ref, k_ref, v_ref, band_ref,
                 o_ref, s_s, e_s, *, lam_init, tq, sub, seq, kc):
    qi = pl.program_id(2)
    lam = (jnp.exp(jnp.sum(lq1_ref[...] * lk1_ref[...], axis=-1, keepdims=True))
           - jnp.exp(jnp.sum(lq2_ref[...] * lk2_ref[...], axis=-1, keepdims=True)) + lam_init)
    first = lax.broadcasted_iota(jnp.int32, (1, 2 * DIFF_QK_DIM), 1) < DIFF_QK_DIM
    nsb = tq // sub

    def logits(sb):
        q = q_ref[0, sb * sub:(sb + 1) * sub, :]
        zero = jnp.zeros_like(q)
        qq = jnp.concatenate([jnp.where(first, q, zero), jnp.where(first, zero, q)], axis=0)
        off = seq - sub - (qi * tq + sb * sub)
        m = jnp.full((1, 2 * sub), -jnp.inf, F32)
        for c in range(seq // kc):
            rows = slice(c * kc, (c + 1) * kc)
            s = lax.dot_general(k_ref[0, rows, :], qq, _NT, preferred_element_type=F32)
            bias = band_ref[0, pl.ds(pl.multiple_of(off + c * kc, 128), kc), :]
            s = s + jnp.concatenate([bias, bias], axis=1)
            s_s[sb, rows, :] = s
            m = jnp.maximum(m, jnp.max(s, axis=0, keepdims=True))
        return m

    def softmax_pv(sb, m):
        for c in range(seq // kc):
            rows = slice(c * kc, (c + 1) * kc)
            e_s[sb, rows, :] = jnp.exp2(s_s[sb, rows, :] - m).astype(BF16)
        ot = _dot(v_ref[0, 0], e_s[sb])
        r1 = 1.0 / ot[DIFF_V_DIM:DIFF_V_DIM + 1, 0:sub]
        r2 = lam / ot[DIFF_V_DIM:DIFF_V_DIM + 1, sub:2 * sub]
        o = ot[0:DIFF_V_DIM, 0:sub] * r1 - ot[0:DIFF_V_DIM, sub:2 * sub] * r2
        y = o * lax.rsqrt(jnp.mean(o * o, axis=0, keepdims=True) + RMS_EPS) * nw_ref[...] * (1.0 - lam_init)
        o_ref[0, sb * sub:(sb + 1) * sub, :] = y.T.astype(o_ref.dtype)

    m_next = logits(0)
    for sb in range(nsb):
        m = m_next
        if sb + 1 < nsb:
            m_next = logits(sb + 1)
        softmax_pv(sb, m)


def _diff_attention(qd, kd, vd, band, lq1, lk1, lq2, lk2, norm_w, *, lam_init, tq, sub):
    b, seq, _ = qd.shape
    nq = seq // tq
    vec = lambda n: pl.BlockSpec((1, n), lambda h, bi, qi: (0, 0))
    return pl.pallas_call(
        functools.partial(_attn_kernel, lam_init=lam_init, tq=tq, sub=sub, seq=seq, kc=512),
        out_shape=jax.ShapeDtypeStruct((b, seq, DIFF_WIDTH), BF16),
        grid=(DIFF_HEADS, b, nq),
        in_specs=[vec(DIFF_QK_DIM), vec(DIFF_QK_DIM), vec(DIFF_QK_DIM), vec(DIFF_QK_DIM),
                  pl.BlockSpec((DIFF_V_DIM, 1), lambda h, bi, qi: (0, 0)),
                  pl.BlockSpec((1, tq, 128), lambda h, bi, qi: (bi, qi, h)),
                  pl.BlockSpec((1, seq, 128), lambda h, bi, qi: (bi, 0, h)),
                  pl.BlockSpec((1, 1, VT_ROWS, seq), lambda h, bi, qi: (bi, h, 0, 0)),
                  pl.BlockSpec((1, 2 * seq - sub, sub), lambda h, bi, qi: (h, 0, 0))],
        out_specs=pl.BlockSpec((1, tq, 128), lambda h, bi, qi: (bi, qi, h)),
        scratch_shapes=[pltpu.VMEM((tq // sub, seq, 2 * sub), F32),
                        pltpu.VMEM((tq // sub, seq, 2 * sub), BF16)],
        compiler_params=pltpu.CompilerParams(
            dimension_semantics=("arbitrary", "arbitrary", "arbitrary"),
            vmem_limit_bytes=V7X_VMEM_LIMIT_BYTES),
        name="diff_attn",
    )(lq1.reshape(1, -1), lk1.reshape(1, -1), lq2.reshape(1, -1), lk2.reshape(1, -1),
      norm_w.reshape(-1, 1), qd, kd, vd, band)


def _chunk_scan(x, row, *, reverse):
    n = x.shape[0]
    d = 1
    while d < GLA_CHUNK:
        if reverse:
            x = x + jnp.where(row < GLA_CHUNK - d, pltpu.roll(x, n - d, 0), 0.0)
        else:
            x = x + jnp.where(row >= d, pltpu.roll(x, d, 0), 0.0)
        d *= 2
    return x


def _gla_kernel(gd_ref, up_ref, gb_ref, q_ref, k_ref, v_ref, r_ref, nw_ref, o_ref,
                qq_s, kk_s, bf_s, bb_s, kv_s, st_s, acc_s, *, seq):
    c = GLA_CHUNK
    nc = seq // c
    z = _dot(gd_ref[0].astype(BF16), up_ref[0, 0]) + gb_ref[0, 0]
    g = (jnp.minimum(z, 0.0) - jnp.log(1.0 + jnp.exp(-jnp.abs(z)))) * (1.0 / GLA_GATE_TAU)
    row = lax.broadcasted_iota(jnp.int32, (seq, 1), 0) & (c - 1)
    bf = _chunk_scan(g[:, :128], row, reverse=False)
    bb = _chunk_scan(g[:, 128:], row, reverse=True)
    q = q_ref[0]
    k = k_ref[0]
    qq_s[:, 0:128] = (q * jnp.exp(bf)).astype(BF16)
    kk_s[:, 0:128] = (k * jnp.exp(-bf)).astype(BF16)
    qq_s[:, 128:256] = (q * jnp.exp(bb)).astype(BF16)
    kk_s[:, 128:256] = (k * jnp.exp(-bb)).astype(BF16)
    bf_s[...] = bf
    bb_s[...] = bb

    for n in range(nc):
        rows = slice(n * c, (n + 1) * c)
        kv_s[n] = lax.dot_general(v_ref[0, rows, :], kk_s[rows, :], _TN, preferred_element_type=F32)

    sr = lax.broadcasted_iota(jnp.int32, (256, 128), 0) < GLA_V_DIM
    sc = lax.broadcasted_iota(jnp.int32, (256, 128), 1) < GLA_K_DIM
    same_head = sr == sc
    sf = jnp.zeros((256, 128), F32)
    sb = jnp.zeros((256, 128), F32)
    for i in range(nc):
        st_s[i, :, 0:128] = sf.astype(BF16)
        dec_f = jnp.exp(bf_s[i * c + c - 1:i * c + c, :])
        sf = dec_f * (sf + jnp.where(same_head, kv_s[i, :, 0:128], 0.0))
        n = nc - 1 - i
        st_s[n, :, 128:256] = sb.astype(BF16)
        dec_b = jnp.exp(bb_s[n * c:n * c + 1, :])
        sb = dec_b * (sb + jnp.where(same_head, kv_s[n, :, 128:256], 0.0))

    lane = lax.broadcasted_iota(jnp.int32, (1, 256), 1)
    head0_v = lane < GLA_V_DIM
    quarter = [(lane >= i * GLA_K_DIM) & (lane < (i + 1) * GLA_K_DIM) for i in range(4)]
    ci = lax.broadcasted_iota(jnp.int32, (c, 128), 0)
    si = lax.broadcasted_iota(jnp.int32, (c, 128), 1) & (c - 1)
    causal = si <= ci
    anti = si > ci
    for n in range(nc):
        rows = slice(n * c, (n + 1) * c)
        qn = qq_s[rows, :]
        kn = kk_s[rows, :]
        vn = v_ref[0, rows, :]
        zk = jnp.zeros_like(kn)
        kbd = jnp.concatenate([jnp.where(m, kn, zk) for m in quarter], axis=0)
        scores = lax.dot_general(qn, kbd, _NT, preferred_element_type=F32)
        p = (jnp.where(causal, scores[:, 0:128], 0.0)
             + jnp.where(anti, scores[:, 128:256], 0.0)).astype(BF16)
        zv = jnp.zeros_like(vn)
        vbd = jnp.concatenate([jnp.where(head0_v, vn, zv), jnp.where(head0_v, zv, vn)], axis=0)
        acc_s[rows, :] = (_dot(p, vbd)
                          + lax.dot_general(qn, st_s[n], _NT, preferred_element_type=F32))

    nw = nw_ref[...]
    for hh in range(2):
        sl = slice(hh * GLA_V_DIM, (hh + 1) * GLA_V_DIM)
        oh = acc_s[:, sl]
        y = oh * lax.rsqrt(jnp.mean(oh * oh, axis=-1, keepdims=True) + RMS_EPS) * nw
        gate = r_ref[0, :, sl]
        o_ref[0, :, sl] = (y * (gate * jax.nn.sigmoid(gate))).astype(o_ref.dtype)


def _gla(gd, up_bd, gb_bd, gq, gk, gv, gr, norm_w):
    b, seq, _ = gq.shape
    hp = GLA_HEADS // 2
    return pl.pallas_call(
        functools.partial(_gla_kernel, seq=seq),
        out_shape=jax.ShapeDtypeStruct((b, seq, GLA_WIDTH), BF16),
        grid=(b, hp),
        in_specs=[pl.BlockSpec((1, seq, 2 * GLA_GATE_RANK), lambda bi, p: (bi, 0, 0)),
                  pl.BlockSpec((1, 1, 2 * GLA_GATE_RANK, 256), lambda bi, p: (p, 0, 0, 0)),
                  pl.BlockSpec((1, 1, 1, 256), lambda bi, p: (p, 0, 0, 0)),
                  pl.BlockSpec((1, seq, 128), lambda bi, p: (bi, 0, p)),
                  pl.BlockSpec((1, seq, 128), lambda bi, p: (bi, 0, p)),
                  pl.BlockSpec((1, seq, 256), lambda bi, p: (bi, 0, p)),
                  pl.BlockSpec((1, seq, 256), lambda bi, p: (bi, 0, p)),
                  pl.BlockSpec((1, GLA_V_DIM), lambda bi, p: (0, 0))],
        out_specs=pl.BlockSpec((1, seq, 256), lambda bi, p: (bi, 0, p)),
        scratch_shapes=[pltpu.VMEM((seq, 256), BF16), pltpu.VMEM((seq, 256), BF16),
                        pltpu.VMEM((seq, 128), F32), pltpu.VMEM((seq, 128), F32),
                        pltpu.VMEM((seq // GLA_CHUNK, 256, 256), F32),
                        pltpu.VMEM((seq // GLA_CHUNK, 256, 256), BF16),
                        pltpu.VMEM((seq, 256), F32)],
        compiler_params=pltpu.CompilerParams(
            dimension_semantics=("arbitrary", "arbitrary"),
            vmem_limit_bytes=V7X_VMEM_LIMIT_BYTES),
        name="gla",
    )(gd, up_bd, gb_bd, gq, gk, gv, gr, norm_w.reshape(1, -1))


def _mix_mlp_kernel(d_ref, g_ref, x_ref, wo_ref, l1g_ref, l1b_ref, w1_ref, b1_ref, w2_ref, b2_ref,
                    l2g_ref, l2b_ref, o_ref, *, ff_chunk):
    tm = x_ref.shape[0]
    halves = (slice(0, tm // 2), slice(tm // 2, tm))
    x1s = []
    for rows in halves:
        mix = (_dot(d_ref[rows, :], wo_ref[0, 0:DIFF_WIDTH, :])
               + _dot(g_ref[rows, :], wo_ref[0, DIFF_WIDTH:D_MODEL, :]))
        x1s.append(_layer_norm(ALPHA * x_ref[rows, :] + mix, l1g_ref[0], l1b_ref[0]))
    for rows, x1 in zip(halves, x1s):
        xb = x1.astype(BF16)
        acc = jnp.zeros(x1.shape, F32)
        for c0 in range(0, D_FF, ff_chunk):
            h = _dot(xb, w1_ref[0, :, c0:c0 + ff_chunk]) + b1_ref[0, :, c0:c0 + ff_chunk]
            h = jnp.square(jnp.maximum(h, 0.0)).astype(BF16)
            acc = acc + _dot(h, w2_ref[0, c0:c0 + ff_chunk, :])
        o_ref[rows, :] = _layer_norm(ALPHA * x1 + (acc + b2_ref[0]), l2g_ref[0], l2b_ref[0])


def _mix_mlp(li, d2d, g2d, x2d, w_o, ln1_g, ln1_b, w1, b1, w2, b2, ln2_g, ln2_b, *, tm=512, ff_chunk=1024):
    t = x2d.shape[0]
    row = lambda n: pl.BlockSpec((tm, n), lambda i: (i, 0))
    layer = lambda a: pl.BlockSpec((1,) + a.shape[1:], lambda i: (li,) + (0,) * (a.ndim - 1),
                                   pipeline_mode=pl.Buffered(1))
    vecs = [v.reshape(DEPTH, 1, -1) for v in (ln1_g, ln1_b, b1, b2, ln2_g, ln2_b)]
    l1g, l1b, b1r, b2r, l2g, l2b = vecs
    args = (d2d, g2d, x2d, w_o, l1g, l1b, w1, b1r, w2, b2r, l2g, l2b)
    in_specs = [row(DIFF_WIDTH), row(GLA_WIDTH), row(D_MODEL)] + [layer(a) for a in args[3:]]
    return pl.pallas_call(
        functools.partial(_mix_mlp_kernel, ff_chunk=ff_chunk),
        out_shape=jax.ShapeDtypeStruct((t, D_MODEL), F32),
        grid=(t // tm,),
        in_specs=in_specs,
        out_specs=row(D_MODEL),
        compiler_params=pltpu.CompilerParams(
            dimension_semantics=("arbitrary",), vmem_limit_bytes=V7X_VMEM_LIMIT_BYTES),
        name="mix_mlp",
    )(*args)


def _gate_weights(gate_up, gate_bias):
    hp = GLA_HEADS // 2
    up = gate_up.reshape(2, GLA_GATE_RANK, hp, 128)
    z = jnp.zeros((GLA_GATE_RANK, hp, 128), gate_up.dtype)
    top = jnp.concatenate([up[0], z], axis=-1)
    bot = jnp.concatenate([z, up[1]], axis=-1)
    up_bd = jnp.concatenate([top, bot], axis=0)
    up_bd = up_bd.transpose(1, 0, 2)[:, None].astype(BF16)
    gb = gate_bias.reshape(2, hp, 128)
    gb_bd = jnp.concatenate([gb[0], gb[1]], axis=-1)[:, None, None]
    return up_bd, gb_bd


def kernel(x, ln_emb_g, ln_emb_b, rel_bias_table, w_in, lambda_q1, lambda_k1, lambda_q2, lambda_k2,
           diff_norm_w, gla_gate_up, gla_gate_bias, gla_norm_w, w_o, ln1_g, ln1_b,
           w_ffn1, b_ffn1, w_ffn2, b_ffn2, ln2_g, ln2_b):
    b, seq, _ = x.shape
    t = b * seq
    tq, sub = 1024, 256
    band = _bias_band(rel_bias_table, sub, seq)
    w_in_b = jnp.pad(w_in.astype(BF16), ((0, 0), (0, 0), (0, D_IN_PAD - D_IN)))
    w_o_b = w_o.astype(BF16)
    w1_b = w_ffn1.astype(BF16)
    w2_b = w_ffn2.astype(BF16)

    h = x.reshape(t, D_MODEL)
    for li in range(DEPTH):
        if li == 0:
            h, qd, kd, vd, gq, gk, gv, gr, gd = _inproj(li, h, w_in_b, seq, ln_emb_g, ln_emb_b)
        else:
            qd, kd, vd, gq, gk, gv, gr, gd = _inproj(li, h, w_in_b, seq)
        r3 = lambda a: a.reshape(b, seq, a.shape[-1])
        lam_init = 0.8 - 0.6 * math.exp(-0.3 * li)
        d_out = _diff_attention(r3(qd), r3(kd), vd, band, lambda_q1[li], lambda_k1[li],
                                lambda_q2[li], lambda_k2[li], diff_norm_w[li],
                                lam_init=lam_init, tq=tq, sub=sub)
        up_bd, gb_bd = _gate_weights(gla_gate_up[li], gla_gate_bias[li])
        g_out = _gla(r3(gd), up_bd, gb_bd, r3(gq), r3(gk), r3(gv), r3(gr), gla_norm_w[li])
        h = _mix_mlp(li, d_out.reshape(t, DIFF_WIDTH), g_out.reshape(t, GLA_WIDTH), h, w_o_b,
                     ln1_g, ln1_b, w1_b, b_ffn1, w2_b, b_ffn2, ln2_g, ln2_b)
    return h.reshape(b, seq, D_MODEL)
```

```python
import functools
import math

import jax
import jax.numpy as jnp
from jax import lax
from jax.experimental import pallas as pl
from jax.experimental.pallas import tpu as pltpu

D_MODEL = 1024
DEPTH = 2
DIFF_HEADS = 4
DIFF_QK_DIM = 64
DIFF_V_DIM = 128
DIFF_WIDTH = 512
GLA_HEADS = 4
GLA_WIDTH = 512
GLA_V_DIM = 128
GLA_K_DIM = 64
GLA_KEY_WIDTH = 256
GLA_GATE_RANK = 16
GLA_GATE_TAU = 16.0
GLA_CHUNK = 64
D_FF = 4096
N_BUCKETS = 32
LN_EPS = 1e-5
RMS_EPS = 1e-5
ALPHA = (2.0 * DEPTH) ** 0.25
LOG2E = math.log2(math.e)
VT_ROWS = DIFF_V_DIM + 16
D_IN = 3104
D_IN_PAD = 3200

V7X_VMEM_LIMIT_BYTES = 56 * 1024 * 1024

BF16 = jnp.bfloat16
F32 = jnp.float32

_NT = (((1,), (1,)), ((), ()))
_TN = (((0,), (0,)), ((), ()))


def _dot(a, b):
    return jnp.dot(a, b, preferred_element_type=F32)


def _layer_norm(y, g, b):
    mu = jnp.mean(y, axis=-1, keepdims=True)
    d = y - mu
    var = jnp.mean(d * d, axis=-1, keepdims=True)
    return d * lax.rsqrt(var + LN_EPS) * g + b


def _band_kernel(table_ref, band_ref, *, tq, seq):
    h = pl.program_id(0)
    width = 2 * seq - tq
    lo, hi = seq - tq - 128, seq + 128
    near = hi - lo
    r = lax.broadcasted_iota(jnp.int32, (near, tq), 1)
    m = lax.broadcasted_iota(jnp.int32, (near, tq), 0) + lo
    rel = m - (seq - tq) - r
    n = jnp.abs(rel)
    n2 = n * n
    large = jnp.full_like(n, 8)
    for p in range(7, 14):
        large = large + jnp.where(n2 >= (1 << p), 1, 0)
    bucket = jnp.where(n < 8, n, large) + jnp.where(rel > 0, 16, 0)
    acc = jnp.zeros((near, tq), F32)
    for i in range(N_BUCKETS):
        acc = jnp.where(bucket == i, table_ref[i * DIFF_HEADS + h], acc)
    half = N_BUCKETS // 2
    band_ref[0, 0:lo, :] = jnp.full((lo, tq), table_ref[(half - 1) * DIFF_HEADS + h] * LOG2E, F32)
    band_ref[0, lo:hi, :] = acc * LOG2E
    band_ref[0, hi:width, :] = jnp.full((width - hi, tq), table_ref[(N_BUCKETS - 1) * DIFF_HEADS + h] * LOG2E, F32)


def _bias_band(table, tq, seq):
    width = 2 * seq - tq
    return pl.pallas_call(
        functools.partial(_band_kernel, tq=tq, seq=seq),
        out_shape=jax.ShapeDtypeStruct((DIFF_HEADS, width, tq), F32),
        grid=(DIFF_HEADS,),
        in_specs=[pl.BlockSpec(memory_space=pltpu.SMEM)],
        out_specs=pl.BlockSpec((1, width, tq), lambda h: (h, 0, 0)),
        name="bias_band",
    )(table.reshape(-1))


def _inproj_kernel(*refs, apply_ln):
    if apply_ln:
        (x_ref, g_ref, b_ref, w_ref, h0_ref,
         qd_ref, kd_ref, vd_ref, gq_ref, gk_ref, gv_ref, gr_ref, gd_ref) = refs
        xn = _layer_norm(x_ref[...], g_ref[...], b_ref[...])
        h0_ref[...] = xn
    else:
        (x_ref, w_ref,
         qd_ref, kd_ref, vd_ref, gq_ref, gk_ref, gv_ref, gr_ref, gd_ref) = refs
        xn = x_ref[...]
    xb = xn.astype(BF16)
    q = _dot(xb, w_ref[0, :,0:512]) * (DIFF_QK_DIM ** -0.5 * LOG2E)
    for hd in range(DIFF_HEADS):
        qd_ref[0, hd] = q[:, hd * 2 * DIFF_QK_DIM:(hd + 1) * 2 * DIFF_QK_DIM].T.astype(BF16)
    kd_ref[...] = _dot(xb, w_ref[0, :,512:1024]).astype(BF16)
    tm = xb.shape[0]
    pad_rows = VT_ROWS - DIFF_V_DIM
    ones_row = (lax.broadcasted_iota(jnp.int32, (pad_rows, tm), 0) == 0).astype(BF16)
    v = _dot(xb, w_ref[0, :,1024:1536])
    for hd in range(DIFF_HEADS):
        vd_ref[0, hd, 0:DIFF_V_DIM, :] = v[:, hd * DIFF_V_DIM:(hd + 1) * DIFF_V_DIM].T.astype(BF16)
        vd_ref[0, hd, DIFF_V_DIM:VT_ROWS, :] = ones_row
    gq_ref[...] = _dot(xb, w_ref[0, :,1536:1792]) * (GLA_K_DIM ** -0.5)
    gk_ref[...] = _dot(xb, w_ref[0, :,1792:2048])
    gv_ref[...] = _dot(xb, w_ref[0, :,2048:2560]).astype(BF16)
    gr_ref[...] = _dot(xb, w_ref[0, :,2560:3072])
    gd_ref[...] = _dot(xb, w_ref[0, :,3072:3200])[:, :2 * GLA_GATE_RANK]


def _inproj(li, x2d, w_pad, seq, ln_g=None, ln_b=None, *, tm=512):
    t = x2d.shape[0]
    apply_ln = ln_g is not None
    row = lambda n: pl.BlockSpec((tm, n), lambda i: (i, 0))
    const = lambda shape: pl.BlockSpec(shape, lambda i: (0,) * len(shape))
    in_specs = [row(D_MODEL)]
    args = [x2d]
    if apply_ln:
        in_specs += [const((1, D_MODEL)), const((1, D_MODEL))]
        args += [ln_g.reshape(1, -1), ln_b.reshape(1, -1)]
    in_specs.append(pl.BlockSpec((1, D_MODEL, D_IN_PAD), lambda i: (li, 0, 0), pipeline_mode=pl.Buffered(1)))
    args.append(w_pad)
    widths = [None, (512, BF16), None, (256, F32), (256, F32),
              (512, BF16), (512, F32), (2 * GLA_GATE_RANK, F32)]
    out_shape = [jax.ShapeDtypeStruct((t, w[0]), w[1]) if w else None for w in widths]
    out_specs = [row(w[0]) if w else None for w in widths]
    tiles = seq // tm
    for pos, nrows in ((0, 2 * DIFF_QK_DIM), (2, VT_ROWS)):
        out_shape[pos] = jax.ShapeDtypeStruct((t // seq, DIFF_HEADS, nrows, seq), BF16)
        out_specs[pos] = pl.BlockSpec((1, DIFF_HEADS, nrows, tm), lambda i: (i // tiles, 0, 0, i % tiles))
    if apply_ln:
        out_shape = [jax.ShapeDtypeStruct((t, D_MODEL), F32)] + out_shape
        out_specs = [row(D_MODEL)] + out_specs
    return pl.pallas_call(
        functools.partial(_inproj_kernel, apply_ln=apply_ln),
        out_shape=out_shape,
        grid=(t // tm,),
        in_specs=in_specs,
        out_specs=out_specs,
        compiler_params=pltpu.CompilerParams(
            dimension_semantics=("arbitrary",), vmem_limit_bytes=V7X_VMEM_LIMIT_BYTES),
        name="ln_inproj" if apply_ln else "inproj",
    )(*args)


def _attn_kernel(lq1_ref, lk1_ref, lq2_ref, lk2_ref, nw_ref, q_ref, k_ref, v_ref, band_ref,
                 o_ref, s_s, e_s, *, lam_init, tq, sub, seq, kc):
    qi = pl.program_id(2)
    lam = (jnp.exp(jnp.sum(lq1_ref[...] * lk1_ref[...], axis=-1, keepdims=True))
           - jnp.exp(jnp.sum(lq2_ref[...] * lk2_ref[...], axis=-1, keepdims=True)) + lam_init)
    first = lax.broadcasted_iota(jnp.int32, (2 * DIFF_QK_DIM, 1), 0) < DIFF_QK_DIM
    nsb = tq // sub

    def logits(sb):
        qt = q_ref[0, 0, :, sb * sub:(sb + 1) * sub]
        zero = jnp.zeros_like(qt)
        qq = jnp.concatenate([jnp.where(first, qt, zero), jnp.where(first, zero, qt)], axis=1)
        off = seq - sub - (qi * tq + sb * sub)
        m = jnp.full((1, 2 * sub), -jnp.inf, F32)
        for c in range(seq // kc):
            rows = slice(c * kc, (c + 1) * kc)
            s = _dot(k_ref[0, rows, :], qq)
            bias = band_ref[0, pl.ds(pl.multiple_of(off + c * kc, 128), kc), :]
            s = s + jnp.concatenate([bias, bias], axis=1)
            s_s[sb, rows, :] = s
            m = jnp.maximum(m, jnp.max(s, axis=0, keepdims=True))
        return m

    def softmax_pv(sb, m):
        for c in range(seq // kc):
            rows = slice(c * kc, (c + 1) * kc)
            e_s[sb, rows, :] = jnp.exp2(s_s[sb, rows, :] - m).astype(BF16)
        ot = _dot(v_ref[0, 0], e_s[sb])
        r1 = 1.0 / ot[DIFF_V_DIM:DIFF_V_DIM + 1, 0:sub]
        r2 = lam / ot[DIFF_V_DIM:DIFF_V_DIM + 1, sub:2 * sub]
        o = ot[0:DIFF_V_DIM, 0:sub] * r1 - ot[0:DIFF_V_DIM, sub:2 * sub] * r2
        y = o * lax.rsqrt(jnp.mean(o * o, axis=0, keepdims=True) + RMS_EPS) * nw_ref[...] * (1.0 - lam_init)
        o_ref[0, sb * sub:(sb + 1) * sub, :] = y.T.astype(o_ref.dtype)

    m_next = logits(0)
    for sb in range(nsb):
        m = m_next
        if sb + 1 < nsb:
            m_next = logits(sb + 1)
        softmax_pv(sb, m)


def _diff_attention(qd, kd, vd, band, lq1, lk1, lq2, lk2, norm_w, *, lam_init, tq, sub):
    b, seq, _ = kd.shape
    nq = seq // tq
    vec = lambda n: pl.BlockSpec((1, n), lambda h, bi, qi: (0, 0))
    return pl.pallas_call(
        functools.partial(_attn_kernel, lam_init=lam_init, tq=tq, sub=sub, seq=seq, kc=512),
        out_shape=jax.ShapeDtypeStruct((b, seq, DIFF_WIDTH), BF16),
        grid=(DIFF_HEADS, b, nq),
        in_specs=[vec(DIFF_QK_DIM), vec(DIFF_QK_DIM), vec(DIFF_QK_DIM), vec(DIFF_QK_DIM),
                  pl.BlockSpec((DIFF_V_DIM, 1), lambda h, bi, qi: (0, 0)),
                  pl.BlockSpec((1, 1, 2 * DIFF_QK_DIM, tq), lambda h, bi, qi: (bi, h, 0, qi)),
                  pl.BlockSpec((1, seq, 128), lambda h, bi, qi: (bi, 0, h)),
                  pl.BlockSpec((1, 1, VT_ROWS, seq), lambda h, bi, qi: (bi, h, 0, 0)),
                  pl.BlockSpec((1, 2 * seq - sub, sub), lambda h, bi, qi: (h, 0, 0))],
        out_specs=pl.BlockSpec((1, tq, 128), lambda h, bi, qi: (bi, qi, h)),
        scratch_shapes=[pltpu.VMEM((tq // sub, seq, 2 * sub), F32),
                        pltpu.VMEM((tq // sub, seq, 2 * sub), BF16)],
        compiler_params=pltpu.CompilerParams(
            dimension_semantics=("arbitrary", "arbitrary", "arbitrary"),
            vmem_limit_bytes=V7X_VMEM_LIMIT_BYTES),
        name="diff_attn",
    )(lq1.reshape(1, -1), lk1.reshape(1, -1), lq2.reshape(1, -1), lk2.reshape(1, -1),
      norm_w.reshape(-1, 1), qd, kd, vd, band)


def _chunk_scan(x, row, *, reverse):
    n = x.shape[0]
    d = 1
    while d < GLA_CHUNK:
        if reverse:
            x = x + jnp.where(row < GLA_CHUNK - d, pltpu.roll(x, n - d, 0), 0.0)
        else:
            x = x + jnp.where(row >= d, pltpu.roll(x, d, 0), 0.0)
        d *= 2
    return x


def _gla_kernel(gd_ref, up_ref, gb_ref, q_ref, k_ref, v_ref, r_ref, nw_ref, o_ref,
                qq_s, kk_s, bf_s, bb_s, kv_s, st_s, acc_s, *, seq):
    c = GLA_CHUNK
    nc = seq // c
    z = _dot(gd_ref[0].astype(BF16), up_ref[0, 0]) + gb_ref[0, 0]
    g = (jnp.minimum(z, 0.0) - jnp.log(1.0 + jnp.exp(-jnp.abs(z)))) * (1.0 / GLA_GATE_TAU)
    row = lax.broadcasted_iota(jnp.int32, (seq, 1), 0) & (c - 1)
    bf = _chunk_scan(g[:, :128], row, reverse=False)
    bb = _chunk_scan(g[:, 128:], row, reverse=True)
    q = q_ref[0]
    k = k_ref[0]
    qq_s[:, 0:128] = (q * jnp.exp(bf)).astype(BF16)
    kk_s[:, 0:128] = (k * jnp.exp(-bf)).astype(BF16)
    qq_s[:, 128:256] = (q * jnp.exp(bb)).astype(BF16)
    kk_s[:, 128:256] = (k * jnp.exp(-bb)).astype(BF16)
    bf_s[...] = bf
    bb_s[...] = bb

    for n in range(nc):
        rows = slice(n * c, (n + 1) * c)
        kv_s[n] = lax.dot_general(v_ref[0, rows, :], kk_s[rows, :], _TN, preferred_element_type=F32)

    sr = lax.broadcasted_iota(jnp.int32, (256, 128), 0) < GLA_V_DIM
    sc = lax.broadcasted_iota(jnp.int32, (256, 128), 1) < GLA_K_DIM
    same_head = sr == sc
    sf = jnp.zeros((256, 128), F32)
    sb = jnp.zeros((256, 128), F32)
    for i in range(nc):
        st_s[i, :, 0:128] = sf.astype(BF16)
        dec_f = jnp.exp(bf_s[i * c + c - 1:i * c + c, :])
        sf = dec_f * (sf + jnp.where(same_head, kv_s[i, :, 0:128], 0.0))
        n = nc - 1 - i
        st_s[n, :, 128:256] = sb.astype(BF16)
        dec_b = jnp.exp(bb_s[n * c:n * c + 1, :])
        sb = dec_b * (sb + jnp.where(same_head, kv_s[n, :, 128:256], 0.0))

    lane = lax.broadcasted_iota(jnp.int32, (1, 256), 1)
    head0_v = lane < GLA_V_DIM
    quarter = [(lane >= i * GLA_K_DIM) & (lane < (i + 1) * GLA_K_DIM) for i in range(4)]
    ci = lax.broadcasted_iota(jnp.int32, (c, 128), 0)
    si = lax.broadcasted_iota(jnp.int32, (c, 128), 1) & (c - 1)
    causal = si <= ci
    anti = si > ci
    for n in range(nc):
        rows = slice(n * c, (n + 1) * c)
        qn = qq_s[rows, :]
        kn = kk_s[rows, :]
        vn = v_ref[0, rows, :]
        zk = jnp.zeros_like(kn)
        kbd = jnp.concatenate([jnp.where(m, kn, zk) for m in quarter], axis=0)
        scores = lax.dot_general(qn, kbd, _NT, preferred_element_type=F32)
        p = (jnp.where(causal, scores[:, 0:128], 0.0)
             + jnp.where(anti, scores[:, 128:256], 0.0)).astype(BF16)
        zv = jnp.zeros_like(vn)
        vbd = jnp.concatenate([jnp.where(head0_v, vn, zv), jnp.where(head0_v, zv, vn)], axis=0)
        acc_s[rows, :] = (_dot(p, vbd)
                          + lax.dot_general(qn, st_s[n], _NT, preferred_element_type=F32))

    nw = nw_ref[...]
    for hh in range(2):
        sl = slice(hh * GLA_V_DIM, (hh + 1) * GLA_V_DIM)
        oh = acc_s[:, sl]
        y = oh * lax.rsqrt(jnp.mean(oh * oh, axis=-1, keepdims=True) + RMS_EPS) * nw
        gate = r_ref[0, :, sl]
        o_ref[0, :, sl] = (y * (gate * jax.nn.sigmoid(gate))).astype(o_ref.dtype)


def _gla(gd, up_bd, gb_bd, gq, gk, gv, gr, norm_w):
    b, seq, _ = gq.shape
    hp = GLA_HEADS // 2
    return pl.pallas_call(
        functools.partial(_gla_kernel, seq=seq),
        out_shape=jax.ShapeDtypeStruct((b, seq, GLA_WIDTH), BF16),
        grid=(b, hp),
        in_specs=[pl.BlockSpec((1, seq, 2 * GLA_GATE_RANK), lambda bi, p: (bi, 0, 0)),
                  pl.BlockSpec((1, 1, 2 * GLA_GATE_RANK, 256), lambda bi, p: (p, 0, 0, 0)),
                  pl.BlockSpec((1, 1, 1, 256), lambda bi, p: (p, 0, 0, 0)),
                  pl.BlockSpec((1, seq, 128), lambda bi, p: (bi, 0, p)),
                  pl.BlockSpec((1, seq, 128), lambda bi, p: (bi, 0, p)),
                  pl.BlockSpec((1, seq, 256), lambda bi, p: (bi, 0, p)),
                  pl.BlockSpec((1, seq, 256), lambda bi, p: (bi, 0, p)),
                  pl.BlockSpec((1, GLA_V_DIM), lambda bi, p: (0, 0))],
        out_specs=pl.BlockSpec((1, seq, 256), lambda bi, p: (bi, 0, p)),
        scratch_shapes=[pltpu.VMEM((seq, 256), BF16), pltpu.VMEM((seq, 256), BF16),
                        pltpu.VMEM((seq, 128), F32), pltpu.VMEM((seq, 128), F32),
                        pltpu.VMEM((seq // GLA_CHUNK, 256, 256), F32),
                        pltpu.VMEM((seq // GLA_CHUNK, 256, 256), BF16),
                        pltpu.VMEM((seq, 256), F32)],
        compiler_params=pltpu.CompilerParams(
            dimension_semantics=("arbitrary", "arbitrary"),
            vmem_limit_bytes=V7X_VMEM_LIMIT_BYTES),
        name="gla",
    )(gd, up_bd, gb_bd, gq, gk, gv, gr, norm_w.reshape(1, -1))


def _mix_mlp_kernel(d_ref, g_ref, x_ref, wo_ref, l1g_ref, l1b_ref, w1_ref, b1_ref, w2_ref, b2_ref,
                    l2g_ref, l2b_ref, o_ref, *, ff_chunk):
    tm = x_ref.shape[0]
    halves = (slice(0, tm // 2), slice(tm // 2, tm))
    x1s = []
    for rows in halves:
        mix = (_dot(d_ref[rows, :], wo_ref[0, 0:DIFF_WIDTH, :])
               + _dot(g_ref[rows, :], wo_ref[0, DIFF_WIDTH:D_MODEL, :]))
        x1s.append(_layer_norm(ALPHA * x_ref[rows, :] + mix, l1g_ref[0], l1b_ref[0]))
    for rows, x1 in zip(halves, x1s):
        xb = x1.astype(BF16)
        acc = jnp.zeros(x1.shape, F32)
        for c0 in range(0, D_FF, ff_chunk):
            h = _dot(xb, w1_ref[0, :, c0:c0 + ff_chunk]) + b1_ref[0, :, c0:c0 + ff_chunk]
            h = jnp.square(jnp.maximum(h, 0.0)).astype(BF16)
            acc = acc + _dot(h, w2_ref[0, c0:c0 + ff_chunk, :])
        o_ref[rows, :] = _layer_norm(ALPHA * x1 + (acc + b2_ref[0]), l2g_ref[0], l2b_ref[0])


def _mix_mlp(li, d2d, g2d, x2d, w_o, ln1_g, ln1_b, w1, b1, w2, b2, ln2_g, ln2_b, *, tm=512, ff_chunk=1024):
    t = x2d.shape[0]
    row = lambda n: pl.BlockSpec((tm, n), lambda i: (i, 0))
    layer = lambda a: pl.BlockSpec((1,) + a.shape[1:], lambda i: (li,) + (0,) * (a.ndim - 1),
                                   pipeline_mode=pl.Buffered(1))
    vecs = [v.reshape(DEPTH, 1, -1) for v in (ln1_g, ln1_b, b1, b2, ln2_g, ln2_b)]
    l1g, l1b, b1r, b2r, l2g, l2b = vecs
    args = (d2d, g2d, x2d, w_o, l1g, l1b, w1, b1r, w2, b2r, l2g, l2b)
    in_specs = [row(DIFF_WIDTH), row(GLA_WIDTH), row(D_MODEL)] + [layer(a) for a in args[3:]]
    return pl.pallas_call(
        functools.partial(_mix_mlp_kernel, ff_chunk=ff_chunk),
        out_shape=jax.ShapeDtypeStruct((t, D_MODEL), F32),
        grid=(t // tm,),
        in_specs=in_specs,
        out_specs=row(D_MODEL),
        compiler_params=pltpu.CompilerParams(
            dimension_semantics=("arbitrary",), vmem_limit_bytes=V7X_VMEM_LIMIT_BYTES),
        name="mix_mlp",
    )(*args)


def _gate_weights(gate_up, gate_bias):
    hp = GLA_HEADS // 2
    up = gate_up.reshape(2, GLA_GATE_RANK, hp, 128)
    z = jnp.zeros((GLA_GATE_RANK, hp, 128), gate_up.dtype)
    top = jnp.concatenate([up[0], z], axis=-1)
    bot = jnp.concatenate([z, up[1]], axis=-1)
    up_bd = jnp.concatenate([top, bot], axis=0)
    up_bd = up_bd.transpose(1, 0, 2)[:, None].astype(BF16)
    gb = gate_bias.reshape(2, hp, 128)
    gb_bd = jnp.concatenate([gb[0], gb[1]], axis=-1)[:, None, None]
    return up_bd, gb_bd


def kernel(x, ln_emb_g, ln_emb_b, rel_bias_table, w_in, lambda_q1, lambda_k1, lambda_q2, lambda_k2,
           diff_norm_w, gla_gate_up, gla_gate_bias, gla_norm_w, w_o, ln1_g, ln1_b,
           w_ffn1, b_ffn1, w_ffn2, b_ffn2, ln2_g, ln2_b):
    b, seq, _ = x.shape
    t = b * seq
    tq, sub = 1024, 256
    band = _bias_band(rel_bias_table, sub, seq)
    w_in_b = jnp.pad(w_in.astype(BF16), ((0, 0), (0, 0), (0, D_IN_PAD - D_IN)))
    w_o_b = w_o.astype(BF16)
    w1_b = w_ffn1.astype(BF16)
    w2_b = w_ffn2.astype(BF16)

    h = x.reshape(t, D_MODEL)
    for li in range(DEPTH):
        if li == 0:
            h, qd, kd, vd, gq, gk, gv, gr, gd = _inproj(li, h, w_in_b, seq, ln_emb_g, ln_emb_b)
        else:
            qd, kd, vd, gq, gk, gv, gr, gd = _inproj(li, h, w_in_b, seq)
        r3 = lambda a: a.reshape(b, seq, a.shape[-1])
        lam_init = 0.8 - 0.6 * math.exp(-0.3 * li)
        d_out = _diff_attention(qd, r3(kd), vd, band, lambda_q1[li], lambda_k1[li],
                                lambda_q2[li], lambda_k2[li], diff_norm_w[li],
                                lam_init=lam_init, tq=tq, sub=sub)
        up_bd, gb_bd = _gate_weights(gla_gate_up[li], gla_gate_bias[li])
        g_out = _gla(r3(gd), up_bd, gb_bd, r3(gq), r3(gk), r3(gv), r3(gr), gla_norm_w[li])
        h = _mix_mlp(li, d_out.reshape(t, DIFF_WIDTH), g_out.reshape(t, GLA_WIDTH), h, w_o_b,
                     ln1_g, ln1_b, w1_b, b_ffn1, w2_b, b_ffn2, ln2_g, ln2_b)
    return h.reshape(b, seq, D_MODEL)
```

```python
import functools
import math

import jax
import jax.numpy as jnp
from jax import lax
from jax.experimental import pallas as pl
from jax.experimental.pallas import tpu as pltpu

D_MODEL = 1024
DEPTH = 2
DIFF_HEADS = 4
DIFF_QK_DIM = 64
DIFF_V_DIM = 128
DIFF_WIDTH = 512
GLA_HEADS = 4
GLA_WIDTH = 512
GLA_V_DIM = 128
GLA_K_DIM = 64
GLA_KEY_WIDTH = 256
GLA_GATE_RANK = 16
GLA_GATE_TAU = 16.0
GLA_CHUNK = 64
D_FF = 4096
N_BUCKETS = 32
LN_EPS = 1e-5
RMS_EPS = 1e-5
ALPHA = (2.0 * DEPTH) ** 0.25
LOG2E = math.log2(math.e)
VT_ROWS = DIFF_V_DIM + 16
D_IN = 3104

V7X_VMEM_LIMIT_BYTES = 56 * 1024 * 1024

BF16 = jnp.bfloat16
F32 = jnp.float32

_NT = (((1,), (1,)), ((), ()))
_TN = (((0,), (0,)), ((), ()))


def _dot(a, b):
    return jnp.dot(a, b, preferred_element_type=F32)


def _layer_norm(y, g, b):
    mu = jnp.mean(y, axis=-1, keepdims=True)
    d = y - mu
    var = jnp.mean(d * d, axis=-1, keepdims=True)
    return d * lax.rsqrt(var + LN_EPS) * g + b


def _band_kernel(table_ref, band_ref, *, tq, seq):
    h = pl.program_id(0)
    width = 2 * seq - tq
    lo, hi = seq - tq - 128, seq + 128
    near = hi - lo
    r = lax.broadcasted_iota(jnp.int32, (near, tq), 1)
    m = lax.broadcasted_iota(jnp.int32, (near, tq), 0) + lo
    rel = m - (seq - tq) - r
    n = jnp.abs(rel)
    n2 = n * n
    large = jnp.full_like(n, 8)
    for p in range(7, 14):
        large = large + jnp.where(n2 >= (1 << p), 1, 0)
    bucket = jnp.where(n < 8, n, large) + jnp.where(rel > 0, 16, 0)
    acc = jnp.zeros((near, tq), F32)
    for i in range(N_BUCKETS):
        acc = jnp.where(bucket == i, table_ref[i * DIFF_HEADS + h], acc)
    half = N_BUCKETS // 2
    band_ref[0, 0:lo, :] = jnp.full((lo, tq), table_ref[(half - 1) * DIFF_HEADS + h] * LOG2E, F32)
    band_ref[0, lo:hi, :] = acc * LOG2E
    band_ref[0, hi:width, :] = jnp.full((width - hi, tq), table_ref[(N_BUCKETS - 1) * DIFF_HEADS + h] * LOG2E, F32)


def _bias_band(table, tq, seq):
    width = 2 * seq - tq
    return pl.pallas_call(
        functools.partial(_band_kernel, tq=tq, seq=seq),
        out_shape=jax.ShapeDtypeStruct((DIFF_HEADS, width, tq), F32),
        grid=(DIFF_HEADS,),
        in_specs=[pl.BlockSpec(memory_space=pltpu.SMEM)],
        out_specs=pl.BlockSpec((1, width, tq), lambda h: (h, 0, 0)),
        name="bias_band",
    )(table.reshape(-1))


def _inproj_kernel(*refs, apply_ln):
    if apply_ln:
        (x_ref, g_ref, b_ref, w_ref,
         qd_ref, kd_ref, vd_ref, gq_ref, gk_ref, gv_ref, gr_ref, gd_ref) = refs
        xn = _layer_norm(x_ref[...], g_ref[...], b_ref[...])
    else:
        (x_ref, w_ref,
         qd_ref, kd_ref, vd_ref, gq_ref, gk_ref, gv_ref, gr_ref, gd_ref) = refs
        xn = x_ref[...]
    xb = xn.astype(BF16)
    qd_ref[...] = (_dot(xb, w_ref[0, :,0:512]) * (DIFF_QK_DIM ** -0.5 * LOG2E)).astype(BF16)
    kd_ref[...] = _dot(xb, w_ref[0, :,512:1024]).astype(BF16)
    tm = xb.shape[0]
    pad_rows = VT_ROWS - DIFF_V_DIM
    ones_row = (lax.broadcasted_iota(jnp.int32, (pad_rows, tm), 0) == 0).astype(BF16)
    v = _dot(xb, w_ref[0, :,1024:1536])
    for hd in range(DIFF_HEADS):
        vd_ref[0, hd, 0:DIFF_V_DIM, :] = v[:, hd * DIFF_V_DIM:(hd + 1) * DIFF_V_DIM].T.astype(BF16)
        vd_ref[0, hd, DIFF_V_DIM:VT_ROWS, :] = ones_row
    gq_ref[...] = _dot(xb, w_ref[0, :,1536:1792]) * (GLA_K_DIM ** -0.5)
    gk_ref[...] = _dot(xb, w_ref[0, :,1792:2048])
    gv_ref[...] = _dot(xb, w_ref[0, :,2048:2560]).astype(BF16)
    gr_ref[...] = _dot(xb, w_ref[0, :,2560:3072])
    gd_ref[...] = _dot(xb, w_ref[0, :,3072:D_IN])


def _inproj(li, x2d, w_pad, seq, ln_g=None, ln_b=None, *, tm=512):
    t = x2d.shape[0]
    apply_ln = ln_g is not None
    row = lambda n: pl.BlockSpec((tm, n), lambda i: (i, 0))
    const = lambda shape: pl.BlockSpec(shape, lambda i: (0,) * len(shape))
    in_specs = [row(D_MODEL)]
    args = [x2d]
    if apply_ln:
        in_specs += [const((1, D_MODEL)), const((1, D_MODEL))]
        args += [ln_g.reshape(1, -1), ln_b.reshape(1, -1)]
    in_specs.append(pl.BlockSpec((1, D_MODEL, D_IN), lambda i: (li, 0, 0), pipeline_mode=pl.Buffered(1)))
    args.append(w_pad)
    widths = [(512, BF16), (512, BF16), None, (256, F32), (256, F32),
              (512, BF16), (512, F32), (2 * GLA_GATE_RANK, F32)]
    out_shape = [jax.ShapeDtypeStruct((t, w[0]), w[1]) if w else None for w in widths]
    out_specs = [row(w[0]) if w else None for w in widths]
    tiles = seq // tm
    out_shape[2] = jax.ShapeDtypeStruct((t // seq, DIFF_HEADS, VT_ROWS, seq), BF16)
    out_specs[2] = pl.BlockSpec((1, DIFF_HEADS, VT_ROWS, tm), lambda i: (i // tiles, 0, 0, i % tiles))
    return pl.pallas_call(
        functools.partial(_inproj_kernel, apply_ln=apply_ln),
        out_shape=out_shape,
        grid=(t // tm,),
        in_specs=in_specs,
        out_specs=out_specs,
        compiler_params=pltpu.CompilerParams(
            dimension_semantics=("arbitrary",), vmem_limit_bytes=V7X_VMEM_LIMIT_BYTES),
        name="ln_inproj" if apply_ln else "inproj",
    )(*args)


def _attn_kernel(lq1_ref, lk1_ref, lq2_ref, lk2_ref, nw_ref, q_ref, k_ref, v_ref, band_ref,
                 o_ref, s_s, e_s, *, lam_init, tq, sub, seq, kc):
    qi = pl.program_id(2)
    lam = (jnp.exp(jnp.sum(lq1_ref[...] * lk1_ref[...], axis=-1, keepdims=True))
           - jnp.exp(jnp.sum(lq2_ref[...] * lk2_ref[...], axis=-1, keepdims=True)) + lam_init)
    first = lax.broadcasted_iota(jnp.int32, (1, 2 * DIFF_QK_DIM), 1) < DIFF_QK_DIM
    nsb = tq // sub
    nslots = s_s.shape[0]

    def logits(sb):
        q = q_ref[0, sb * sub:(sb + 1) * sub, :]
        zero = jnp.zeros_like(q)
        qq = jnp.concatenate([jnp.where(first, q, zero), jnp.where(first, zero, q)], axis=0)
        off = seq - sub - (qi * tq + sb * sub)
        m = jnp.full((1, 2 * sub), -jnp.inf, F32)
        for c in range(seq // kc):
            rows = slice(c * kc, (c + 1) * kc)
            s = lax.dot_general(k_ref[0, rows, :], qq, _NT, preferred_element_type=F32)
            bias = band_ref[0, pl.ds(pl.multiple_of(off + c * kc, 128), kc), :]
            s = s + jnp.concatenate([bias, bias], axis=1)
            s_s[sb % nslots, rows, :] = s
            m = jnp.maximum(m, jnp.max(s, axis=0, keepdims=True))
        return m

    def softmax_pv(sb, m):
        for c in range(seq // kc):
            rows = slice(c * kc, (c + 1) * kc)
            e_s[sb % nslots, rows, :] = jnp.exp2(s_s[sb % nslots, rows, :] - m).astype(BF16)
        ot = _dot(v_ref[0, 0], e_s[sb % nslots])
        r1 = 1.0 / ot[DIFF_V_DIM:DIFF_V_DIM + 1, 0:sub]
        r2 = lam / ot[DIFF_V_DIM:DIFF_V_DIM + 1, sub:2 * sub]
        o = ot[0:DIFF_V_DIM, 0:sub] * r1 - ot[0:DIFF_V_DIM, sub:2 * sub] * r2
        y = o * lax.rsqrt(jnp.mean(o * o, axis=0, keepdims=True) + RMS_EPS) * nw_ref[...] * (1.0 - lam_init)
        o_ref[0, sb * sub:(sb + 1) * sub, :] = y.T.astype(o_ref.dtype)

    m_next = logits(0)
    for sb in range(nsb):
        m = m_next
        if sb + 1 < nsb:
            m_next = logits(sb + 1)
        softmax_pv(sb, m)


def _diff_attention(qd, kd, vd, band, lq1, lk1, lq2, lk2, norm_w, *, lam_init, tq, sub):
    b, seq, _ = qd.shape
    nq = seq // tq
    vec = lambda n: pl.BlockSpec((1, n), lambda h, bi, qi: (0, 0))
    return pl.pallas_call(
        functools.partial(_attn_kernel, lam_init=lam_init, tq=tq, sub=sub, seq=seq, kc=512),
        out_shape=jax.ShapeDtypeStruct((b, seq, DIFF_WIDTH), BF16),
        grid=(DIFF_HEADS, b, nq),
        in_specs=[vec(DIFF_QK_DIM), vec(DIFF_QK_DIM), vec(DIFF_QK_DIM), vec(DIFF_QK_DIM),
                  pl.BlockSpec((DIFF_V_DIM, 1), lambda h, bi, qi: (0, 0)),
                  pl.BlockSpec((1, tq, 128), lambda h, bi, qi: (bi, qi, h)),
                  pl.BlockSpec((1, seq, 128), lambda h, bi, qi: (bi, 0, h)),
                  pl.BlockSpec((1, 1, VT_ROWS, seq), lambda h, bi, qi: (bi, h, 0, 0)),
                  pl.BlockSpec((1, 2 * seq - sub, sub), lambda h, bi, qi: (h, 0, 0))],
        out_specs=pl.BlockSpec((1, tq, 128), lambda h, bi, qi: (bi, qi, h)),
        scratch_shapes=[pltpu.VMEM((min(tq // sub, 3), seq, 2 * sub), F32),
                        pltpu.VMEM((min(tq // sub, 3), seq, 2 * sub), BF16)],
        compiler_params=pltpu.CompilerParams(
            dimension_semantics=("arbitrary", "arbitrary", "arbitrary"),
            vmem_limit_bytes=V7X_VMEM_LIMIT_BYTES),
        name="diff_attn",
    )(lq1.reshape(1, -1), lk1.reshape(1, -1), lq2.reshape(1, -1), lk2.reshape(1, -1),
      norm_w.reshape(-1, 1), qd, kd, vd, band)


def _chunk_scan(x, row, *, reverse):
    n = x.shape[0]
    d = 1
    while d < GLA_CHUNK:
        if reverse:
            x = x + jnp.where(row < GLA_CHUNK - d, pltpu.roll(x, n - d, 0), 0.0)
        else:
            x = x + jnp.where(row >= d, pltpu.roll(x, d, 0), 0.0)
        d *= 2
    return x


def _gla_kernel(gd_ref, up_ref, gb_ref, q_ref, k_ref, v_ref, r_ref, nw_ref, o_ref,
                qq_s, kk_s, bf_s, bb_s, kv_s, st_s, acc_s, *, seq):
    c = GLA_CHUNK
    nc = seq // c
    z = _dot(gd_ref[0].astype(BF16), up_ref[0, 0]) + gb_ref[0, 0]
    g = (jnp.minimum(z, 0.0) - jnp.log(1.0 + jnp.exp(-jnp.abs(z)))) * (1.0 / GLA_GATE_TAU)
    row = lax.broadcasted_iota(jnp.int32, (seq, 1), 0) & (c - 1)
    bf = _chunk_scan(g[:, :128], row, reverse=False)
    bb = _chunk_scan(g[:, 128:], row, reverse=True)
    q = q_ref[0]
    k = k_ref[0]
    qq_s[:, 0:128] = (q * jnp.exp(bf)).astype(BF16)
    kk_s[:, 0:128] = (k * jnp.exp(-bf)).astype(BF16)
    qq_s[:, 128:256] = (q * jnp.exp(bb)).astype(BF16)
    kk_s[:, 128:256] = (k * jnp.exp(-bb)).astype(BF16)
    bf_s[...] = bf
    bb_s[...] = bb

    for n in range(nc):
        rows = slice(n * c, (n + 1) * c)
        kv_s[n] = lax.dot_general(v_ref[0, rows, :], kk_s[rows, :], _TN, preferred_element_type=F32)

    sr = lax.broadcasted_iota(jnp.int32, (256, 128), 0) < GLA_V_DIM
    sc = lax.broadcasted_iota(jnp.int32, (256, 128), 1) < GLA_K_DIM
    same_head = sr == sc
    sf = jnp.zeros((256, 128), F32)
    sb = jnp.zeros((256, 128), F32)
    for i in range(nc):
        st_s[i, :, 0:128] = sf.astype(BF16)
        dec_f = jnp.exp(bf_s[i * c + c - 1:i * c + c, :])
        sf = dec_f * (sf + jnp.where(same_head, kv_s[i, :, 0:128], 0.0))
        n = nc - 1 - i
        st_s[n, :, 128:256] = sb.astype(BF16)
        dec_b = jnp.exp(bb_s[n * c:n * c + 1, :])
        sb = dec_b * (sb + jnp.where(same_head, kv_s[n, :, 128:256], 0.0))

    lane = lax.broadcasted_iota(jnp.int32, (1, 256), 1)
    head0_v = lane < GLA_V_DIM
    quarter = [(lane >= i * GLA_K_DIM) & (lane < (i + 1) * GLA_K_DIM) for i in range(4)]
    ci = lax.broadcasted_iota(jnp.int32, (c, 128), 0)
    si = lax.broadcasted_iota(jnp.int32, (c, 128), 1) & (c - 1)
    causal = si <= ci
    anti = si > ci
    for n in range(nc):
        rows = slice(n * c, (n + 1) * c)
        qn = qq_s[rows, :]
        kn = kk_s[rows, :]
        vn = v_ref[0, rows, :]
        zk = jnp.zeros_like(kn)
        kbd = jnp.concatenate([jnp.where(m, kn, zk) for m in quarter], axis=0)
        scores = lax.dot_general(qn, kbd, _NT, preferred_element_type=F32)
        p = (jnp.where(causal, scores[:, 0:128], 0.0)
             + jnp.where(anti, scores[:, 128:256], 0.0)).astype(BF16)
        zv = jnp.zeros_like(vn)
        vbd = jnp.concatenate([jnp.where(head0_v, vn, zv), jnp.where(head0_v, zv, vn)], axis=0)
        acc_s[rows, :] = (_dot(p, vbd)
                          + lax.dot_general(qn, st_s[n], _NT, preferred_element_type=F32))

    nw = nw_ref[...]
    for hh in range(2):
        sl = slice(hh * GLA_V_DIM, (hh + 1) * GLA_V_DIM)
        oh = acc_s[:, sl]
        y = oh * lax.rsqrt(jnp.mean(oh * oh, axis=-1, keepdims=True) + RMS_EPS) * nw
        gate = r_ref[0, :, sl]
        o_ref[0, :, sl] = (y * (gate * jax.nn.sigmoid(gate))).astype(o_ref.dtype)


def _gla(gd, up_bd, gb_bd, gq, gk, gv, gr, norm_w):
    b, seq, _ = gq.shape
    hp = GLA_HEADS // 2
    return pl.pallas_call(
        functools.partial(_gla_kernel, seq=seq),
        out_shape=jax.ShapeDtypeStruct((b, seq, GLA_WIDTH), BF16),
        grid=(b, hp),
        in_specs=[pl.BlockSpec((1, seq, 2 * GLA_GATE_RANK), lambda bi, p: (bi, 0, 0)),
                  pl.BlockSpec((1, 1, 2 * GLA_GATE_RANK, 256), lambda bi, p: (p, 0, 0, 0)),
                  pl.BlockSpec((1, 1, 1, 256), lambda bi, p: (p, 0, 0, 0)),
                  pl.BlockSpec((1, seq, 128), lambda bi, p: (bi, 0, p)),
                  pl.BlockSpec((1, seq, 128), lambda bi, p: (bi, 0, p)),
                  pl.BlockSpec((1, seq, 256), lambda bi, p: (bi, 0, p)),
                  pl.BlockSpec((1, seq, 256), lambda bi, p: (bi, 0, p)),
                  pl.BlockSpec((1, GLA_V_DIM), lambda bi, p: (0, 0))],
        out_specs=pl.BlockSpec((1, seq, 256), lambda bi, p: (bi, 0, p)),
        scratch_shapes=[pltpu.VMEM((seq, 256), BF16), pltpu.VMEM((seq, 256), BF16),
                        pltpu.VMEM((seq, 128), F32), pltpu.VMEM((seq, 128), F32),
                        pltpu.VMEM((seq // GLA_CHUNK, 256, 256), F32),
                        pltpu.VMEM((seq // GLA_CHUNK, 256, 256), BF16),
                        pltpu.VMEM((seq, 256), F32)],
        compiler_params=pltpu.CompilerParams(
            dimension_semantics=("arbitrary", "arbitrary"),
            vmem_limit_bytes=V7X_VMEM_LIMIT_BYTES),
        name="gla",
    )(gd, up_bd, gb_bd, gq, gk, gv, gr, norm_w.reshape(1, -1))


def _mix_mlp_kernel(d_ref, g_ref, x_ref, eg_ref, eb_ref, wo_ref, l1g_ref, l1b_ref, w1_ref, b1_ref, w2_ref, b2_ref,
                    l2g_ref, l2b_ref, o_ref, *, ff_chunk, emb_ln):
    tm = x_ref.shape[0]
    halves = (slice(0, tm // 2), slice(tm // 2, tm))
    x1s = []
    for rows in halves:
        mix = (_dot(d_ref[rows, :], wo_ref[0, 0:DIFF_WIDTH, :])
               + _dot(g_ref[rows, :], wo_ref[0, DIFF_WIDTH:D_MODEL, :]))
        x = x_ref[rows, :]
        if emb_ln:
            x = _layer_norm(x, eg_ref[...], eb_ref[...])
        x1s.append(_layer_norm(ALPHA * x + mix, l1g_ref[0], l1b_ref[0]))
    for rows, x1 in zip(halves, x1s):
        xb = x1.astype(BF16)
        acc = jnp.zeros(x1.shape, F32)
        for c0 in range(0, D_FF, ff_chunk):
            h = _dot(xb, w1_ref[0, :, c0:c0 + ff_chunk]) + b1_ref[0, :, c0:c0 + ff_chunk]
            h = jnp.square(jnp.maximum(h, 0.0)).astype(BF16)
            acc = acc + _dot(h, w2_ref[0, c0:c0 + ff_chunk, :])
        o_ref[rows, :] = _layer_norm(ALPHA * x1 + (acc + b2_ref[0]), l2g_ref[0], l2b_ref[0])


def _mix_mlp(li, d2d, g2d, x2d, emb_g, emb_b, w_o, ln1_g, ln1_b, w1, b1, w2, b2, ln2_g, ln2_b,
             *, emb_ln, tm=512, ff_chunk=1024):
    t = x2d.shape[0]
    row = lambda n: pl.BlockSpec((tm, n), lambda i: (i, 0))
    const = lambda shape: pl.BlockSpec(shape, lambda i: (0,) * len(shape))
    layer = lambda a: pl.BlockSpec((1,) + a.shape[1:], lambda i: (li,) + (0,) * (a.ndim - 1),
                                   pipeline_mode=pl.Buffered(1))
    vecs = [v.reshape(DEPTH, 1, -1) for v in (ln1_g, ln1_b, b1, b2, ln2_g, ln2_b)]
    l1g, l1b, b1r, b2r, l2g, l2b = vecs
    args = (d2d, g2d, x2d, emb_g.reshape(1, -1), emb_b.reshape(1, -1),
            w_o, l1g, l1b, w1, b1r, w2, b2r, l2g, l2b)
    in_specs = ([row(DIFF_WIDTH), row(GLA_WIDTH), row(D_MODEL), const((1, D_MODEL)), const((1, D_MODEL))]
                + [layer(a) for a in args[5:]])
    return pl.pallas_call(
        functools.partial(_mix_mlp_kernel, ff_chunk=ff_chunk, emb_ln=emb_ln),
        out_shape=jax.ShapeDtypeStruct((t, D_MODEL), F32),
        grid=(t // tm,),
        in_specs=in_specs,
        out_specs=row(D_MODEL),
        compiler_params=pltpu.CompilerParams(
            dimension_semantics=("arbitrary",), vmem_limit_bytes=V7X_VMEM_LIMIT_BYTES),
        name="mix_mlp",
    )(*args)


def _gate_weights(gate_up, gate_bias):
    hp = GLA_HEADS // 2
    up = gate_up.reshape(2, GLA_GATE_RANK, hp, 128)
    z = jnp.zeros((GLA_GATE_RANK, hp, 128), gate_up.dtype)
    top = jnp.concatenate([up[0], z], axis=-1)
    bot = jnp.concatenate([z, up[1]], axis=-1)
    up_bd = jnp.concatenate([top, bot], axis=0)
    up_bd = up_bd.transpose(1, 0, 2)[:, None].astype(BF16)
    gb = gate_bias.reshape(2, hp, 128)
    gb_bd = jnp.concatenate([gb[0], gb[1]], axis=-1)[:, None, None]
    return up_bd, gb_bd


def kernel(x, ln_emb_g, ln_emb_b, rel_bias_table, w_in, lambda_q1, lambda_k1, lambda_q2, lambda_k2,
           diff_norm_w, gla_gate_up, gla_gate_bias, gla_norm_w, w_o, ln1_g, ln1_b,
           w_ffn1, b_ffn1, w_ffn2, b_ffn2, ln2_g, ln2_b):
    b, seq, _ = x.shape
    t = b * seq
    tq, sub = 2048, 256
    band = _bias_band(rel_bias_table, sub, seq)
    w_in_b = w_in.astype(BF16)
    w_o_b = w_o.astype(BF16)
    w1_b = w_ffn1.astype(BF16)
    w2_b = w_ffn2.astype(BF16)

    h = x.reshape(t, D_MODEL)
    for li in range(DEPTH):
        emb = (ln_emb_g, ln_emb_b) if li == 0 else ()
        qd, kd, vd, gq, gk, gv, gr, gd = _inproj(li, h, w_in_b, seq, *emb)
        r3 = lambda a: a.reshape(b, seq, a.shape[-1])
        lam_init = 0.8 - 0.6 * math.exp(-0.3 * li)
        d_out = _diff_attention(r3(qd), r3(kd), vd, band, lambda_q1[li], lambda_k1[li],
                                lambda_q2[li], lambda_k2[li], diff_norm_w[li],
                                lam_init=lam_init, tq=tq, sub=sub)
        up_bd, gb_bd = _gate_weights(gla_gate_up[li], gla_gate_bias[li])
        g_out = _gla(r3(gd), up_bd, gb_bd, r3(gq), r3(gk), r3(gv), r3(gr), gla_norm_w[li])
        h = _mix_mlp(li, d_out.reshape(t, DIFF_WIDTH), g_out.reshape(t, GLA_WIDTH), h, ln_emb_g, ln_emb_b,
                     w_o_b, ln1_g, ln1_b, w1_b, b_ffn1, w2_b, b_ffn2, ln2_g, ln2_b, emb_ln=(li == 0))
    return h.reshape(b, seq, D_MODEL)
```

```python
import functools
import math

import jax
import jax.numpy as jnp
from jax import lax
from jax.experimental import pallas as pl
from jax.experimental.pallas import tpu as pltpu

D_MODEL = 1024
DEPTH = 2
DIFF_HEADS = 4
DIFF_QK_DIM = 64
DIFF_V_DIM = 128
DIFF_WIDTH = 512
GLA_HEADS = 4
GLA_WIDTH = 512
GLA_V_DIM = 128
GLA_K_DIM = 64
GLA_KEY_WIDTH = 256
GLA_GATE_RANK = 16
GLA_GATE_TAU = 16.0
GLA_CHUNK = 64
D_FF = 4096
N_BUCKETS = 32
LN_EPS = 1e-5
RMS_EPS = 1e-5
ALPHA = (2.0 * DEPTH) ** 0.25
LOG2E = math.log2(math.e)
VT_ROWS = DIFF_V_DIM + 16
D_IN = 3104

V7X_VMEM_LIMIT_BYTES = 56 * 1024 * 1024

BF16 = jnp.bfloat16
F32 = jnp.float32

_NT = (((1,), (1,)), ((), ()))
_TN = (((0,), (0,)), ((), ()))


def _dot(a, b):
    return jnp.dot(a, b, preferred_element_type=F32)


def _layer_norm(y, g, b):
    mu = jnp.mean(y, axis=-1, keepdims=True)
    d = y - mu
    var = jnp.mean(d * d, axis=-1, keepdims=True)
    return d * lax.rsqrt(var + LN_EPS) * g + b


def _band_kernel(table_ref, band_ref, *, tq, seq):
    h = pl.program_id(0)
    width = 2 * seq - tq
    lo, hi = seq - tq - 128, seq + 128
    near = hi - lo
    r = lax.broadcasted_iota(jnp.int32, (near, tq), 1)
    m = lax.broadcasted_iota(jnp.int32, (near, tq), 0) + lo
    rel = m - (seq - tq) - r
    n = jnp.abs(rel)
    n2 = n * n
    large = jnp.full_like(n, 8)
    for p in range(7, 14):
        large = large + jnp.where(n2 >= (1 << p), 1, 0)
    bucket = jnp.where(n < 8, n, large) + jnp.where(rel > 0, 16, 0)
    acc = jnp.zeros((near, tq), F32)
    for i in range(N_BUCKETS):
        acc = jnp.where(bucket == i, table_ref[i * DIFF_HEADS + h], acc)
    half = N_BUCKETS // 2
    band_ref[0, 0:lo, :] = jnp.full((lo, tq), table_ref[(half - 1) * DIFF_HEADS + h] * LOG2E, F32)
    band_ref[0, lo:hi, :] = acc * LOG2E
    band_ref[0, hi:width, :] = jnp.full((width - hi, tq), table_ref[(N_BUCKETS - 1) * DIFF_HEADS + h] * LOG2E, F32)


def _bias_band(table, tq, seq):
    width = 2 * seq - tq
    return pl.pallas_call(
        functools.partial(_band_kernel, tq=tq, seq=seq),
        out_shape=jax.ShapeDtypeStruct((DIFF_HEADS, width, tq), F32),
        grid=(DIFF_HEADS,),
        in_specs=[pl.BlockSpec(memory_space=pltpu.SMEM)],
        out_specs=pl.BlockSpec((1, width, tq), lambda h: (h, 0, 0)),
        name="bias_band",
    )(table.reshape(-1))


def _inproj_kernel(*refs, apply_ln):
    if apply_ln:
        (x_ref, g_ref, b_ref, w_ref,
         qd_ref, kd_ref, vd_ref, gq_ref, gk_ref, gv_ref, gr_ref, gd_ref) = refs
        xn = _layer_norm(x_ref[...], g_ref[...], b_ref[...])
    else:
        (x_ref, w_ref,
         qd_ref, kd_ref, vd_ref, gq_ref, gk_ref, gv_ref, gr_ref, gd_ref) = refs
        xn = x_ref[...]
    xb = xn.astype(BF16)
    qd_ref[...] = (_dot(xb, w_ref[0, :,0:512]) * (DIFF_QK_DIM ** -0.5 * LOG2E)).astype(BF16)
    kd_ref[...] = _dot(xb, w_ref[0, :,512:1024]).astype(BF16)
    tm = xb.shape[0]
    pad_rows = VT_ROWS - DIFF_V_DIM
    ones_row = (lax.broadcasted_iota(jnp.int32, (pad_rows, tm), 0) == 0).astype(BF16)
    v = _dot(xb, w_ref[0, :,1024:1536])
    for hd in range(DIFF_HEADS):
        vd_ref[0, hd, 0:DIFF_V_DIM, :] = v[:, hd * DIFF_V_DIM:(hd + 1) * DIFF_V_DIM].T.astype(BF16)
        vd_ref[0, hd, DIFF_V_DIM:VT_ROWS, :] = ones_row
    gq_ref[...] = _dot(xb, w_ref[0, :,1536:1792]) * (GLA_K_DIM ** -0.5)
    gk_ref[...] = _dot(xb, w_ref[0, :,1792:2048])
    gv_ref[...] = _dot(xb, w_ref[0, :,2048:2560]).astype(BF16)
    gr_ref[...] = _dot(xb, w_ref[0, :,2560:3072])
    gd_ref[...] = _dot(xb, w_ref[0, :,3072:D_IN])


def _inproj(li, x2d, w_pad, seq, ln_g=None, ln_b=None, *, tm=512):
    t = x2d.shape[0]
    apply_ln = ln_g is not None
    row = lambda n: pl.BlockSpec((tm, n), lambda i: (i, 0))
    const = lambda shape: pl.BlockSpec(shape, lambda i: (0,) * len(shape))
    in_specs = [row(D_MODEL)]
    args = [x2d]
    if apply_ln:
        in_specs += [const((1, D_MODEL)), const((1, D_MODEL))]
        args += [ln_g.reshape(1, -1), ln_b.reshape(1, -1)]
    in_specs.append(pl.BlockSpec((1, D_MODEL, D_IN), lambda i: (li, 0, 0), pipeline_mode=pl.Buffered(1)))
    args.append(w_pad)
    widths = [(512, BF16), (512, BF16), None, (256, F32), (256, F32),
              (512, BF16), (512, F32), (2 * GLA_GATE_RANK, F32)]
    out_shape = [jax.ShapeDtypeStruct((t, w[0]), w[1]) if w else None for w in widths]
    out_specs = [row(w[0]) if w else None for w in widths]
    tiles = seq // tm
    out_shape[2] = jax.ShapeDtypeStruct((t // seq, DIFF_HEADS, VT_ROWS, seq), BF16)
    out_specs[2] = pl.BlockSpec((1, DIFF_HEADS, VT_ROWS, tm), lambda i: (i // tiles, 0, 0, i % tiles))
    return pl.pallas_call(
        functools.partial(_inproj_kernel, apply_ln=apply_ln),
        out_shape=out_shape,
        grid=(t // tm,),
        in_specs=in_specs,
        out_specs=out_specs,
        compiler_params=pltpu.CompilerParams(
            dimension_semantics=("arbitrary",), vmem_limit_bytes=V7X_VMEM_LIMIT_BYTES),
        name="ln_inproj" if apply_ln else "inproj",
    )(*args)


def _attn_kernel(lq1_ref, lk1_ref, lq2_ref, lk2_ref, nw_ref, q_ref, k_ref, v_ref, band_ref,
                 o_ref, s_s, e_s, *, lam_init, tq, sub, seq, kc):
    qi = pl.program_id(2)
    lam = (jnp.exp(jnp.sum(lq1_ref[...] * lk1_ref[...], axis=-1, keepdims=True))
           - jnp.exp(jnp.sum(lq2_ref[...] * lk2_ref[...], axis=-1, keepdims=True)) + lam_init)
    first = lax.broadcasted_iota(jnp.int32, (1, 2 * DIFF_QK_DIM), 1) < DIFF_QK_DIM
    nsb = tq // sub
    nslots = s_s.shape[0]

    def logits(sb):
        q = q_ref[0, sb * sub:(sb + 1) * sub, :]
        zero = jnp.zeros_like(q)
        qq = jnp.concatenate([jnp.where(first, q, zero), jnp.where(first, zero, q)], axis=0)
        off = seq - sub - (qi * tq + sb * sub)
        m = jnp.full((1, 2 * sub), -jnp.inf, F32)
        for c in range(seq // kc):
            rows = slice(c * kc, (c + 1) * kc)
            s = lax.dot_general(k_ref[0, rows, :], qq, _NT, preferred_element_type=F32)
            bias = band_ref[0, pl.ds(pl.multiple_of(off + c * kc, 128), kc), :]
            s = s + jnp.concatenate([bias, bias], axis=1)
            s_s[sb % nslots, rows, :] = s
            m = jnp.maximum(m, jnp.max(s, axis=0, keepdims=True))
        return m

    def softmax_pv(sb, m):
        for c in range(seq // kc):
            rows = slice(c * kc, (c + 1) * kc)
            e_s[sb % nslots, rows, :] = jnp.exp2(s_s[sb % nslots, rows, :] - m).astype(BF16)
        ot = _dot(v_ref[0, 0], e_s[sb % nslots])
        r1 = 1.0 / ot[DIFF_V_DIM:DIFF_V_DIM + 1, 0:sub]
        r2 = lam / ot[DIFF_V_DIM:DIFF_V_DIM + 1, sub:2 * sub]
        o = ot[0:DIFF_V_DIM, 0:sub] * r1 - ot[0:DIFF_V_DIM, sub:2 * sub] * r2
        y = o * lax.rsqrt(jnp.mean(o * o, axis=0, keepdims=True) + RMS_EPS) * nw_ref[...] * (1.0 - lam_init)
        o_ref[0, sb * sub:(sb + 1) * sub, :] = y.T.astype(o_ref.dtype)

    m_next = logits(0)
    for sb in range(nsb):
        m = m_next
        if sb + 1 < nsb:
            m_next = logits(sb + 1)
        softmax_pv(sb, m)


def _diff_attention(qd, kd, vd, band, lq1, lk1, lq2, lk2, norm_w, *, lam_init, tq, sub):
    b, seq, _ = qd.shape
    nq = seq // tq
    vec = lambda n: pl.BlockSpec((1, n), lambda h, bi, qi: (0, 0))
    return pl.pallas_call(
        functools.partial(_attn_kernel, lam_init=lam_init, tq=tq, sub=sub, seq=seq, kc=512),
        out_shape=jax.ShapeDtypeStruct((b, seq, DIFF_WIDTH), BF16),
        grid=(DIFF_HEADS, b, nq),
        in_specs=[vec(DIFF_QK_DIM), vec(DIFF_QK_DIM), vec(DIFF_QK_DIM), vec(DIFF_QK_DIM),
                  pl.BlockSpec((DIFF_V_DIM, 1), lambda h, bi, qi: (0, 0)),
                  pl.BlockSpec((1, tq, 128), lambda h, bi, qi: (bi, qi, h)),
                  pl.BlockSpec((1, seq, 128), lambda h, bi, qi: (bi, 0, h)),
                  pl.BlockSpec((1, 1, VT_ROWS, seq), lambda h, bi, qi: (bi, h, 0, 0)),
                  pl.BlockSpec((1, 2 * seq - sub, sub), lambda h, bi, qi: (h, 0, 0))],
        out_specs=pl.BlockSpec((1, tq, 128), lambda h, bi, qi: (bi, qi, h)),
        scratch_shapes=[pltpu.VMEM((min(tq // sub, 3), seq, 2 * sub), F32),
                        pltpu.VMEM((min(tq // sub, 3), seq, 2 * sub), BF16)],
        compiler_params=pltpu.CompilerParams(
            dimension_semantics=("arbitrary", "arbitrary", "arbitrary"),
            vmem_limit_bytes=V7X_VMEM_LIMIT_BYTES),
        name="diff_attn",
    )(lq1.reshape(1, -1), lk1.reshape(1, -1), lq2.reshape(1, -1), lk2.reshape(1, -1),
      norm_w.reshape(-1, 1), qd, kd, vd, band)


def _chunk_scan(x, row, *, reverse):
    n = x.shape[0]
    d = 1
    while d < GLA_CHUNK:
        if reverse:
            x = x + jnp.where(row < GLA_CHUNK - d, pltpu.roll(x, n - d, 0), 0.0)
        else:
            x = x + jnp.where(row >= d, pltpu.roll(x, d, 0), 0.0)
        d *= 2
    return x


def _gla_kernel(gd_ref, up_ref, gb_ref, q_ref, k_ref, v_ref, r_ref, nw_ref, o_ref,
                qq_s, kk_s, bf_s, bb_s, kv_s, st_s, acc_s, *, seq):
    c = GLA_CHUNK
    blk = 2 * c
    nb = seq // blk
    z = _dot(gd_ref[0].astype(BF16), up_ref[0, 0]) + gb_ref[0, 0]
    g = (jnp.minimum(z, 0.0) - jnp.log(1.0 + jnp.exp(-jnp.abs(z)))) * (1.0 / GLA_GATE_TAU)
    row = lax.broadcasted_iota(jnp.int32, (seq, 1), 0) & (c - 1)
    bf_s[...] = _chunk_scan(g[:, :128], row, reverse=False)
    bb_s[...] = _chunk_scan(g[:, 128:], row, reverse=True)

    first_half = lax.broadcasted_iota(jnp.int32, (blk, 1), 0) < c
    for n in range(nb):
        rows = slice(n * blk, (n + 1) * blk)
        pf = bf_s[rows, :]
        ub = bb_s[rows, :]
        bfn = pf - jnp.where(first_half, pf[c - 1:c, :], 0.0)
        bbn = ub - jnp.where(first_half, 0.0, ub[c:c + 1, :])
        q = q_ref[0, rows, :]
        k = k_ref[0, rows, :]
        qq_s[rows, 0:128] = (q * jnp.exp(bfn)).astype(BF16)
        kk_s[rows, 0:128] = (k * jnp.exp(-bfn)).astype(BF16)
        qq_s[rows, 128:256] = (q * jnp.exp(bbn)).astype(BF16)
        kk_s[rows, 128:256] = (k * jnp.exp(-bbn)).astype(BF16)

    for n in range(nb):
        rows = slice(n * blk, (n + 1) * blk)
        kv_s[n] = lax.dot_general(v_ref[0, rows, :], kk_s[rows, :], _TN, preferred_element_type=F32)

    sr = lax.broadcasted_iota(jnp.int32, (256, 128), 0) < GLA_V_DIM
    sc = lax.broadcasted_iota(jnp.int32, (256, 128), 1) < GLA_K_DIM
    same_head = sr == sc
    sf = jnp.zeros((256, 128), F32)
    sb = jnp.zeros((256, 128), F32)
    for i in range(nb):
        t1 = bf_s[i * blk + c - 1:i * blk + c, :]
        p2 = bf_s[i * blk + blk - 1:i * blk + blk, :]
        st_s[i, :, 0:128] = (sf * jnp.exp(t1)).astype(BF16)
        sf = jnp.exp(t1 + p2) * sf + jnp.exp(p2) * jnp.where(same_head, kv_s[i, :, 0:128], 0.0)
        n = nb - 1 - i
        t2 = bb_s[n * blk + c:n * blk + c + 1, :]
        u1 = bb_s[n * blk:n * blk + 1, :]
        st_s[n, :, 128:256] = (sb * jnp.exp(t2)).astype(BF16)
        sb = jnp.exp(u1 + t2) * sb + jnp.exp(u1) * jnp.where(same_head, kv_s[n, :, 128:256], 0.0)

    lane = lax.broadcasted_iota(jnp.int32, (1, 256), 1)
    head0_v = lane < GLA_V_DIM
    quarter = [(lane >= i * GLA_K_DIM) & (lane < (i + 1) * GLA_K_DIM) for i in range(4)]
    ci = lax.broadcasted_iota(jnp.int32, (blk, 2 * blk), 0)
    si = lax.broadcasted_iota(jnp.int32, (blk, 2 * blk), 1) & (blk - 1)
    causal = si <= ci
    anti = si > ci
    for n in range(nb):
        rows = slice(n * blk, (n + 1) * blk)
        qn = qq_s[rows, :]
        kn = kk_s[rows, :]
        vn = v_ref[0, rows, :]
        zk = jnp.zeros_like(kn)
        kbd = jnp.concatenate([jnp.where(m, kn, zk) for m in quarter], axis=0)
        scores = lax.dot_general(qn, kbd, _NT, preferred_element_type=F32)
        p = (jnp.where(causal, scores[:, 0:2 * blk], 0.0)
             + jnp.where(anti, scores[:, 2 * blk:4 * blk], 0.0)).astype(BF16)
        zv = jnp.zeros_like(vn)
        vbd = jnp.concatenate([jnp.where(head0_v, vn, zv), jnp.where(head0_v, zv, vn)], axis=0)
        acc_s[rows, :] = (_dot(p, vbd)
                          + lax.dot_general(qn, st_s[n], _NT, preferred_element_type=F32))

    nw = nw_ref[...]
    for hh in range(2):
        sl = slice(hh * GLA_V_DIM, (hh + 1) * GLA_V_DIM)
        oh = acc_s[:, sl]
        y = oh * lax.rsqrt(jnp.mean(oh * oh, axis=-1, keepdims=True) + RMS_EPS) * nw
        gate = r_ref[0, :, sl]
        o_ref[0, :, sl] = (y * (gate * jax.nn.sigmoid(gate))).astype(o_ref.dtype)


def _gla(gd, up_bd, gb_bd, gq, gk, gv, gr, norm_w):
    b, seq, _ = gq.shape
    hp = GLA_HEADS // 2
    return pl.pallas_call(
        functools.partial(_gla_kernel, seq=seq),
        out_shape=jax.ShapeDtypeStruct((b, seq, GLA_WIDTH), BF16),
        grid=(b, hp),
        in_specs=[pl.BlockSpec((1, seq, 2 * GLA_GATE_RANK), lambda bi, p: (bi, 0, 0)),
                  pl.BlockSpec((1, 1, 2 * GLA_GATE_RANK, 256), lambda bi, p: (p, 0, 0, 0)),
                  pl.BlockSpec((1, 1, 1, 256), lambda bi, p: (p, 0, 0, 0)),
                  pl.BlockSpec((1, seq, 128), lambda bi, p: (bi, 0, p)),
                  pl.BlockSpec((1, seq, 128), lambda bi, p: (bi, 0, p)),
                  pl.BlockSpec((1, seq, 256), lambda bi, p: (bi, 0, p)),
                  pl.BlockSpec((1, seq, 256), lambda bi, p: (bi, 0, p)),
                  pl.BlockSpec((1, GLA_V_DIM), lambda bi, p: (0, 0))],
        out_specs=pl.BlockSpec((1, seq, 256), lambda bi, p: (bi, 0, p)),
        scratch_shapes=[pltpu.VMEM((seq, 256), BF16), pltpu.VMEM((seq, 256), BF16),
                        pltpu.VMEM((seq, 128), F32), pltpu.VMEM((seq, 128), F32),
                        pltpu.VMEM((seq // (2 * GLA_CHUNK), 256, 256), F32),
                        pltpu.VMEM((seq // (2 * GLA_CHUNK), 256, 256), BF16),
                        pltpu.VMEM((seq, 256), F32)],
        compiler_params=pltpu.CompilerParams(
            dimension_semantics=("arbitrary", "arbitrary"),
            vmem_limit_bytes=V7X_VMEM_LIMIT_BYTES),
        name="gla",
    )(gd, up_bd, gb_bd, gq, gk, gv, gr, norm_w.reshape(1, -1))


def _mix_mlp_kernel(d_ref, g_ref, x_ref, eg_ref, eb_ref, wo_ref, l1g_ref, l1b_ref, w1_ref, b1_ref, w2_ref, b2_ref,
                    l2g_ref, l2b_ref, o_ref, *, ff_chunk, emb_ln):
    tm = x_ref.shape[0]
    halves = (slice(0, tm // 2), slice(tm // 2, tm))
    x1s = []
    for rows in halves:
        mix = (_dot(d_ref[rows, :], wo_ref[0, 0:DIFF_WIDTH, :])
               + _dot(g_ref[rows, :], wo_ref[0, DIFF_WIDTH:D_MODEL, :]))
        x = x_ref[rows, :]
        if emb_ln:
            x = _layer_norm(x, eg_ref[...], eb_ref[...])
        x1s.append(_layer_norm(ALPHA * x + mix, l1g_ref[0], l1b_ref[0]))
    for rows, x1 in zip(halves, x1s):
        xb = x1.astype(BF16)
        acc = jnp.zeros(x1.shape, F32)
        for c0 in range(0, D_FF, ff_chunk):
            h = _dot(xb, w1_ref[0, :, c0:c0 + ff_chunk]) + b1_ref[0, :, c0:c0 + ff_chunk]
            h = jnp.square(jnp.maximum(h, 0.0)).astype(BF16)
            acc = acc + _dot(h, w2_ref[0, c0:c0 + ff_chunk, :])
        o_ref[rows, :] = _layer_norm(ALPHA * x1 + (acc + b2_ref[0]), l2g_ref[0], l2b_ref[0])


def _mix_mlp(li, d2d, g2d, x2d, emb_g, emb_b, w_o, ln1_g, ln1_b, w1, b1, w2, b2, ln2_g, ln2_b,
             *, emb_ln, tm=512, ff_chunk=1024):
    t = x2d.shape[0]
    row = lambda n: pl.BlockSpec((tm, n), lambda i: (i, 0))
    const = lambda shape: pl.BlockSpec(shape, lambda i: (0,) * len(shape))
    layer = lambda a: pl.BlockSpec((1,) + a.shape[1:], lambda i: (li,) + (0,) * (a.ndim - 1),
                                   pipeline_mode=pl.Buffered(1))
    vecs = [v.reshape(DEPTH, 1, -1) for v in (ln1_g, ln1_b, b1, b2, ln2_g, ln2_b)]
    l1g, l1b, b1r, b2r, l2g, l2b = vecs
    args = (d2d, g2d, x2d, emb_g.reshape(1, -1), emb_b.reshape(1, -1),
            w_o, l1g, l1b, w1, b1r, w2, b2r, l2g, l2b)
    in_specs = ([row(DIFF_WIDTH), row(GLA_WIDTH), row(D_MODEL), const((1, D_MODEL)), const((1, D_MODEL))]
                + [layer(a) for a in args[5:]])
    return pl.pallas_call(
        functools.partial(_mix_mlp_kernel, ff_chunk=ff_chunk, emb_ln=emb_ln),
        out_shape=jax.ShapeDtypeStruct((t, D_MODEL), F32),
        grid=(t // tm,),
        in_specs=in_specs,
        out_specs=row(D_MODEL),
        compiler_params=pltpu.CompilerParams(
            dimension_semantics=("arbitrary",), vmem_limit_bytes=V7X_VMEM_LIMIT_BYTES),
        name="mix_mlp",
    )(*args)


def _gate_weights(gate_up, gate_bias):
    hp = GLA_HEADS // 2
    up = gate_up.reshape(2, GLA_GATE_RANK, hp, 128)
    z = jnp.zeros((GLA_GATE_RANK, hp, 128), gate_up.dtype)
    top = jnp.concatenate([up[0], z], axis=-1)
    bot = jnp.concatenate([z, up[1]], axis=-1)
    up_bd = jnp.concatenate([top, bot], axis=0)
    up_bd = up_bd.transpose(1, 0, 2)[:, None].astype(BF16)
    gb = gate_bias.reshape(2, hp, 128)
    gb_bd = jnp.concatenate([gb[0], gb[1]], axis=-1)[:, None, None]
    return up_bd, gb_bd


def kernel(x, ln_emb_g, ln_emb_b, rel_bias_table, w_in, lambda_q1, lambda_k1, lambda_q2, lambda_k2,
           diff_norm_w, gla_gate_up, gla_gate_bias, gla_norm_w, w_o, ln1_g, ln1_b,
           w_ffn1, b_ffn1, w_ffn2, b_ffn2, ln2_g, ln2_b):
    b, seq, _ = x.shape
    t = b * seq
    tq, sub = 2048, 256
    band = _bias_band(rel_bias_table, sub, seq)
    w_in_b = w_in.astype(BF16)
    w_o_b = w_o.astype(BF16)
    w1_b = w_ffn1.astype(BF16)
    w2_b = w_ffn2.astype(BF16)

    h = x.reshape(t, D_MODEL)
    for li in range(DEPTH):
        emb = (ln_emb_g, ln_emb_b) if li == 0 else ()
        qd, kd, vd, gq, gk, gv, gr, gd = _inproj(li, h, w_in_b, seq, *emb)
        r3 = lambda a: a.reshape(b, seq, a.shape[-1])
        lam_init = 0.8 - 0.6 * math.exp(-0.3 * li)
        d_out = _diff_attention(r3(qd), r3(kd), vd, band, lambda_q1[li], lambda_k1[li],
                                lambda_q2[li], lambda_k2[li], diff_norm_w[li],
                                lam_init=lam_init, tq=tq, sub=sub)
        up_bd, gb_bd = _gate_weights(gla_gate_up[li], gla_gate_bias[li])
        g_out = _gla(r3(gd), up_bd, gb_bd, r3(gq), r3(gk), r3(gv), r3(gr), gla_norm_w[li])
        h = _mix_mlp(li, d_out.reshape(t, DIFF_WIDTH), g_out.reshape(t, GLA_WIDTH), h, ln_emb_g, ln_emb_b,
                     w_o_b, ln1_g, ln1_b, w1_b, b_ffn1, w2_b, b_ffn2, ln2_g, ln2_b, emb_ln=(li == 0))
    return h.reshape(b, seq, D_MODEL)
```

```python
import functools
import math

import jax
import jax.numpy as jnp
from jax import lax
from jax.experimental import pallas as pl
from jax.experimental.pallas import tpu as pltpu

D_MODEL = 1024
DEPTH = 2
DIFF_HEADS = 4
DIFF_QK_DIM = 64
DIFF_V_DIM = 128
DIFF_WIDTH = 512
GLA_HEADS = 4
GLA_WIDTH = 512
GLA_V_DIM = 128
GLA_K_DIM = 64
GLA_KEY_WIDTH = 256
GLA_GATE_RANK = 16
GLA_GATE_TAU = 16.0
GLA_CHUNK = 64
D_FF = 4096
N_BUCKETS = 32
LN_EPS = 1e-5
RMS_EPS = 1e-5
ALPHA = (2.0 * DEPTH) ** 0.25
LOG2E = math.log2(math.e)
VT_ROWS = DIFF_V_DIM + 16
D_IN = 3104

V7X_VMEM_LIMIT_BYTES = 56 * 1024 * 1024

BF16 = jnp.bfloat16
F32 = jnp.float32

_NT = (((1,), (1,)), ((), ()))
_TN = (((0,), (0,)), ((), ()))


def _dot(a, b):
    return jnp.dot(a, b, preferred_element_type=F32)


def _layer_norm(y, g, b):
    mu = jnp.mean(y, axis=-1, keepdims=True)
    d = y - mu
    var = jnp.mean(d * d, axis=-1, keepdims=True)
    return d * lax.rsqrt(var + LN_EPS) * g + b


def _band_kernel(table_ref, band_ref, *, tq, seq):
    h = pl.program_id(0)
    width = 2 * seq - tq
    lo, hi = seq - tq - 128, seq + 128
    near = hi - lo
    r = lax.broadcasted_iota(jnp.int32, (near, tq), 1)
    m = lax.broadcasted_iota(jnp.int32, (near, tq), 0) + lo
    rel = m - (seq - tq) - r
    n = jnp.abs(rel)
    n2 = n * n
    large = jnp.full_like(n, 8)
    for p in range(7, 14):
        large = large + jnp.where(n2 >= (1 << p), 1, 0)
    bucket = jnp.where(n < 8, n, large) + jnp.where(rel > 0, 16, 0)
    acc = jnp.zeros((near, tq), F32)
    for i in range(N_BUCKETS):
        acc = jnp.where(bucket == i, table_ref[i * DIFF_HEADS + h], acc)
    half = N_BUCKETS // 2
    band_ref[0, 0:lo, :] = jnp.full((lo, tq), table_ref[(half - 1) * DIFF_HEADS + h] * LOG2E, F32)
    band_ref[0, lo:hi, :] = acc * LOG2E
    band_ref[0, hi:width, :] = jnp.full((width - hi, tq), table_ref[(N_BUCKETS - 1) * DIFF_HEADS + h] * LOG2E, F32)


def _bias_band(table, tq, seq):
    width = 2 * seq - tq
    return pl.pallas_call(
        functools.partial(_band_kernel, tq=tq, seq=seq),
        out_shape=jax.ShapeDtypeStruct((DIFF_HEADS, width, tq), F32),
        grid=(DIFF_HEADS,),
        in_specs=[pl.BlockSpec(memory_space=pltpu.SMEM)],
        out_specs=pl.BlockSpec((1, width, tq), lambda h: (h, 0, 0)),
        name="bias_band",
    )(table.reshape(-1))


def _inproj_kernel(*refs, apply_ln):
    if apply_ln:
        (x_ref, g_ref, b_ref, w_ref,
         qd_ref, kd_ref, vd_ref, gq_ref, gk_ref, gv_ref, gr_ref, gd_ref) = refs
        xn = _layer_norm(x_ref[...], g_ref[...], b_ref[...])
    else:
        (x_ref, w_ref,
         qd_ref, kd_ref, vd_ref, gq_ref, gk_ref, gv_ref, gr_ref, gd_ref) = refs
        xn = x_ref[...]
    xb = xn.astype(BF16)
    qd_ref[...] = (_dot(xb, w_ref[0, :,0:512]) * (DIFF_QK_DIM ** -0.5 * LOG2E)).astype(BF16)
    kd_ref[...] = _dot(xb, w_ref[0, :,512:1024]).astype(BF16)
    tm = xb.shape[0]
    pad_rows = VT_ROWS - DIFF_V_DIM
    ones_row = (lax.broadcasted_iota(jnp.int32, (pad_rows, tm), 0) == 0).astype(BF16)
    v = _dot(xb, w_ref[0, :,1024:1536])
    for hd in range(DIFF_HEADS):
        vd_ref[0, hd, 0:DIFF_V_DIM, :] = v[:, hd * DIFF_V_DIM:(hd + 1) * DIFF_V_DIM].T.astype(BF16)
        vd_ref[0, hd, DIFF_V_DIM:VT_ROWS, :] = ones_row
    gq_ref[...] = _dot(xb, w_ref[0, :,1536:1792]) * (GLA_K_DIM ** -0.5)
    gk_ref[...] = _dot(xb, w_ref[0, :,1792:2048])
    gv_ref[...] = _dot(xb, w_ref[0, :,2048:2560]).astype(BF16)
    gr_ref[...] = _dot(xb, w_ref[0, :,2560:3072])
    gd_ref[...] = _dot(xb, w_ref[0, :,3072:D_IN])


def _inproj(li, x2d, w_pad, seq, ln_g=None, ln_b=None, *, tm=512):
    t = x2d.shape[0]
    apply_ln = ln_g is not None
    row = lambda n: pl.BlockSpec((tm, n), lambda i: (i, 0))
    const = lambda shape: pl.BlockSpec(shape, lambda i: (0,) * len(shape))
    in_specs = [row(D_MODEL)]
    args = [x2d]
    if apply_ln:
        in_specs += [const((1, D_MODEL)), const((1, D_MODEL))]
        args += [ln_g.reshape(1, -1), ln_b.reshape(1, -1)]
    in_specs.append(pl.BlockSpec((1, D_MODEL, D_IN), lambda i: (li, 0, 0), pipeline_mode=pl.Buffered(1)))
    args.append(w_pad)
    widths = [(512, BF16), (512, BF16), None, (256, F32), (256, F32),
              (512, BF16), (512, F32), (2 * GLA_GATE_RANK, F32)]
    out_shape = [jax.ShapeDtypeStruct((t, w[0]), w[1]) if w else None for w in widths]
    out_specs = [row(w[0]) if w else None for w in widths]
    tiles = seq // tm
    out_shape[2] = jax.ShapeDtypeStruct((t // seq, DIFF_HEADS, VT_ROWS, seq), BF16)
    out_specs[2] = pl.BlockSpec((1, DIFF_HEADS, VT_ROWS, tm), lambda i: (i // tiles, 0, 0, i % tiles))
    return pl.pallas_call(
        functools.partial(_inproj_kernel, apply_ln=apply_ln),
        out_shape=out_shape,
        grid=(t // tm,),
        in_specs=in_specs,
        out_specs=out_specs,
        compiler_params=pltpu.CompilerParams(
            dimension_semantics=("arbitrary",), vmem_limit_bytes=V7X_VMEM_LIMIT_BYTES),
        name="ln_inproj" if apply_ln else "inproj",
    )(*args)


def _attn_kernel(lq1_ref, lk1_ref, lq2_ref, lk2_ref, nw_ref, q_ref, k_ref, v_ref, band_ref,
                 o_ref, s_s, e_s, *, lam_init, tq, sub, seq, kc):
    qi = pl.program_id(2)
    lam = (jnp.exp(jnp.sum(lq1_ref[...] * lk1_ref[...], axis=-1, keepdims=True))
           - jnp.exp(jnp.sum(lq2_ref[...] * lk2_ref[...], axis=-1, keepdims=True)) + lam_init)
    first = lax.broadcasted_iota(jnp.int32, (1, 2 * DIFF_QK_DIM), 1) < DIFF_QK_DIM
    nsb = tq // sub
    nslots = s_s.shape[0]

    def logits(sb):
        q = q_ref[0, sb * sub:(sb + 1) * sub, :]
        zero = jnp.zeros_like(q)
        qq = jnp.concatenate([jnp.where(first, q, zero), jnp.where(first, zero, q)], axis=0)
        off = seq - sub - (qi * tq + sb * sub)
        m = jnp.full((1, 2 * sub), -jnp.inf, F32)
        for c in range(seq // kc):
            rows = slice(c * kc, (c + 1) * kc)
            s = lax.dot_general(k_ref[0, rows, :], qq, _NT, preferred_element_type=F32)
            bias = band_ref[0, pl.ds(pl.multiple_of(off + c * kc, 128), kc), :]
            s = s + jnp.concatenate([bias, bias], axis=1)
            s_s[sb % nslots, rows, :] = s
            m = jnp.maximum(m, jnp.max(s, axis=0, keepdims=True))
        return m

    def softmax_pv(sb, m):
        for c in range(seq // kc):
            rows = slice(c * kc, (c + 1) * kc)
            e_s[sb % nslots, rows, :] = jnp.exp2(s_s[sb % nslots, rows, :] - m).astype(BF16)
        ot = _dot(v_ref[0, 0], e_s[sb % nslots])
        r1 = 1.0 / ot[DIFF_V_DIM:DIFF_V_DIM + 1, 0:sub]
        r2 = lam / ot[DIFF_V_DIM:DIFF_V_DIM + 1, sub:2 * sub]
        o = ot[0:DIFF_V_DIM, 0:sub] * r1 - ot[0:DIFF_V_DIM, sub:2 * sub] * r2
        y = o * lax.rsqrt(jnp.mean(o * o, axis=0, keepdims=True) + RMS_EPS) * nw_ref[...] * (1.0 - lam_init)
        o_ref[0, sb * sub:(sb + 1) * sub, :] = y.T.astype(o_ref.dtype)

    m_next = logits(0)
    for sb in range(nsb):
        m = m_next
        if sb + 1 < nsb:
            m_next = logits(sb + 1)
        softmax_pv(sb, m)


def _diff_attention(qd, kd, vd, band, lq1, lk1, lq2, lk2, norm_w, *, lam_init, tq, sub):
    b, seq, _ = qd.shape
    nq = seq // tq
    vec = lambda n: pl.BlockSpec((1, n), lambda h, bi, qi: (0, 0))
    return pl.pallas_call(
        functools.partial(_attn_kernel, lam_init=lam_init, tq=tq, sub=sub, seq=seq, kc=512),
        out_shape=jax.ShapeDtypeStruct((b, seq, DIFF_WIDTH), BF16),
        grid=(DIFF_HEADS, b, nq),
        in_specs=[vec(DIFF_QK_DIM), vec(DIFF_QK_DIM), vec(DIFF_QK_DIM), vec(DIFF_QK_DIM),
                  pl.BlockSpec((DIFF_V_DIM, 1), lambda h, bi, qi: (0, 0)),
                  pl.BlockSpec((1, tq, 128), lambda h, bi, qi: (bi, qi, h)),
                  pl.BlockSpec((1, seq, 128), lambda h, bi, qi: (bi, 0, h)),
                  pl.BlockSpec((1, 1, VT_ROWS, seq), lambda h, bi, qi: (bi, h, 0, 0)),
                  pl.BlockSpec((1, 2 * seq - sub, sub), lambda h, bi, qi: (h, 0, 0))],
        out_specs=pl.BlockSpec((1, tq, 128), lambda h, bi, qi: (bi, qi, h)),
        scratch_shapes=[pltpu.VMEM((min(tq // sub, 3), seq, 2 * sub), F32),
                        pltpu.VMEM((min(tq // sub, 3), seq, 2 * sub), BF16)],
        compiler_params=pltpu.CompilerParams(
            dimension_semantics=("arbitrary", "arbitrary", "arbitrary"),
            vmem_limit_bytes=V7X_VMEM_LIMIT_BYTES),
        name="diff_attn",
    )(lq1.reshape(1, -1), lk1.reshape(1, -1), lq2.reshape(1, -1), lk2.reshape(1, -1),
      norm_w.reshape(-1, 1), qd, kd, vd, band)


def _chunk_scan(x, row, *, reverse):
    n = x.shape[0]
    d = 1
    while d < GLA_CHUNK:
        if reverse:
            x = x + jnp.where(row < GLA_CHUNK - d, pltpu.roll(x, n - d, 0), 0.0)
        else:
            x = x + jnp.where(row >= d, pltpu.roll(x, d, 0), 0.0)
        d *= 2
    return x


def _gla_kernel(gd_ref, up_ref, gb_ref, q_ref, k_ref, v_ref, r_ref, nw_ref, o_ref,
                qq_s, kk_s, bf_s, bb_s, kv_s, st_s, *, seq):
    c = GLA_CHUNK
    blk = 2 * c
    nb = seq // blk

    row = lax.broadcasted_iota(jnp.int32, (blk, 1), 0) & (c - 1)
    first_half = lax.broadcasted_iota(jnp.int32, (blk, 1), 0) < c

    def gates(hp, n):
        rows = slice(n * blk, (n + 1) * blk)
        z = _dot(gd_ref[0, rows, :].astype(BF16), up_ref[hp, 0]) + gb_ref[hp, 0]
        g = (jnp.minimum(z, 0.0) - jnp.log(1.0 + jnp.exp(-jnp.abs(z)))) * (1.0 / GLA_GATE_TAU)
        pf = _chunk_scan(g[:, :128], row, reverse=False)
        ub = _chunk_scan(g[:, 128:], row, reverse=True)
        bf_s[hp, rows, :] = pf
        bb_s[hp, rows, :] = ub
        bfn = pf - jnp.where(first_half, pf[c - 1:c, :], 0.0)
        bbn = ub - jnp.where(first_half, 0.0, ub[c:c + 1, :])
        q = q_ref[0, rows, hp * 128:(hp + 1) * 128]
        k = k_ref[0, rows, hp * 128:(hp + 1) * 128]
        qq_s[hp, rows, 0:128] = (q * jnp.exp(bfn)).astype(BF16)
        kk_s[hp, rows, 0:128] = (k * jnp.exp(-bfn)).astype(BF16)
        qq_s[hp, rows, 128:256] = (q * jnp.exp(bbn)).astype(BF16)
        kk_s[hp, rows, 128:256] = (k * jnp.exp(-bbn)).astype(BF16)

    def kv_scan(hp):
        for n in range(nb):
            rows = slice(n * blk, (n + 1) * blk)
            kv_s[hp, n] = lax.dot_general(v_ref[0, rows, hp * 256:(hp + 1) * 256], kk_s[hp, rows, :], _TN,
                                          preferred_element_type=F32)
        sr = lax.broadcasted_iota(jnp.int32, (256, 128), 0) < GLA_V_DIM
        sc = lax.broadcasted_iota(jnp.int32, (256, 128), 1) < GLA_K_DIM
        same_head = sr == sc
        sf = jnp.zeros((256, 128), F32)
        sb = jnp.zeros((256, 128), F32)
        for i in range(nb):
            t1 = bf_s[hp, i * blk + c - 1:i * blk + c, :]
            p2 = bf_s[hp, i * blk + blk - 1:i * blk + blk, :]
            st_s[hp, i, :, 0:128] = (sf * jnp.exp(t1)).astype(BF16)
            sf = jnp.exp(t1 + p2) * sf + jnp.exp(p2) * jnp.where(same_head, kv_s[hp, i, :, 0:128], 0.0)
            n = nb - 1 - i
            t2 = bb_s[hp, n * blk + c:n * blk + c + 1, :]
            u1 = bb_s[hp, n * blk:n * blk + 1, :]
            st_s[hp, n, :, 128:256] = (sb * jnp.exp(t2)).astype(BF16)
            sb = jnp.exp(u1 + t2) * sb + jnp.exp(u1) * jnp.where(same_head, kv_s[hp, n, :, 128:256], 0.0)

    lane = lax.broadcasted_iota(jnp.int32, (1, 256), 1)
    head0_v = lane < GLA_V_DIM
    quarter = [(lane >= i * GLA_K_DIM) & (lane < (i + 1) * GLA_K_DIM) for i in range(4)]
    ci = lax.broadcasted_iota(jnp.int32, (blk, 2 * blk), 0)
    si = lax.broadcasted_iota(jnp.int32, (blk, 2 * blk), 1) & (blk - 1)
    causal = si <= ci
    anti = si > ci

    def scores_of(hp, n):
        rows = slice(n * blk, (n + 1) * blk)
        kn = kk_s[hp, rows, :]
        zk = jnp.zeros_like(kn)
        kbd = jnp.concatenate([jnp.where(m, kn, zk) for m in quarter], axis=0)
        return lax.dot_general(qq_s[hp, rows, :], kbd, _NT, preferred_element_type=F32)

    def outputs(hp, n, scores):
        rows = slice(n * blk, (n + 1) * blk)
        qn = qq_s[hp, rows, :]
        vn = v_ref[0, rows, hp * 256:(hp + 1) * 256]
        prob = (jnp.where(causal, scores[:, 0:2 * blk], 0.0)
                + jnp.where(anti, scores[:, 2 * blk:4 * blk], 0.0)).astype(BF16)
        zv = jnp.zeros_like(vn)
        vbd = jnp.concatenate([jnp.where(head0_v, vn, zv), jnp.where(head0_v, zv, vn)], axis=0)
        o = _dot(prob, vbd) + lax.dot_general(qn, st_s[hp, n], _NT, preferred_element_type=F32)
        nw = nw_ref[...]
        for hh in range(2):
            cols = slice(hp * 256 + hh * GLA_V_DIM, hp * 256 + (hh + 1) * GLA_V_DIM)
            oh = o[:, hh * GLA_V_DIM:(hh + 1) * GLA_V_DIM]
            y = oh * lax.rsqrt(jnp.mean(oh * oh, axis=-1, keepdims=True) + RMS_EPS) * nw
            gate = r_ref[0, rows, cols]
            o_ref[0, rows, cols] = (y * (gate * jax.nn.sigmoid(gate))).astype(o_ref.dtype)

    for n in range(nb):
        gates(0, n)
    kv_scan(0)
    for hp in range(2):
        sc_next = scores_of(hp, 0)
        for n in range(nb):
            sc = sc_next
            if n + 1 < nb:
                sc_next = scores_of(hp, n + 1)
            outputs(hp, n, sc)
            if hp == 0:
                gates(1, n)
        if hp == 0:
            kv_scan(1)


def _gla(gd, up_bd, gb_bd, gq, gk, gv, gr, norm_w):
    b, seq, _ = gq.shape
    hp = GLA_HEADS // 2
    return pl.pallas_call(
        functools.partial(_gla_kernel, seq=seq),
        out_shape=jax.ShapeDtypeStruct((b, seq, GLA_WIDTH), BF16),
        grid=(b,),
        in_specs=[pl.BlockSpec((1, seq, 2 * GLA_GATE_RANK), lambda bi: (bi, 0, 0)),
                  pl.BlockSpec((hp, 1, 2 * GLA_GATE_RANK, 256), lambda bi: (0, 0, 0, 0)),
                  pl.BlockSpec((hp, 1, 1, 256), lambda bi: (0, 0, 0, 0)),
                  pl.BlockSpec((1, seq, GLA_KEY_WIDTH), lambda bi: (bi, 0, 0)),
                  pl.BlockSpec((1, seq, GLA_KEY_WIDTH), lambda bi: (bi, 0, 0)),
                  pl.BlockSpec((1, seq, GLA_WIDTH), lambda bi: (bi, 0, 0)),
                  pl.BlockSpec((1, seq, GLA_WIDTH), lambda bi: (bi, 0, 0)),
                  pl.BlockSpec((1, GLA_V_DIM), lambda bi: (0, 0))],
        out_specs=pl.BlockSpec((1, seq, GLA_WIDTH), lambda bi: (bi, 0, 0)),
        scratch_shapes=[pltpu.VMEM((hp, seq, 256), BF16), pltpu.VMEM((hp, seq, 256), BF16),
                        pltpu.VMEM((hp, seq, 128), F32), pltpu.VMEM((hp, seq, 128), F32),
                        pltpu.VMEM((hp, seq // (2 * GLA_CHUNK), 256, 256), F32),
                        pltpu.VMEM((hp, seq // (2 * GLA_CHUNK), 256, 256), BF16)],
        compiler_params=pltpu.CompilerParams(
            dimension_semantics=("arbitrary",),
            vmem_limit_bytes=V7X_VMEM_LIMIT_BYTES),
        name="gla",
    )(gd, up_bd, gb_bd, gq, gk, gv, gr, norm_w.reshape(1, -1))


def _mix_mlp_kernel(d_ref, g_ref, x_ref, eg_ref, eb_ref, wo_ref, l1g_ref, l1b_ref, w1_ref, b1_ref, w2_ref, b2_ref,
                    l2g_ref, l2b_ref, o_ref, *, ff_chunk, emb_ln):
    tm = x_ref.shape[0]
    halves = (slice(0, tm // 2), slice(tm // 2, tm))
    x1s = []
    for rows in halves:
        mix = (_dot(d_ref[rows, :], wo_ref[0, 0:DIFF_WIDTH, :])
               + _dot(g_ref[rows, :], wo_ref[0, DIFF_WIDTH:D_MODEL, :]))
        x = x_ref[rows, :]
        if emb_ln:
            x = _layer_norm(x, eg_ref[...], eb_ref[...])
        x1s.append(_layer_norm(ALPHA * x + mix, l1g_ref[0], l1b_ref[0]))
    for rows, x1 in zip(halves, x1s):
        xb = x1.astype(BF16)
        acc = jnp.zeros(x1.shape, F32)
        for c0 in range(0, D_FF, ff_chunk):
            h = _dot(xb, w1_ref[0, :, c0:c0 + ff_chunk]) + b1_ref[0, :, c0:c0 + ff_chunk]
            h = jnp.square(jnp.maximum(h, 0.0)).astype(BF16)
            acc = acc + _dot(h, w2_ref[0, c0:c0 + ff_chunk, :])
        o_ref[rows, :] = _layer_norm(ALPHA * x1 + (acc + b2_ref[0]), l2g_ref[0], l2b_ref[0])


def _mix_mlp(li, d2d, g2d, x2d, emb_g, emb_b, w_o, ln1_g, ln1_b, w1, b1, w2, b2, ln2_g, ln2_b,
             *, emb_ln, tm=512, ff_chunk=1024):
    t = x2d.shape[0]
    row = lambda n: pl.BlockSpec((tm, n), lambda i: (i, 0))
    const = lambda shape: pl.BlockSpec(shape, lambda i: (0,) * len(shape))
    layer = lambda a: pl.BlockSpec((1,) + a.shape[1:], lambda i: (li,) + (0,) * (a.ndim - 1),
                                   pipeline_mode=pl.Buffered(1))
    vecs = [v.reshape(DEPTH, 1, -1) for v in (ln1_g, ln1_b, b1, b2, ln2_g, ln2_b)]
    l1g, l1b, b1r, b2r, l2g, l2b = vecs
    args = (d2d, g2d, x2d, emb_g.reshape(1, -1), emb_b.reshape(1, -1),
            w_o, l1g, l1b, w1, b1r, w2, b2r, l2g, l2b)
    in_specs = ([row(DIFF_WIDTH), row(GLA_WIDTH), row(D_MODEL), const((1, D_MODEL)), const((1, D_MODEL))]
                + [layer(a) for a in args[5:]])
    return pl.pallas_call(
        functools.partial(_mix_mlp_kernel, ff_chunk=ff_chunk, emb_ln=emb_ln),
        out_shape=jax.ShapeDtypeStruct((t, D_MODEL), F32),
        grid=(t // tm,),
        in_specs=in_specs,
        out_specs=row(D_MODEL),
        compiler_params=pltpu.CompilerParams(
            dimension_semantics=("arbitrary",), vmem_limit_bytes=V7X_VMEM_LIMIT_BYTES),
        name="mix_mlp",
    )(*args)


def _gate_weights(gate_up, gate_bias):
    hp = GLA_HEADS // 2
    up = gate_up.reshape(2, GLA_GATE_RANK, hp, 128)
    z = jnp.zeros((GLA_GATE_RANK, hp, 128), gate_up.dtype)
    top = jnp.concatenate([up[0], z], axis=-1)
    bot = jnp.concatenate([z, up[1]], axis=-1)
    up_bd = jnp.concatenate([top, bot], axis=0)
    up_bd = up_bd.transpose(1, 0, 2)[:, None].astype(BF16)
    gb = gate_bias.reshape(2, hp, 128)
    gb_bd = jnp.concatenate([gb[0], gb[1]], axis=-1)[:, None, None]
    return up_bd, gb_bd


def kernel(x, ln_emb_g, ln_emb_b, rel_bias_table, w_in, lambda_q1, lambda_k1, lambda_q2, lambda_k2,
           diff_norm_w, gla_gate_up, gla_gate_bias, gla_norm_w, w_o, ln1_g, ln1_b,
           w_ffn1, b_ffn1, w_ffn2, b_ffn2, ln2_g, ln2_b):
    b, seq, _ = x.shape
    t = b * seq
    tq, sub = 2048, 256
    band = _bias_band(rel_bias_table, sub, seq)
    w_in_b = w_in.astype(BF16)
    w_o_b = w_o.astype(BF16)
    w1_b = w_ffn1.astype(BF16)
    w2_b = w_ffn2.astype(BF16)

    h = x.reshape(t, D_MODEL)
    for li in range(DEPTH):
        emb = (ln_emb_g, ln_emb_b) if li == 0 else ()
        qd, kd, vd, gq, gk, gv, gr, gd = _inproj(li, h, w_in_b, seq, *emb)
        r3 = lambda a: a.reshape(b, seq, a.shape[-1])
        lam_init = 0.8 - 0.6 * math.exp(-0.3 * li)
        d_out = _diff_attention(r3(qd), r3(kd), vd, band, lambda_q1[li], lambda_k1[li],
                                lambda_q2[li], lambda_k2[li], diff_norm_w[li],
                                lam_init=lam_init, tq=tq, sub=sub)
        up_bd, gb_bd = _gate_weights(gla_gate_up[li], gla_gate_bias[li])
        g_out = _gla(r3(gd), up_bd, gb_bd, r3(gq), r3(gk), r3(gv), r3(gr), gla_norm_w[li])
        h = _mix_mlp(li, d_out.reshape(t, DIFF_WIDTH), g_out.reshape(t, GLA_WIDTH), h, ln_emb_g, ln_emb_b,
                     w_o_b, ln1_g, ln1_b, w1_b, b_ffn1, w2_b, b_ffn2, ln2_g, ln2_b, emb_ln=(li == 0))
    return h.reshape(b, seq, D_MODEL)
```

```python
import functools
import math

import jax
import jax.numpy as jnp
from jax import lax
from jax.experimental import pallas as pl
from jax.experimental.pallas import tpu as pltpu

D_MODEL = 1024
DEPTH = 2
DIFF_HEADS = 4
DIFF_QK_DIM = 64
DIFF_V_DIM = 128
DIFF_WIDTH = 512
GLA_HEADS = 4
GLA_WIDTH = 512
GLA_V_DIM = 128
GLA_K_DIM = 64
GLA_KEY_WIDTH = 256
GLA_GATE_RANK = 16
GLA_GATE_TAU = 16.0
GLA_CHUNK = 64
D_FF = 4096
N_BUCKETS = 32
LN_EPS = 1e-5
RMS_EPS = 1e-5
ALPHA = (2.0 * DEPTH) ** 0.25
LOG2E = math.log2(math.e)
VT_ROWS = DIFF_V_DIM + 16
D_IN = 3104

V7X_VMEM_LIMIT_BYTES = 56 * 1024 * 1024

BF16 = jnp.bfloat16
F32 = jnp.float32

_NT = (((1,), (1,)), ((), ()))
_TN = (((0,), (0,)), ((), ()))


def _dot(a, b):
    return jnp.dot(a, b, preferred_element_type=F32)


def _layer_norm(y, g, b):
    mu = jnp.mean(y, axis=-1, keepdims=True)
    d = y - mu
    var = jnp.mean(d * d, axis=-1, keepdims=True)
    return d * lax.rsqrt(var + LN_EPS) * g + b


def _band_kernel(table_ref, band_ref, *, tq, seq):
    h = pl.program_id(0)
    width = 2 * seq - tq
    lo, hi = seq - tq - 128, seq + 128
    near = hi - lo
    r = lax.broadcasted_iota(jnp.int32, (near, tq), 1)
    m = lax.broadcasted_iota(jnp.int32, (near, tq), 0) + lo
    rel = m - (seq - tq) - r
    n = jnp.abs(rel)
    n2 = n * n
    large = jnp.full_like(n, 8)
    for p in range(7, 14):
        large = large + jnp.where(n2 >= (1 << p), 1, 0)
    bucket = jnp.where(n < 8, n, large) + jnp.where(rel > 0, 16, 0)
    acc = jnp.zeros((near, tq), F32)
    for i in range(N_BUCKETS):
        acc = jnp.where(bucket == i, table_ref[i * DIFF_HEADS + h], acc)
    half = N_BUCKETS // 2
    band_ref[0, 0:lo, :] = jnp.full((lo, tq), table_ref[(half - 1) * DIFF_HEADS + h] * LOG2E, F32)
    band_ref[0, lo:hi, :] = acc * LOG2E
    band_ref[0, hi:width, :] = jnp.full((width - hi, tq), table_ref[(N_BUCKETS - 1) * DIFF_HEADS + h] * LOG2E, F32)


def _bias_band(table, tq, seq):
    width = 2 * seq - tq
    return pl.pallas_call(
        functools.partial(_band_kernel, tq=tq, seq=seq),
        out_shape=jax.ShapeDtypeStruct((DIFF_HEADS, width, tq), F32),
        grid=(DIFF_HEADS,),
        in_specs=[pl.BlockSpec(memory_space=pltpu.SMEM)],
        out_specs=pl.BlockSpec((1, width, tq), lambda h: (h, 0, 0)),
        name="bias_band",
    )(table.reshape(-1))


def _inproj_kernel(*refs, apply_ln):
    if apply_ln:
        (x_ref, g_ref, b_ref, wf_ref,
         qd_ref, kd_ref, vd_ref, gq_ref, gk_ref, gv_ref, gr_ref, gd_ref, w_ref) = refs
        xn = _layer_norm(x_ref[...], g_ref[...], b_ref[...])
    else:
        (x_ref, wf_ref,
         qd_ref, kd_ref, vd_ref, gq_ref, gk_ref, gv_ref, gr_ref, gd_ref, w_ref) = refs
        xn = x_ref[...]

    @pl.when(pl.program_id(0) == 0)
    def _():
        w_ref[0] = wf_ref[0].astype(BF16)

    xb = xn.astype(BF16)
    qd_ref[...] = (_dot(xb, w_ref[0, :,0:512]) * (DIFF_QK_DIM ** -0.5 * LOG2E)).astype(BF16)
    kd_ref[...] = _dot(xb, w_ref[0, :,512:1024]).astype(BF16)
    tm = xb.shape[0]
    pad_rows = VT_ROWS - DIFF_V_DIM
    ones_row = (lax.broadcasted_iota(jnp.int32, (pad_rows, tm), 0) == 0).astype(BF16)
    v = _dot(xb, w_ref[0, :,1024:1536])
    for hd in range(DIFF_HEADS):
        vd_ref[0, hd, 0:DIFF_V_DIM, :] = v[:, hd * DIFF_V_DIM:(hd + 1) * DIFF_V_DIM].T.astype(BF16)
        vd_ref[0, hd, DIFF_V_DIM:VT_ROWS, :] = ones_row
    gq_ref[...] = _dot(xb, w_ref[0, :,1536:1792]) * (GLA_K_DIM ** -0.5)
    gk_ref[...] = _dot(xb, w_ref[0, :,1792:2048])
    gv_ref[...] = _dot(xb, w_ref[0, :,2048:2560]).astype(BF16)
    gr_ref[...] = _dot(xb, w_ref[0, :,2560:3072])
    gd_ref[...] = _dot(xb, w_ref[0, :,3072:D_IN])


def _inproj(li, x2d, w_in, seq, ln_g=None, ln_b=None, *, tm=512):
    t = x2d.shape[0]
    apply_ln = ln_g is not None
    row = lambda n: pl.BlockSpec((tm, n), lambda i: (i, 0))
    const = lambda shape: pl.BlockSpec(shape, lambda i: (0,) * len(shape))
    in_specs = [row(D_MODEL)]
    args = [x2d]
    if apply_ln:
        in_specs += [const((1, D_MODEL)), const((1, D_MODEL))]
        args += [ln_g.reshape(1, -1), ln_b.reshape(1, -1)]
    in_specs.append(pl.BlockSpec((1, D_MODEL, D_IN), lambda i: (li, 0, 0), pipeline_mode=pl.Buffered(1)))
    args.append(w_in)
    widths = [(512, BF16), (512, BF16), None, (256, F32), (256, F32),
              (512, BF16), (512, F32), (2 * GLA_GATE_RANK, F32)]
    out_shape = [jax.ShapeDtypeStruct((t, w[0]), w[1]) if w else None for w in widths]
    out_specs = [row(w[0]) if w else None for w in widths]
    tiles = seq // tm
    out_shape[2] = jax.ShapeDtypeStruct((t // seq, DIFF_HEADS, VT_ROWS, seq), BF16)
    out_specs[2] = pl.BlockSpec((1, DIFF_HEADS, VT_ROWS, tm), lambda i: (i // tiles, 0, 0, i % tiles))
    return pl.pallas_call(
        functools.partial(_inproj_kernel, apply_ln=apply_ln),
        out_shape=out_shape,
        grid=(t // tm,),
        in_specs=in_specs,
        out_specs=out_specs,
        scratch_shapes=[pltpu.VMEM((1, D_MODEL, D_IN), BF16)],
        compiler_params=pltpu.CompilerParams(
            dimension_semantics=("arbitrary",), vmem_limit_bytes=V7X_VMEM_LIMIT_BYTES),
        name="ln_inproj" if apply_ln else "inproj",
    )(*args)


def _attn_kernel(lq1_ref, lk1_ref, lq2_ref, lk2_ref, nw_ref, q_ref, k_ref, v_ref, band_ref,
                 o_ref, s_s, e_s, *, lam_init, tq, sub, seq, kc):
    qi = pl.program_id(2)
    lam = (jnp.exp(jnp.sum(lq1_ref[...] * lk1_ref[...], axis=-1, keepdims=True))
           - jnp.exp(jnp.sum(lq2_ref[...] * lk2_ref[...], axis=-1, keepdims=True)) + lam_init)
    first = lax.broadcasted_iota(jnp.int32, (1, 2 * DIFF_QK_DIM), 1) < DIFF_QK_DIM
    nsb = tq // sub
    nslots = s_s.shape[0]

    def logits(sb):
        q = q_ref[0, sb * sub:(sb + 1) * sub, :]
        zero = jnp.zeros_like(q)
        qq = jnp.concatenate([jnp.where(first, q, zero), jnp.where(first, zero, q)], axis=0)
        off = seq - sub - (qi * tq + sb * sub)
        m = jnp.full((1, 2 * sub), -jnp.inf, F32)
        for c in range(seq // kc):
            rows = slice(c * kc, (c + 1) * kc)
            s = lax.dot_general(k_ref[0, rows, :], qq, _NT, preferred_element_type=F32)
            bias = band_ref[0, pl.ds(pl.multiple_of(off + c * kc, 128), kc), :]
            s = s + jnp.concatenate([bias, bias], axis=1)
            s_s[sb % nslots, rows, :] = s
            m = jnp.maximum(m, jnp.max(s, axis=0, keepdims=True))
        return m

    def softmax_pv(sb, m):
        for c in range(seq // kc):
            rows = slice(c * kc, (c + 1) * kc)
            e_s[sb % nslots, rows, :] = jnp.exp2(s_s[sb % nslots, rows, :] - m).astype(BF16)
        ot = _dot(v_ref[0, 0], e_s[sb % nslots])
        r1 = 1.0 / ot[DIFF_V_DIM:DIFF_V_DIM + 1, 0:sub]
        r2 = lam / ot[DIFF_V_DIM:DIFF_V_DIM + 1, sub:2 * sub]
        o = ot[0:DIFF_V_DIM, 0:sub] * r1 - ot[0:DIFF_V_DIM, sub:2 * sub] * r2
        y = o * lax.rsqrt(jnp.mean(o * o, axis=0, keepdims=True) + RMS_EPS) * nw_ref[...] * (1.0 - lam_init)
        o_ref[0, sb * sub:(sb + 1) * sub, :] = y.T.astype(o_ref.dtype)

    m_next = logits(0)
    for sb in range(nsb):
        m = m_next
        if sb + 1 < nsb:
            m_next = logits(sb + 1)
        softmax_pv(sb, m)


def _diff_attention(qd, kd, vd, band, lq1, lk1, lq2, lk2, norm_w, *, lam_init, tq, sub):
    b, seq, _ = qd.shape
    nq = seq // tq
    vec = lambda n: pl.BlockSpec((1, n), lambda h, bi, qi: (0, 0))
    return pl.pallas_call(
        functools.partial(_attn_kernel, lam_init=lam_init, tq=tq, sub=sub, seq=seq, kc=512),
        out_shape=jax.ShapeDtypeStruct((b, seq, DIFF_WIDTH), BF16),
        grid=(DIFF_HEADS, b, nq),
        in_specs=[vec(DIFF_QK_DIM), vec(DIFF_QK_DIM), vec(DIFF_QK_DIM), vec(DIFF_QK_DIM),
                  pl.BlockSpec((DIFF_V_DIM, 1), lambda h, bi, qi: (0, 0)),
                  pl.BlockSpec((1, tq, 128), lambda h, bi, qi: (bi, qi, h)),
                  pl.BlockSpec((1, seq, 128), lambda h, bi, qi: (bi, 0, h)),
                  pl.BlockSpec((1, 1, VT_ROWS, seq), lambda h, bi, qi: (bi, h, 0, 0)),
                  pl.BlockSpec((1, 2 * seq - sub, sub), lambda h, bi, qi: (h, 0, 0))],
        out_specs=pl.BlockSpec((1, tq, 128), lambda h, bi, qi: (bi, qi, h)),
        scratch_shapes=[pltpu.VMEM((min(tq // sub, 3), seq, 2 * sub), F32),
                        pltpu.VMEM((min(tq // sub, 3), seq, 2 * sub), BF16)],
        compiler_params=pltpu.CompilerParams(
            dimension_semantics=("arbitrary", "arbitrary", "arbitrary"),
            vmem_limit_bytes=V7X_VMEM_LIMIT_BYTES),
        name="diff_attn",
    )(lq1.reshape(1, -1), lk1.reshape(1, -1), lq2.reshape(1, -1), lk2.reshape(1, -1),
      norm_w.reshape(-1, 1), qd, kd, vd, band)


def _chunk_scan(x, row, *, reverse):
    n = x.shape[0]
    d = 1
    while d < GLA_CHUNK:
        if reverse:
            x = x + jnp.where(row < GLA_CHUNK - d, pltpu.roll(x, n - d, 0), 0.0)
        else:
            x = x + jnp.where(row >= d, pltpu.roll(x, d, 0), 0.0)
        d *= 2
    return x


def _gla_kernel(gd_ref, up_ref, gb_ref, q_ref, k_ref, v_ref, r_ref, nw_ref, o_ref,
                qq_s, kk_s, bf_s, bb_s, kv_s, st_s, *, seq):
    c = GLA_CHUNK
    blk = 2 * c
    nb = seq // blk

    row = lax.broadcasted_iota(jnp.int32, (blk, 1), 0) & (c - 1)
    first_half = lax.broadcasted_iota(jnp.int32, (blk, 1), 0) < c

    def gates(hp, n):
        rows = slice(n * blk, (n + 1) * blk)
        z = _dot(gd_ref[0, rows, :].astype(BF16), up_ref[hp, 0]) + gb_ref[hp, 0]
        g = (jnp.minimum(z, 0.0) - jnp.log(1.0 + jnp.exp(-jnp.abs(z)))) * (1.0 / GLA_GATE_TAU)
        pf = _chunk_scan(g[:, :128], row, reverse=False)
        ub = _chunk_scan(g[:, 128:], row, reverse=True)
        bf_s[hp, rows, :] = pf
        bb_s[hp, rows, :] = ub
        bfn = pf - jnp.where(first_half, pf[c - 1:c, :], 0.0)
        bbn = ub - jnp.where(first_half, 0.0, ub[c:c + 1, :])
        q = q_ref[0, rows, hp * 128:(hp + 1) * 128]
        k = k_ref[0, rows, hp * 128:(hp + 1) * 128]
        qq_s[hp, rows, 0:128] = (q * jnp.exp(bfn)).astype(BF16)
        kk_s[hp, rows, 0:128] = (k * jnp.exp(-bfn)).astype(BF16)
        qq_s[hp, rows, 128:256] = (q * jnp.exp(bbn)).astype(BF16)
        kk_s[hp, rows, 128:256] = (k * jnp.exp(-bbn)).astype(BF16)

    def kv_scan(hp):
        for n in range(nb):
            rows = slice(n * blk, (n + 1) * blk)
            kv_s[hp, n] = lax.dot_general(v_ref[0, rows, hp * 256:(hp + 1) * 256], kk_s[hp, rows, :], _TN,
                                          preferred_element_type=F32)
        sr = lax.broadcasted_iota(jnp.int32, (256, 128), 0) < GLA_V_DIM
        sc = lax.broadcasted_iota(jnp.int32, (256, 128), 1) < GLA_K_DIM
        same_head = sr == sc
        sf = jnp.zeros((256, 128), F32)
        sb = jnp.zeros((256, 128), F32)
        for i in range(nb):
            t1 = bf_s[hp, i * blk + c - 1:i * blk + c, :]
            p2 = bf_s[hp, i * blk + blk - 1:i * blk + blk, :]
            st_s[hp, i, :, 0:128] = (sf * jnp.exp(t1)).astype(BF16)
            sf = jnp.exp(t1 + p2) * sf + jnp.exp(p2) * jnp.where(same_head, kv_s[hp, i, :, 0:128], 0.0)
            n = nb - 1 - i
            t2 = bb_s[hp, n * blk + c:n * blk + c + 1, :]
            u1 = bb_s[hp, n * blk:n * blk + 1, :]
            st_s[hp, n, :, 128:256] = (sb * jnp.exp(t2)).astype(BF16)
            sb = jnp.exp(u1 + t2) * sb + jnp.exp(u1) * jnp.where(same_head, kv_s[hp, n, :, 128:256], 0.0)

    lane = lax.broadcasted_iota(jnp.int32, (1, 256), 1)
    head0_v = lane < GLA_V_DIM
    quarter = [(lane >= i * GLA_K_DIM) & (lane < (i + 1) * GLA_K_DIM) for i in range(4)]
    ci = lax.broadcasted_iota(jnp.int32, (blk, 2 * blk), 0)
    si = lax.broadcasted_iota(jnp.int32, (blk, 2 * blk), 1) & (blk - 1)
    causal = si <= ci
    anti = si > ci

    def scores_of(hp, n):
        rows = slice(n * blk, (n + 1) * blk)
        kn = kk_s[hp, rows, :]
        zk = jnp.zeros_like(kn)
        kbd = jnp.concatenate([jnp.where(m, kn, zk) for m in quarter], axis=0)
        return lax.dot_general(qq_s[hp, rows, :], kbd, _NT, preferred_element_type=F32)

    def outputs(hp, n, scores):
        rows = slice(n * blk, (n + 1) * blk)
        qn = qq_s[hp, rows, :]
        vn = v_ref[0, rows, hp * 256:(hp + 1) * 256]
        prob = (jnp.where(causal, scores[:, 0:2 * blk], 0.0)
                + jnp.where(anti, scores[:, 2 * blk:4 * blk], 0.0)).astype(BF16)
        zv = jnp.zeros_like(vn)
        vbd = jnp.concatenate([jnp.where(head0_v, vn, zv), jnp.where(head0_v, zv, vn)], axis=0)
        o = _dot(prob, vbd) + lax.dot_general(qn, st_s[hp, n], _NT, preferred_element_type=F32)
        nw = nw_ref[...]
        for hh in range(2):
            cols = slice(hp * 256 + hh * GLA_V_DIM, hp * 256 + (hh + 1) * GLA_V_DIM)
            oh = o[:, hh * GLA_V_DIM:(hh + 1) * GLA_V_DIM]
            y = oh * lax.rsqrt(jnp.mean(oh * oh, axis=-1, keepdims=True) + RMS_EPS) * nw
            gate = r_ref[0, rows, cols]
            o_ref[0, rows, cols] = (y * (gate * jax.nn.sigmoid(gate))).astype(o_ref.dtype)

    for n in range(nb):
        gates(0, n)
    kv_scan(0)
    for hp in range(2):
        sc_next = scores_of(hp, 0)
        for n in range(nb):
            sc = sc_next
            if n + 1 < nb:
                sc_next = scores_of(hp, n + 1)
            outputs(hp, n, sc)
            if hp == 0:
                gates(1, n)
        if hp == 0:
            kv_scan(1)


def _gla(gd, up_bd, gb_bd, gq, gk, gv, gr, norm_w):
    b, seq, _ = gq.shape
    hp = GLA_HEADS // 2
    return pl.pallas_call(
        functools.partial(_gla_kernel, seq=seq),
        out_shape=jax.ShapeDtypeStruct((b, seq, GLA_WIDTH), BF16),
        grid=(b,),
        in_specs=[pl.BlockSpec((1, seq, 2 * GLA_GATE_RANK), lambda bi: (bi, 0, 0)),
                  pl.BlockSpec((hp, 1, 2 * GLA_GATE_RANK, 256), lambda bi: (0, 0, 0, 0)),
                  pl.BlockSpec((hp, 1, 1, 256), lambda bi: (0, 0, 0, 0)),
                  pl.BlockSpec((1, seq, GLA_KEY_WIDTH), lambda bi: (bi, 0, 0)),
                  pl.BlockSpec((1, seq, GLA_KEY_WIDTH), lambda bi: (bi, 0, 0)),
                  pl.BlockSpec((1, seq, GLA_WIDTH), lambda bi: (bi, 0, 0)),
                  pl.BlockSpec((1, seq, GLA_WIDTH), lambda bi: (bi, 0, 0)),
                  pl.BlockSpec((1, GLA_V_DIM), lambda bi: (0, 0))],
        out_specs=pl.BlockSpec((1, seq, GLA_WIDTH), lambda bi: (bi, 0, 0)),
        scratch_shapes=[pltpu.VMEM((hp, seq, 256), BF16), pltpu.VMEM((hp, seq, 256), BF16),
                        pltpu.VMEM((hp, seq, 128), F32), pltpu.VMEM((hp, seq, 128), F32),
                        pltpu.VMEM((hp, seq // (2 * GLA_CHUNK), 256, 256), F32),
                        pltpu.VMEM((hp, seq // (2 * GLA_CHUNK), 256, 256), BF16)],
        compiler_params=pltpu.CompilerParams(
            dimension_semantics=("arbitrary",),
            vmem_limit_bytes=V7X_VMEM_LIMIT_BYTES),
        name="gla",
    )(gd, up_bd, gb_bd, gq, gk, gv, gr, norm_w.reshape(1, -1))


def _mix_mlp_kernel(d_ref, g_ref, x_ref, eg_ref, eb_ref, wo_ref, l1g_ref, l1b_ref, w1_ref, b1_ref, w2_ref, b2_ref,
                    l2g_ref, l2b_ref, o_ref, *, ff_chunk, emb_ln):
    tm = x_ref.shape[0]
    halves = (slice(0, tm // 2), slice(tm // 2, tm))
    x1s = []
    for rows in halves:
        mix = (_dot(d_ref[rows, :], wo_ref[0, 0:DIFF_WIDTH, :])
               + _dot(g_ref[rows, :], wo_ref[0, DIFF_WIDTH:D_MODEL, :]))
        x = x_ref[rows, :]
        if emb_ln:
            x = _layer_norm(x, eg_ref[...], eb_ref[...])
        x1s.append(_layer_norm(ALPHA * x + mix, l1g_ref[0], l1b_ref[0]))
    for rows, x1 in zip(halves, x1s):
        xb = x1.astype(BF16)
        acc = jnp.zeros(x1.shape, F32)
        for c0 in range(0, D_FF, ff_chunk):
            h = _dot(xb, w1_ref[0, :, c0:c0 + ff_chunk]) + b1_ref[0, :, c0:c0 + ff_chunk]
            h = jnp.square(jnp.maximum(h, 0.0)).astype(BF16)
            acc = acc + _dot(h, w2_ref[0, c0:c0 + ff_chunk, :])
        o_ref[rows, :] = _layer_norm(ALPHA * x1 + (acc + b2_ref[0]), l2g_ref[0], l2b_ref[0])


def _mix_mlp(li, d2d, g2d, x2d, emb_g, emb_b, w_o, ln1_g, ln1_b, w1, b1, w2, b2, ln2_g, ln2_b,
             *, emb_ln, tm=512, ff_chunk=1024):
    t = x2d.shape[0]
    row = lambda n: pl.BlockSpec((tm, n), lambda i: (i, 0))
    const = lambda shape: pl.BlockSpec(shape, lambda i: (0,) * len(shape))
    layer = lambda a: pl.BlockSpec((1,) + a.shape[1:], lambda i: (li,) + (0,) * (a.ndim - 1),
                                   pipeline_mode=pl.Buffered(1))
    vecs = [v.reshape(DEPTH, 1, -1) for v in (ln1_g, ln1_b, b1, b2, ln2_g, ln2_b)]
    l1g, l1b, b1r, b2r, l2g, l2b = vecs
    args = (d2d, g2d, x2d, emb_g.reshape(1, -1), emb_b.reshape(1, -1),
            w_o, l1g, l1b, w1, b1r, w2, b2r, l2g, l2b)
    in_specs = ([row(DIFF_WIDTH), row(GLA_WIDTH), row(D_MODEL), const((1, D_MODEL)), const((1, D_MODEL))]
                + [layer(a) for a in args[5:]])
    return pl.pallas_call(
        functools.partial(_mix_mlp_kernel, ff_chunk=ff_chunk, emb_ln=emb_ln),
        out_shape=jax.ShapeDtypeStruct((t, D_MODEL), F32),
        grid=(t // tm,),
        in_specs=in_specs,
        out_specs=row(D_MODEL),
        compiler_params=pltpu.CompilerParams(
            dimension_semantics=("arbitrary",), vmem_limit_bytes=V7X_VMEM_LIMIT_BYTES),
        name="mix_mlp",
    )(*args)


def _gate_weights(gate_up, gate_bias):
    hp = GLA_HEADS // 2
    up = gate_up.reshape(2, GLA_GATE_RANK, hp, 128)
    z = jnp.zeros((GLA_GATE_RANK, hp, 128), gate_up.dtype)
    top = jnp.concatenate([up[0], z], axis=-1)
    bot = jnp.concatenate([z, up[1]], axis=-1)
    up_bd = jnp.concatenate([top, bot], axis=0)
    up_bd = up_bd.transpose(1, 0, 2)[:, None].astype(BF16)
    gb = gate_bias.reshape(2, hp, 128)
    gb_bd = jnp.concatenate([gb[0], gb[1]], axis=-1)[:, None, None]
    return up_bd, gb_bd


def kernel(x, ln_emb_g, ln_emb_b, rel_bias_table, w_in, lambda_q1, lambda_k1, lambda_q2, lambda_k2,
           diff_norm_w, gla_gate_up, gla_gate_bias, gla_norm_w, w_o, ln1_g, ln1_b,
           w_ffn1, b_ffn1, w_ffn2, b_ffn2, ln2_g, ln2_b):
    b, seq, _ = x.shape
    t = b * seq
    tq, sub = 2048, 256
    band = _bias_band(rel_bias_table, sub, seq)
    w_o_b = w_o.astype(BF16)
    w1_b = w_ffn1.astype(BF16)
    w2_b = w_ffn2.astype(BF16)

    h = x.reshape(t, D_MODEL)
    for li in range(DEPTH):
        emb = (ln_emb_g, ln_emb_b) if li == 0 else ()
        qd, kd, vd, gq, gk, gv, gr, gd = _inproj(li, h, w_in, seq, *emb)
        r3 = lambda a: a.reshape(b, seq, a.shape[-1])
        lam_init = 0.8 - 0.6 * math.exp(-0.3 * li)
        d_out = _diff_attention(r3(qd), r3(kd), vd, band, lambda_q1[li], lambda_k1[li],
                                lambda_q2[li], lambda_k2[li], diff_norm_w[li],
                                lam_init=lam_init, tq=tq, sub=sub)
        up_bd, gb_bd = _gate_weights(gla_gate_up[li], gla_gate_bias[li])
        g_out = _gla(r3(gd), up_bd, gb_bd, r3(gq), r3(gk), r3(gv), r3(gr), gla_norm_w[li])
        h = _mix_mlp(li, d_out.reshape(t, DIFF_WIDTH), g_out.reshape(t, GLA_WIDTH), h, ln_emb_g, ln_emb_b,
                     w_o_b, ln1_g, ln1_b, w1_b, b_ffn1, w2_b, b_ffn2, ln2_g, ln2_b, emb_ln=(li == 0))
    return h.reshape(b, seq, D_MODEL)
```

```python
import functools
import math

import jax
import jax.numpy as jnp
from jax import lax
from jax.experimental import pallas as pl
from jax.experimental.pallas import tpu as pltpu

D_MODEL = 1024
DEPTH = 2
DIFF_HEADS = 4
DIFF_QK_DIM = 64
DIFF_V_DIM = 128
DIFF_WIDTH = 512
GLA_HEADS = 4
GLA_WIDTH = 512
GLA_V_DIM = 128
GLA_K_DIM = 64
GLA_KEY_WIDTH = 256
GLA_GATE_RANK = 16
GLA_GATE_TAU = 16.0
GLA_CHUNK = 64
D_FF = 4096
N_BUCKETS = 32
LN_EPS = 1e-5
RMS_EPS = 1e-5
ALPHA = (2.0 * DEPTH) ** 0.25
LOG2E = math.log2(math.e)
VT_ROWS = DIFF_V_DIM + 16
D_IN = 3104

V7X_VMEM_LIMIT_BYTES = 56 * 1024 * 1024

BF16 = jnp.bfloat16
F32 = jnp.float32

_NT = (((1,), (1,)), ((), ()))
_TN = (((0,), (0,)), ((), ()))


def _dot(a, b):
    return jnp.dot(a, b, preferred_element_type=F32)


def _layer_norm(y, g, b):
    mu = jnp.mean(y, axis=-1, keepdims=True)
    d = y - mu
    var = jnp.mean(d * d, axis=-1, keepdims=True)
    return d * lax.rsqrt(var + LN_EPS) * g + b


def _band_kernel(table_ref, band_ref, *, tq, seq):
    h = pl.program_id(0)
    width = 2 * seq - tq
    lo, hi = seq - tq - 128, seq + 128
    near = hi - lo
    r = lax.broadcasted_iota(jnp.int32, (near, tq), 1)
    m = lax.broadcasted_iota(jnp.int32, (near, tq), 0) + lo
    rel = m - (seq - tq) - r
    n = jnp.abs(rel)
    n2 = n * n
    large = jnp.full_like(n, 8)
    for p in range(7, 14):
        large = large + jnp.where(n2 >= (1 << p), 1, 0)
    bucket = jnp.where(n < 8, n, large) + jnp.where(rel > 0, 16, 0)
    acc = jnp.zeros((near, tq), F32)
    for i in range(N_BUCKETS):
        acc = jnp.where(bucket == i, table_ref[i * DIFF_HEADS + h], acc)
    half = N_BUCKETS // 2
    band_ref[0, 0:lo, :] = jnp.full((lo, tq), table_ref[(half - 1) * DIFF_HEADS + h] * LOG2E, F32)
    band_ref[0, lo:hi, :] = acc * LOG2E
    band_ref[0, hi:width, :] = jnp.full((width - hi, tq), table_ref[(N_BUCKETS - 1) * DIFF_HEADS + h] * LOG2E, F32)


def _bias_band(table, tq, seq):
    width = 2 * seq - tq
    return pl.pallas_call(
        functools.partial(_band_kernel, tq=tq, seq=seq),
        out_shape=jax.ShapeDtypeStruct((DIFF_HEADS, width, tq), F32),
        grid=(DIFF_HEADS,),
        in_specs=[pl.BlockSpec(memory_space=pltpu.SMEM)],
        out_specs=pl.BlockSpec((1, width, tq), lambda h: (h, 0, 0)),
        name="bias_band",
    )(table.reshape(-1))


def _inproj_kernel(*refs, apply_ln):
    if apply_ln:
        (x_ref, g_ref, b_ref, w_ref,
         qd_ref, kd_ref, vd_ref, gq_ref, gk_ref, gv_ref, gr_ref, gd_ref) = refs
        xn = _layer_norm(x_ref[...], g_ref[...], b_ref[...])
    else:
        (x_ref, w_ref,
         qd_ref, kd_ref, vd_ref, gq_ref, gk_ref, gv_ref, gr_ref, gd_ref) = refs
        xn = x_ref[...]
    xb = xn.astype(BF16)
    qd_ref[...] = (_dot(xb, w_ref[0, :,0:512]) * (DIFF_QK_DIM ** -0.5 * LOG2E)).astype(BF16)
    kd_ref[...] = _dot(xb, w_ref[0, :,512:1024]).astype(BF16)
    tm = xb.shape[0]
    pad_rows = VT_ROWS - DIFF_V_DIM
    ones_row = (lax.broadcasted_iota(jnp.int32, (pad_rows, tm), 0) == 0).astype(BF16)
    v = _dot(xb, w_ref[0, :,1024:1536])
    for hd in range(DIFF_HEADS):
        vd_ref[0, hd, 0:DIFF_V_DIM, :] = v[:, hd * DIFF_V_DIM:(hd + 1) * DIFF_V_DIM].T.astype(BF16)
        vd_ref[0, hd, DIFF_V_DIM:VT_ROWS, :] = ones_row
    gq_ref[...] = _dot(xb, w_ref[0, :,1536:1792]) * (GLA_K_DIM ** -0.5)
    gk_ref[...] = _dot(xb, w_ref[0, :,1792:2048])
    gv_ref[...] = _dot(xb, w_ref[0, :,2048:2560]).astype(BF16)
    gr_ref[...] = _dot(xb, w_ref[0, :,2560:3072])
    gd_ref[...] = _dot(xb, w_ref[0, :,3072:D_IN])


def _inproj(li, x2d, w_pad, seq, ln_g=None, ln_b=None, *, tm=512):
    t = x2d.shape[0]
    apply_ln = ln_g is not None
    row = lambda n: pl.BlockSpec((tm, n), lambda i: (i, 0))
    const = lambda shape: pl.BlockSpec(shape, lambda i: (0,) * len(shape))
    in_specs = [row(D_MODEL)]
    args = [x2d]
    if apply_ln:
        in_specs += [const((1, D_MODEL)), const((1, D_MODEL))]
        args += [ln_g.reshape(1, -1), ln_b.reshape(1, -1)]
    in_specs.append(pl.BlockSpec((1, D_MODEL, D_IN), lambda i: (li, 0, 0), pipeline_mode=pl.Buffered(1)))
    args.append(w_pad)
    widths = [(512, BF16), (512, BF16), None, (256, F32), (256, F32),
              (512, BF16), (512, F32), (2 * GLA_GATE_RANK, F32)]
    out_shape = [jax.ShapeDtypeStruct((t, w[0]), w[1]) if w else None for w in widths]
    out_specs = [row(w[0]) if w else None for w in widths]
    tiles = seq // tm
    out_shape[2] = jax.ShapeDtypeStruct((t // seq, DIFF_HEADS, VT_ROWS, seq), BF16)
    out_specs[2] = pl.BlockSpec((1, DIFF_HEADS, VT_ROWS, tm), lambda i: (i // tiles, 0, 0, i % tiles))
    return pl.pallas_call(
        functools.partial(_inproj_kernel, apply_ln=apply_ln),
        out_shape=out_shape,
        grid=(t // tm,),
        in_specs=in_specs,
        out_specs=out_specs,
        compiler_params=pltpu.CompilerParams(
            dimension_semantics=("arbitrary",), vmem_limit_bytes=V7X_VMEM_LIMIT_BYTES),
        name="ln_inproj" if apply_ln else "inproj",
    )(*args)


def _attn_kernel(lq1_ref, lk1_ref, lq2_ref, lk2_ref, nw_ref, q_ref, k_ref, v_ref, band_ref,
                 o_ref, s_s, e_s, *, lam_init, tq, sub, seq, kc):
    qi = pl.program_id(2)
    lam = (jnp.exp(jnp.sum(lq1_ref[...] * lk1_ref[...], axis=-1, keepdims=True))
           - jnp.exp(jnp.sum(lq2_ref[...] * lk2_ref[...], axis=-1, keepdims=True)) + lam_init)
    first = lax.broadcasted_iota(jnp.int32, (1, 2 * DIFF_QK_DIM), 1) < DIFF_QK_DIM
    nsb = tq // sub
    nslots = s_s.shape[0]

    def logits(sb):
        q = q_ref[0, sb * sub:(sb + 1) * sub, :]
        zero = jnp.zeros_like(q)
        qq = jnp.concatenate([jnp.where(first, q, zero), jnp.where(first, zero, q)], axis=0)
        off = seq - sub - (qi * tq + sb * sub)
        m = jnp.full((1, 2 * sub), -jnp.inf, F32)
        for c in range(seq // kc):
            rows = slice(c * kc, (c + 1) * kc)
            s = lax.dot_general(k_ref[0, rows, :], qq, _NT, preferred_element_type=F32)
            bias = band_ref[0, pl.ds(pl.multiple_of(off + c * kc, 128), kc), :]
            s = s + jnp.concatenate([bias, bias], axis=1)
            s_s[sb % nslots, rows, :] = s
            m = jnp.maximum(m, jnp.max(s, axis=0, keepdims=True))
        return m

    def softmax_pv(sb, m):
        for c in range(seq // kc):
            rows = slice(c * kc, (c + 1) * kc)
            e_s[sb % nslots, rows, :] = jnp.exp2((s_s[sb % nslots, rows, :] - m).astype(BF16))
        ot = _dot(v_ref[0, 0], e_s[sb % nslots])
        r1 = 1.0 / ot[DIFF_V_DIM:DIFF_V_DIM + 1, 0:sub]
        r2 = lam / ot[DIFF_V_DIM:DIFF_V_DIM + 1, sub:2 * sub]
        o = ot[0:DIFF_V_DIM, 0:sub] * r1 - ot[0:DIFF_V_DIM, sub:2 * sub] * r2
        y = o * lax.rsqrt(jnp.mean(o * o, axis=0, keepdims=True) + RMS_EPS) * nw_ref[...] * (1.0 - lam_init)
        o_ref[0, sb * sub:(sb + 1) * sub, :] = y.T.astype(o_ref.dtype)

    m_next = logits(0)
    for sb in range(nsb):
        m = m_next
        if sb + 1 < nsb:
            m_next = logits(sb + 1)
        softmax_pv(sb, m)


def _diff_attention(qd, kd, vd, band, lq1, lk1, lq2, lk2, norm_w, *, lam_init, tq, sub):
    b, seq, _ = qd.shape
    nq = seq // tq
    vec = lambda n: pl.BlockSpec((1, n), lambda h, bi, qi: (0, 0))
    return pl.pallas_call(
        functools.partial(_attn_kernel, lam_init=lam_init, tq=tq, sub=sub, seq=seq, kc=512),
        out_shape=jax.ShapeDtypeStruct((b, seq, DIFF_WIDTH), BF16),
        grid=(DIFF_HEADS, b, nq),
        in_specs=[vec(DIFF_QK_DIM), vec(DIFF_QK_DIM), vec(DIFF_QK_DIM), vec(DIFF_QK_DIM),
                  pl.BlockSpec((DIFF_V_DIM, 1), lambda h, bi, qi: (0, 0)),
                  pl.BlockSpec((1, tq, 128), lambda h, bi, qi: (bi, qi, h)),
                  pl.BlockSpec((1, seq, 128), lambda h, bi, qi: (bi, 0, h)),
                  pl.BlockSpec((1, 1, VT_ROWS, seq), lambda h, bi, qi: (bi, h, 0, 0)),
                  pl.BlockSpec((1, 2 * seq - sub, sub), lambda h, bi, qi: (h, 0, 0))],
        out_specs=pl.BlockSpec((1, tq, 128), lambda h, bi, qi: (bi, qi, h)),
        scratch_shapes=[pltpu.VMEM((min(tq // sub, 3), seq, 2 * sub), F32),
                        pltpu.VMEM((min(tq // sub, 3), seq, 2 * sub), BF16)],
        compiler_params=pltpu.CompilerParams(
            dimension_semantics=("arbitrary", "arbitrary", "arbitrary"),
            vmem_limit_bytes=V7X_VMEM_LIMIT_BYTES),
        name="diff_attn",
    )(lq1.reshape(1, -1), lk1.reshape(1, -1), lq2.reshape(1, -1), lk2.reshape(1, -1),
      norm_w.reshape(-1, 1), qd, kd, vd, band)


def _chunk_scan(x, row, *, reverse):
    n = x.shape[0]
    d = 1
    while d < GLA_CHUNK:
        if reverse:
            x = x + jnp.where(row < GLA_CHUNK - d, pltpu.roll(x, n - d, 0), 0.0)
        else:
            x = x + jnp.where(row >= d, pltpu.roll(x, d, 0), 0.0)
        d *= 2
    return x


def _gla_kernel(gd_ref, up_ref, gb_ref, q_ref, k_ref, v_ref, r_ref, nw_ref, o_ref,
                qq_s, kk_s, bf_s, bb_s, kv_s, st_s, *, seq):
    c = GLA_CHUNK
    blk = 2 * c
    nb = seq // blk

    row = lax.broadcasted_iota(jnp.int32, (blk, 1), 0) & (c - 1)
    first_half = lax.broadcasted_iota(jnp.int32, (blk, 1), 0) < c

    def gates(hp, n):
        rows = slice(n * blk, (n + 1) * blk)
        z = _dot(gd_ref[0, rows, :].astype(BF16), up_ref[hp, 0]) + gb_ref[hp, 0]
        g = (jnp.minimum(z, 0.0) - jnp.log(1.0 + jnp.exp(-jnp.abs(z)))) * (1.0 / GLA_GATE_TAU)
        pf = _chunk_scan(g[:, :128], row, reverse=False)
        ub = _chunk_scan(g[:, 128:], row, reverse=True)
        bf_s[hp, rows, :] = pf
        bb_s[hp, rows, :] = ub
        bfn = pf - jnp.where(first_half, pf[c - 1:c, :], 0.0)
        bbn = ub - jnp.where(first_half, 0.0, ub[c:c + 1, :])
        q = q_ref[0, rows, hp * 128:(hp + 1) * 128]
        k = k_ref[0, rows, hp * 128:(hp + 1) * 128]
        qq_s[hp, rows, 0:128] = (q * jnp.exp(bfn)).astype(BF16)
        kk_s[hp, rows, 0:128] = (k * jnp.exp(-bfn)).astype(BF16)
        qq_s[hp, rows, 128:256] = (q * jnp.exp(bbn)).astype(BF16)
        kk_s[hp, rows, 128:256] = (k * jnp.exp(-bbn)).astype(BF16)

    def kv_scan(hp):
        for n in range(nb):
            rows = slice(n * blk, (n + 1) * blk)
            kv_s[hp, n] = lax.dot_general(v_ref[0, rows, hp * 256:(hp + 1) * 256], kk_s[hp, rows, :], _TN,
                                          preferred_element_type=F32)
        sr = lax.broadcasted_iota(jnp.int32, (256, 128), 0) < GLA_V_DIM
        sc = lax.broadcasted_iota(jnp.int32, (256, 128), 1) < GLA_K_DIM
        same_head = sr == sc
        sf = jnp.zeros((256, 128), F32)
        sb = jnp.zeros((256, 128), F32)
        for i in range(nb):
            t1 = bf_s[hp, i * blk + c - 1:i * blk + c, :]
            p2 = bf_s[hp, i * blk + blk - 1:i * blk + blk, :]
            st_s[hp, i, :, 0:128] = (sf * jnp.exp(t1)).astype(BF16)
            sf = jnp.exp(t1 + p2) * sf + jnp.exp(p2) * jnp.where(same_head, kv_s[hp, i, :, 0:128], 0.0)
            n = nb - 1 - i
            t2 = bb_s[hp, n * blk + c:n * blk + c + 1, :]
            u1 = bb_s[hp, n * blk:n * blk + 1, :]
            st_s[hp, n, :, 128:256] = (sb * jnp.exp(t2)).astype(BF16)
            sb = jnp.exp(u1 + t2) * sb + jnp.exp(u1) * jnp.where(same_head, kv_s[hp, n, :, 128:256], 0.0)

    lane = lax.broadcasted_iota(jnp.int32, (1, 256), 1)
    head0_v = lane < GLA_V_DIM
    quarter = [(lane >= i * GLA_K_DIM) & (lane < (i + 1) * GLA_K_DIM) for i in range(4)]
    ci = lax.broadcasted_iota(jnp.int32, (blk, 2 * blk), 0)
    si = lax.broadcasted_iota(jnp.int32, (blk, 2 * blk), 1) & (blk - 1)
    causal = si <= ci
    anti = si > ci

    def scores_of(hp, n):
        rows = slice(n * blk, (n + 1) * blk)
        kn = kk_s[hp, rows, :]
        zk = jnp.zeros_like(kn)
        kbd = jnp.concatenate([jnp.where(m, kn, zk) for m in quarter], axis=0)
        return lax.dot_general(qq_s[hp, rows, :], kbd, _NT, preferred_element_type=F32)

    def outputs(hp, n, scores):
        rows = slice(n * blk, (n + 1) * blk)
        qn = qq_s[hp, rows, :]
        vn = v_ref[0, rows, hp * 256:(hp + 1) * 256]
        prob = (jnp.where(causal, scores[:, 0:2 * blk], 0.0)
                + jnp.where(anti, scores[:, 2 * blk:4 * blk], 0.0)).astype(BF16)
        zv = jnp.zeros_like(vn)
        vbd = jnp.concatenate([jnp.where(head0_v, vn, zv), jnp.where(head0_v, zv, vn)], axis=0)
        o = _dot(prob, vbd) + lax.dot_general(qn, st_s[hp, n], _NT, preferred_element_type=F32)
        nw = nw_ref[...]
        for hh in range(2):
            cols = slice(hp * 256 + hh * GLA_V_DIM, hp * 256 + (hh + 1) * GLA_V_DIM)
            oh = o[:, hh * GLA_V_DIM:(hh + 1) * GLA_V_DIM]
            y = oh * lax.rsqrt(jnp.mean(oh * oh, axis=-1, keepdims=True) + RMS_EPS) * nw
            gate = r_ref[0, rows, cols]
            o_ref[0, rows, cols] = (y * (gate * jax.nn.sigmoid(gate))).astype(o_ref.dtype)

    for n in range(nb):
        gates(0, n)
    kv_scan(0)
    for hp in range(2):
        sc_next = scores_of(hp, 0)
        for n in range(nb):
            sc = sc_next
            if n + 1 < nb:
                sc_next = scores_of(hp, n + 1)
            outputs(hp, n, sc)
            if hp == 0:
                gates(1, n)
        if hp == 0:
            kv_scan(1)


def _gla(gd, up_bd, gb_bd, gq, gk, gv, gr, norm_w):
    b, seq, _ = gq.shape
    hp = GLA_HEADS // 2
    return pl.pallas_call(
        functools.partial(_gla_kernel, seq=seq),
        out_shape=jax.ShapeDtypeStruct((b, seq, GLA_WIDTH), BF16),
        grid=(b,),
        in_specs=[pl.BlockSpec((1, seq, 2 * GLA_GATE_RANK), lambda bi: (bi, 0, 0)),
                  pl.BlockSpec((hp, 1, 2 * GLA_GATE_RANK, 256), lambda bi: (0, 0, 0, 0)),
                  pl.BlockSpec((hp, 1, 1, 256), lambda bi: (0, 0, 0, 0)),
                  pl.BlockSpec((1, seq, GLA_KEY_WIDTH), lambda bi: (bi, 0, 0)),
                  pl.BlockSpec((1, seq, GLA_KEY_WIDTH), lambda bi: (bi, 0, 0)),
                  pl.BlockSpec((1, seq, GLA_WIDTH), lambda bi: (bi, 0, 0)),
                  pl.BlockSpec((1, seq, GLA_WIDTH), lambda bi: (bi, 0, 0)),
                  pl.BlockSpec((1, GLA_V_DIM), lambda bi: (0, 0))],
        out_specs=pl.BlockSpec((1, seq, GLA_WIDTH), lambda bi: (bi, 0, 0)),
        scratch_shapes=[pltpu.VMEM((hp, seq, 256), BF16), pltpu.VMEM((hp, seq, 256), BF16),
                        pltpu.VMEM((hp, seq, 128), F32), pltpu.VMEM((hp, seq, 128), F32),
                        pltpu.VMEM((hp, seq // (2 * GLA_CHUNK), 256, 256), F32),
                        pltpu.VMEM((hp, seq // (2 * GLA_CHUNK), 256, 256), BF16)],
        compiler_params=pltpu.CompilerParams(
            dimension_semantics=("arbitrary",),
            vmem_limit_bytes=V7X_VMEM_LIMIT_BYTES),
        name="gla",
    )(gd, up_bd, gb_bd, gq, gk, gv, gr, norm_w.reshape(1, -1))


def _mix_mlp_kernel(d_ref, g_ref, x_ref, eg_ref, eb_ref, wo_ref, l1g_ref, l1b_ref, w1_ref, b1_ref, w2_ref, b2_ref,
                    l2g_ref, l2b_ref, o_ref, *, ff_chunk, emb_ln):
    tm = x_ref.shape[0]
    halves = (slice(0, tm // 2), slice(tm // 2, tm))
    x1s = []
    for rows in halves:
        mix = (_dot(d_ref[rows, :], wo_ref[0, 0:DIFF_WIDTH, :])
               + _dot(g_ref[rows, :], wo_ref[0, DIFF_WIDTH:D_MODEL, :]))
        x = x_ref[rows, :]
        if emb_ln:
            x = _layer_norm(x, eg_ref[...], eb_ref[...])
        x1s.append(_layer_norm(ALPHA * x + mix, l1g_ref[0], l1b_ref[0]))
    for rows, x1 in zip(halves, x1s):
        xb = x1.astype(BF16)
        acc = jnp.zeros(x1.shape, F32)
        for c0 in range(0, D_FF, ff_chunk):
            h = _dot(xb, w1_ref[0, :, c0:c0 + ff_chunk]) + b1_ref[0, :, c0:c0 + ff_chunk]
            h = jnp.square(jnp.maximum(h, 0.0)).astype(BF16)
            acc = acc + _dot(h, w2_ref[0, c0:c0 + ff_chunk, :])
        o_ref[rows, :] = _layer_norm(ALPHA * x1 + (acc + b2_ref[0]), l2g_ref[0], l2b_ref[0])


def _mix_mlp(li, d2d, g2d, x2d, emb_g, emb_b, w_o, ln1_g, ln1_b, w1, b1, w2, b2, ln2_g, ln2_b,
             *, emb_ln, tm=512, ff_chunk=1024):
    t = x2d.shape[0]
    row = lambda n: pl.BlockSpec((tm, n), lambda i: (i, 0))
    const = lambda shape: pl.BlockSpec(shape, lambda i: (0,) * len(shape))
    layer = lambda a: pl.BlockSpec((1,) + a.shape[1:], lambda i: (li,) + (0,) * (a.ndim - 1),
                                   pipeline_mode=pl.Buffered(1))
    vecs = [v.reshape(DEPTH, 1, -1) for v in (ln1_g, ln1_b, b1, b2, ln2_g, ln2_b)]
    l1g, l1b, b1r, b2r, l2g, l2b = vecs
    args = (d2d, g2d, x2d, emb_g.reshape(1, -1), emb_b.reshape(1, -1),
            w_o, l1g, l1b, w1, b1r, w2, b2r, l2g, l2b)
    in_specs = ([row(DIFF_WIDTH), row(GLA_WIDTH), row(D_MODEL), const((1, D_MODEL)), const((1, D_MODEL))]
                + [layer(a) for a in args[5:]])
    return pl.pallas_call(
        functools.partial(_mix_mlp_kernel, ff_chunk=ff_chunk, emb_ln=emb_ln),
        out_shape=jax.ShapeDtypeStruct((t, D_MODEL), F32),
        grid=(t // tm,),
        in_specs=in_specs,
        out_specs=row(D_MODEL),
        compiler_params=pltpu.CompilerParams(
            dimension_semantics=("arbitrary",), vmem_limit_bytes=V7X_VMEM_LIMIT_BYTES),
        name="mix_mlp",
    )(*args)


def _gate_weights(gate_up, gate_bias):
    hp = GLA_HEADS // 2
    up = gate_up.reshape(2, GLA_GATE_RANK, hp, 128)
    z = jnp.zeros((GLA_GATE_RANK, hp, 128), gate_up.dtype)
    top = jnp.concatenate([up[0], z], axis=-1)
    bot = jnp.concatenate([z, up[1]], axis=-1)
    up_bd = jnp.concatenate([top, bot], axis=0)
    up_bd = up_bd.transpose(1, 0, 2)[:, None].astype(BF16)
    gb = gate_bias.reshape(2, hp, 128)
    gb_bd = jnp.concatenate([gb[0], gb[1]], axis=-1)[:, None, None]
    return up_bd, gb_bd


def kernel(x, ln_emb_g, ln_emb_b, rel_bias_table, w_in, lambda_q1, lambda_k1, lambda_q2, lambda_k2,
           diff_norm_w, gla_gate_up, gla_gate_bias, gla_norm_w, w_o, ln1_g, ln1_b,
           w_ffn1, b_ffn1, w_ffn2, b_ffn2, ln2_g, ln2_b):
    b, seq, _ = x.shape
    t = b * seq
    tq, sub = 2048, 256
    band = _bias_band(rel_bias_table, sub, seq)
    w_in_b = w_in.astype(BF16)
    w_o_b = w_o.astype(BF16)
    w1_b = w_ffn1.astype(BF16)
    w2_b = w_ffn2.astype(BF16)

    h = x.reshape(t, D_MODEL)
    for li in range(DEPTH):
        emb = (ln_emb_g, ln_emb_b) if li == 0 else ()
        qd, kd, vd, gq, gk, gv, gr, gd = _inproj(li, h, w_in_b, seq, *emb)
        r3 = lambda a: a.reshape(b, seq, a.shape[-1])
        lam_init = 0.8 - 0.6 * math.exp(-0.3 * li)
        d_out = _diff_attention(r3(qd), r3(kd), vd, band, lambda_q1[li], lambda_k1[li],
                                lambda_q2[li], lambda_k2[li], diff_norm_w[li],
                                lam_init=lam_init, tq=tq, sub=sub)
        up_bd, gb_bd = _gate_weights(gla_gate_up[li], gla_gate_bias[li])
        g_out = _gla(r3(gd), up_bd, gb_bd, r3(gq), r3(gk), r3(gv), r3(gr), gla_norm_w[li])
        h = _mix_mlp(li, d_out.reshape(t, DIFF_WIDTH), g_out.reshape(t, GLA_WIDTH), h, ln_emb_g, ln_emb_b,
                     w_o_b, ln1_g, ln1_b, w1_b, b_ffn1, w2_b, b_ffn2, ln2_g, ln2_b, emb_ln=(li == 0))
    return h.reshape(b, seq, D_MODEL)
```

```python
import functools
import math

import jax
import jax.numpy as jnp
from jax import lax
from jax.experimental import pallas as pl
from jax.experimental.pallas import tpu as pltpu

D_MODEL = 1024
DEPTH = 2
DIFF_HEADS = 4
DIFF_QK_DIM = 64
DIFF_V_DIM = 128
DIFF_WIDTH = 512
GLA_HEADS = 4
GLA_WIDTH = 512
GLA_V_DIM = 128
GLA_K_DIM = 64
GLA_KEY_WIDTH = 256
GLA_GATE_RANK = 16
GLA_GATE_TAU = 16.0
GLA_CHUNK = 64
D_FF = 4096
N_BUCKETS = 32
LN_EPS = 1e-5
RMS_EPS = 1e-5
ALPHA = (2.0 * DEPTH) ** 0.25
LOG2E = math.log2(math.e)
VT_ROWS = DIFF_V_DIM + 16
MIX_SUB_ROWS = 256
D_IN = 3104

V7X_VMEM_LIMIT_BYTES = 56 * 1024 * 1024

BF16 = jnp.bfloat16
F32 = jnp.float32

_NT = (((1,), (1,)), ((), ()))
_TN = (((0,), (0,)), ((), ()))


def _dot(a, b):
    return jnp.dot(a, b, preferred_element_type=F32)


def _layer_norm(y, g, b):
    mu = jnp.mean(y, axis=-1, keepdims=True)
    d = y - mu
    var = jnp.mean(d * d, axis=-1, keepdims=True)
    return d * lax.rsqrt(var + LN_EPS) * g + b


def _band_kernel(table_ref, band_ref, *, tq, seq):
    h = pl.program_id(0)
    width = 2 * seq - tq
    lo, hi = seq - tq - 128, seq + 128
    near = hi - lo
    r = lax.broadcasted_iota(jnp.int32, (near, tq), 1)
    m = lax.broadcasted_iota(jnp.int32, (near, tq), 0) + lo
    rel = m - (seq - tq) - r
    n = jnp.abs(rel)
    n2 = n * n
    large = jnp.full_like(n, 8)
    for p in range(7, 14):
        large = large + jnp.where(n2 >= (1 << p), 1, 0)
    bucket = jnp.where(n < 8, n, large) + jnp.where(rel > 0, 16, 0)
    acc = jnp.zeros((near, tq), F32)
    for i in range(N_BUCKETS):
        acc = jnp.where(bucket == i, table_ref[i * DIFF_HEADS + h], acc)
    half = N_BUCKETS // 2
    band_ref[0, 0:lo, :] = jnp.full((lo, tq), table_ref[(half - 1) * DIFF_HEADS + h] * LOG2E, F32)
    band_ref[0, lo:hi, :] = acc * LOG2E
    band_ref[0, hi:width, :] = jnp.full((width - hi, tq), table_ref[(N_BUCKETS - 1) * DIFF_HEADS + h] * LOG2E, F32)


def _bias_band(table, tq, seq):
    width = 2 * seq - tq
    return pl.pallas_call(
        functools.partial(_band_kernel, tq=tq, seq=seq),
        out_shape=jax.ShapeDtypeStruct((DIFF_HEADS, width, tq), F32),
        grid=(DIFF_HEADS,),
        in_specs=[pl.BlockSpec(memory_space=pltpu.SMEM)],
        out_specs=pl.BlockSpec((1, width, tq), lambda h: (h, 0, 0)),
        name="bias_band",
    )(table.reshape(-1))


def _inproj_kernel(*refs, apply_ln):
    if apply_ln:
        (x_ref, g_ref, b_ref, w_ref,
         qd_ref, kd_ref, vd_ref, gq_ref, gk_ref, gv_ref, gr_ref, gd_ref) = refs
        xn = _layer_norm(x_ref[...], g_ref[...], b_ref[...])
    else:
        (x_ref, w_ref,
         qd_ref, kd_ref, vd_ref, gq_ref, gk_ref, gv_ref, gr_ref, gd_ref) = refs
        xn = x_ref[...]
    xb = xn.astype(BF16)
    qd_ref[...] = (_dot(xb, w_ref[0, :,0:512]) * (DIFF_QK_DIM ** -0.5 * LOG2E)).astype(BF16)
    kd_ref[...] = _dot(xb, w_ref[0, :,512:1024]).astype(BF16)
    tm = xb.shape[0]
    pad_rows = VT_ROWS - DIFF_V_DIM
    ones_row = (lax.broadcasted_iota(jnp.int32, (pad_rows, tm), 0) == 0).astype(BF16)
    v = _dot(xb, w_ref[0, :,1024:1536])
    for hd in range(DIFF_HEADS):
        vd_ref[0, hd, 0:DIFF_V_DIM, :] = v[:, hd * DIFF_V_DIM:(hd + 1) * DIFF_V_DIM].T.astype(BF16)
        vd_ref[0, hd, DIFF_V_DIM:VT_ROWS, :] = ones_row
    gq_ref[...] = _dot(xb, w_ref[0, :,1536:1792]) * (GLA_K_DIM ** -0.5)
    gk_ref[...] = _dot(xb, w_ref[0, :,1792:2048])
    gv_ref[...] = _dot(xb, w_ref[0, :,2048:2560]).astype(BF16)
    gr_ref[...] = _dot(xb, w_ref[0, :,2560:3072])
    gd_ref[...] = _dot(xb, w_ref[0, :,3072:D_IN])


def _inproj(li, x2d, w_pad, seq, ln_g=None, ln_b=None, *, tm=1024):
    t = x2d.shape[0]
    apply_ln = ln_g is not None
    row = lambda n: pl.BlockSpec((tm, n), lambda i: (i, 0))
    const = lambda shape: pl.BlockSpec(shape, lambda i: (0,) * len(shape))
    in_specs = [row(D_MODEL)]
    args = [x2d]
    if apply_ln:
        in_specs += [const((1, D_MODEL)), const((1, D_MODEL))]
        args += [ln_g.reshape(1, -1), ln_b.reshape(1, -1)]
    in_specs.append(pl.BlockSpec((1, D_MODEL, D_IN), lambda i: (li, 0, 0), pipeline_mode=pl.Buffered(1)))
    args.append(w_pad)
    widths = [(512, BF16), (512, BF16), None, (256, F32), (256, F32),
              (512, BF16), (512, F32), (2 * GLA_GATE_RANK, F32)]
    out_shape = [jax.ShapeDtypeStruct((t, w[0]), w[1]) if w else None for w in widths]
    out_specs = [row(w[0]) if w else None for w in widths]
    tiles = seq // tm
    out_shape[2] = jax.ShapeDtypeStruct((t // seq, DIFF_HEADS, VT_ROWS, seq), BF16)
    out_specs[2] = pl.BlockSpec((1, DIFF_HEADS, VT_ROWS, tm), lambda i: (i // tiles, 0, 0, i % tiles))
    return pl.pallas_call(
        functools.partial(_inproj_kernel, apply_ln=apply_ln),
        out_shape=out_shape,
        grid=(t // tm,),
        in_specs=in_specs,
        out_specs=out_specs,
        compiler_params=pltpu.CompilerParams(
            dimension_semantics=("arbitrary",), vmem_limit_bytes=V7X_VMEM_LIMIT_BYTES),
        name="ln_inproj" if apply_ln else "inproj",
    )(*args)


def _attn_kernel(lq1_ref, lk1_ref, lq2_ref, lk2_ref, nw_ref, q_ref, k_ref, v_ref, band_ref,
                 o_ref, s_s, e_s, *, lam_init, tq, sub, seq, kc):
    qi = pl.program_id(2)
    lam = (jnp.exp(jnp.sum(lq1_ref[...] * lk1_ref[...], axis=-1, keepdims=True))
           - jnp.exp(jnp.sum(lq2_ref[...] * lk2_ref[...], axis=-1, keepdims=True)) + lam_init)
    first = lax.broadcasted_iota(jnp.int32, (1, 2 * DIFF_QK_DIM), 1) < DIFF_QK_DIM
    nsb = tq // sub
    nslots = s_s.shape[0]

    def logits(sb):
        q = q_ref[0, sb * sub:(sb + 1) * sub, :]
        zero = jnp.zeros_like(q)
        qq = jnp.concatenate([jnp.where(first, q, zero), jnp.where(first, zero, q)], axis=0)
        off = seq - sub - (qi * tq + sb * sub)
        m = jnp.full((1, 2 * sub), -jnp.inf, F32)
        for c in range(seq // kc):
            rows = slice(c * kc, (c + 1) * kc)
            s = lax.dot_general(k_ref[0, rows, :], qq, _NT, preferred_element_type=F32)
            bias = band_ref[0, pl.ds(pl.multiple_of(off + c * kc, 128), kc), :]
            s = s + jnp.concatenate([bias, bias], axis=1)
            s_s[sb % nslots, rows, :] = s
            m = jnp.maximum(m, jnp.max(s, axis=0, keepdims=True))
        return m

    def softmax_pv(sb, m):
        for c in range(seq // kc):
            rows = slice(c * kc, (c + 1) * kc)
            e_s[sb % nslots, rows, :] = jnp.exp2(s_s[sb % nslots, rows, :] - m).astype(BF16)
        ot = _dot(v_ref[0, 0], e_s[sb % nslots])
        r1 = 1.0 / ot[DIFF_V_DIM:DIFF_V_DIM + 1, 0:sub]
        r2 = lam / ot[DIFF_V_DIM:DIFF_V_DIM + 1, sub:2 * sub]
        o = ot[0:DIFF_V_DIM, 0:sub] * r1 - ot[0:DIFF_V_DIM, sub:2 * sub] * r2
        y = o * lax.rsqrt(jnp.mean(o * o, axis=0, keepdims=True) + RMS_EPS) * nw_ref[...] * (1.0 - lam_init)
        o_ref[0, sb * sub:(sb + 1) * sub, :] = y.T.astype(o_ref.dtype)

    m_next = logits(0)
    for sb in range(nsb):
        m = m_next
        if sb + 1 < nsb:
            m_next = logits(sb + 1)
        softmax_pv(sb, m)


def _diff_attention(qd, kd, vd, band, lq1, lk1, lq2, lk2, norm_w, *, lam_init, tq, sub):
    b, seq, _ = qd.shape
    nq = seq // tq
    vec = lambda n: pl.BlockSpec((1, n), lambda h, bi, qi: (0, 0))
    return pl.pallas_call(
        functools.partial(_attn_kernel, lam_init=lam_init, tq=tq, sub=sub, seq=seq, kc=512),
        out_shape=jax.ShapeDtypeStruct((b, seq, DIFF_WIDTH), BF16),
        grid=(DIFF_HEADS, b, nq),
        in_specs=[vec(DIFF_QK_DIM), vec(DIFF_QK_DIM), vec(DIFF_QK_DIM), vec(DIFF_QK_DIM),
                  pl.BlockSpec((DIFF_V_DIM, 1), lambda h, bi, qi: (0, 0)),
                  pl.BlockSpec((1, tq, 128), lambda h, bi, qi: (bi, qi, h)),
                  pl.BlockSpec((1, seq, 128), lambda h, bi, qi: (bi, 0, h)),
                  pl.BlockSpec((1, 1, VT_ROWS, seq), lambda h, bi, qi: (bi, h, 0, 0)),
                  pl.BlockSpec((1, 2 * seq - sub, sub), lambda h, bi, qi: (h, 0, 0))],
        out_specs=pl.BlockSpec((1, tq, 128), lambda h, bi, qi: (bi, qi, h)),
        scratch_shapes=[pltpu.VMEM((min(tq // sub, 3), seq, 2 * sub), F32),
                        pltpu.VMEM((min(tq // sub, 3), seq, 2 * sub), BF16)],
        compiler_params=pltpu.CompilerParams(
            dimension_semantics=("arbitrary", "arbitrary", "arbitrary"),
            vmem_limit_bytes=V7X_VMEM_LIMIT_BYTES),
        name="diff_attn",
    )(lq1.reshape(1, -1), lk1.reshape(1, -1), lq2.reshape(1, -1), lk2.reshape(1, -1),
      norm_w.reshape(-1, 1), qd, kd, vd, band)


def _chunk_scan(x, row, *, reverse):
    n = x.shape[0]
    d = 1
    while d < GLA_CHUNK:
        if reverse:
            x = x + jnp.where(row < GLA_CHUNK - d, pltpu.roll(x, n - d, 0), 0.0)
        else:
            x = x + jnp.where(row >= d, pltpu.roll(x, d, 0), 0.0)
        d *= 2
    return x


def _gla_kernel(gd_ref, up_ref, gb_ref, q_ref, k_ref, v_ref, r_ref, nw_ref, o_ref,
                qq_s, kk_s, bf_s, bb_s, kv_s, st_s, *, seq):
    c = GLA_CHUNK
    blk = 2 * c
    nb = seq // blk

    row = lax.broadcasted_iota(jnp.int32, (blk, 1), 0) & (c - 1)
    first_half = lax.broadcasted_iota(jnp.int32, (blk, 1), 0) < c

    def gates(hp, n):
        rows = slice(n * blk, (n + 1) * blk)
        z = _dot(gd_ref[0, rows, :].astype(BF16), up_ref[hp, 0]) + gb_ref[hp, 0]
        g = (jnp.minimum(z, 0.0) - jnp.log(1.0 + jnp.exp(-jnp.abs(z)))) * (1.0 / GLA_GATE_TAU)
        pf = _chunk_scan(g[:, :128], row, reverse=False)
        ub = _chunk_scan(g[:, 128:], row, reverse=True)
        bf_s[hp, rows, :] = pf
        bb_s[hp, rows, :] = ub
        bfn = pf - jnp.where(first_half, pf[c - 1:c, :], 0.0)
        bbn = ub - jnp.where(first_half, 0.0, ub[c:c + 1, :])
        q = q_ref[0, rows, hp * 128:(hp + 1) * 128]
        k = k_ref[0, rows, hp * 128:(hp + 1) * 128]
        qq_s[hp, rows, 0:128] = (q * jnp.exp(bfn)).astype(BF16)
        kk_s[hp, rows, 0:128] = (k * jnp.exp(-bfn)).astype(BF16)
        qq_s[hp, rows, 128:256] = (q * jnp.exp(bbn)).astype(BF16)
        kk_s[hp, rows, 128:256] = (k * jnp.exp(-bbn)).astype(BF16)

    def kv_scan(hp):
        for n in range(nb):
            rows = slice(n * blk, (n + 1) * blk)
            kv_s[hp, n] = lax.dot_general(v_ref[0, rows, hp * 256:(hp + 1) * 256], kk_s[hp, rows, :], _TN,
                                          preferred_element_type=F32)
        sr = lax.broadcasted_iota(jnp.int32, (256, 128), 0) < GLA_V_DIM
        sc = lax.broadcasted_iota(jnp.int32, (256, 128), 1) < GLA_K_DIM
        same_head = sr == sc
        sf = jnp.zeros((256, 128), F32)
        sb = jnp.zeros((256, 128), F32)
        for i in range(nb):
            t1 = bf_s[hp, i * blk + c - 1:i * blk + c, :]
            p2 = bf_s[hp, i * blk + blk - 1:i * blk + blk, :]
            st_s[hp, i, :, 0:128] = (sf * jnp.exp(t1)).astype(BF16)
            sf = jnp.exp(t1 + p2) * sf + jnp.exp(p2) * jnp.where(same_head, kv_s[hp, i, :, 0:128], 0.0)
            n = nb - 1 - i
            t2 = bb_s[hp, n * blk + c:n * blk + c + 1, :]
            u1 = bb_s[hp, n * blk:n * blk + 1, :]
            st_s[hp, n, :, 128:256] = (sb * jnp.exp(t2)).astype(BF16)
            sb = jnp.exp(u1 + t2) * sb + jnp.exp(u1) * jnp.where(same_head, kv_s[hp, n, :, 128:256], 0.0)

    lane = lax.broadcasted_iota(jnp.int32, (1, 256), 1)
    head0_v = lane < GLA_V_DIM
    quarter = [(lane >= i * GLA_K_DIM) & (lane < (i + 1) * GLA_K_DIM) for i in range(4)]
    ci = lax.broadcasted_iota(jnp.int32, (blk, 2 * blk), 0)
    si = lax.broadcasted_iota(jnp.int32, (blk, 2 * blk), 1) & (blk - 1)
    causal = si <= ci
    anti = si > ci

    def scores_of(hp, n):
        rows = slice(n * blk, (n + 1) * blk)
        kn = kk_s[hp, rows, :]
        zk = jnp.zeros_like(kn)
        kbd = jnp.concatenate([jnp.where(m, kn, zk) for m in quarter], axis=0)
        return lax.dot_general(qq_s[hp, rows, :], kbd, _NT, preferred_element_type=F32)

    def outputs(hp, n, scores):
        rows = slice(n * blk, (n + 1) * blk)
        qn = qq_s[hp, rows, :]
        vn = v_ref[0, rows, hp * 256:(hp + 1) * 256]
        prob = (jnp.where(causal, scores[:, 0:2 * blk], 0.0)
                + jnp.where(anti, scores[:, 2 * blk:4 * blk], 0.0)).astype(BF16)
        zv = jnp.zeros_like(vn)
        vbd = jnp.concatenate([jnp.where(head0_v, vn, zv), jnp.where(head0_v, zv, vn)], axis=0)
        o = _dot(prob, vbd) + lax.dot_general(qn, st_s[hp, n], _NT, preferred_element_type=F32)
        nw = nw_ref[...]
        for hh in range(2):
            cols = slice(hp * 256 + hh * GLA_V_DIM, hp * 256 + (hh + 1) * GLA_V_DIM)
            oh = o[:, hh * GLA_V_DIM:(hh + 1) * GLA_V_DIM]
            y = oh * lax.rsqrt(jnp.mean(oh * oh, axis=-1, keepdims=True) + RMS_EPS) * nw
            gate = r_ref[0, rows, cols]
            o_ref[0, rows, cols] = (y * (gate * jax.nn.sigmoid(gate))).astype(o_ref.dtype)

    for n in range(nb):
        gates(0, n)
    kv_scan(0)
    for hp in range(2):
        sc_next = scores_of(hp, 0)
        for n in range(nb):
            sc = sc_next
            if n + 1 < nb:
                sc_next = scores_of(hp, n + 1)
            outputs(hp, n, sc)
            if hp == 0:
                gates(1, n)
        if hp == 0:
            kv_scan(1)


def _gla(gd, up_bd, gb_bd, gq, gk, gv, gr, norm_w):
    b, seq, _ = gq.shape
    hp = GLA_HEADS // 2
    return pl.pallas_call(
        functools.partial(_gla_kernel, seq=seq),
        out_shape=jax.ShapeDtypeStruct((b, seq, GLA_WIDTH), BF16),
        grid=(b,),
        in_specs=[pl.BlockSpec((1, seq, 2 * GLA_GATE_RANK), lambda bi: (bi, 0, 0)),
                  pl.BlockSpec((hp, 1, 2 * GLA_GATE_RANK, 256), lambda bi: (0, 0, 0, 0)),
                  pl.BlockSpec((hp, 1, 1, 256), lambda bi: (0, 0, 0, 0)),
                  pl.BlockSpec((1, seq, GLA_KEY_WIDTH), lambda bi: (bi, 0, 0)),
                  pl.BlockSpec((1, seq, GLA_KEY_WIDTH), lambda bi: (bi, 0, 0)),
                  pl.BlockSpec((1, seq, GLA_WIDTH), lambda bi: (bi, 0, 0)),
                  pl.BlockSpec((1, seq, GLA_WIDTH), lambda bi: (bi, 0, 0)),
                  pl.BlockSpec((1, GLA_V_DIM), lambda bi: (0, 0))],
        out_specs=pl.BlockSpec((1, seq, GLA_WIDTH), lambda bi: (bi, 0, 0)),
        scratch_shapes=[pltpu.VMEM((hp, seq, 256), BF16), pltpu.VMEM((hp, seq, 256), BF16),
                        pltpu.VMEM((hp, seq, 128), F32), pltpu.VMEM((hp, seq, 128), F32),
                        pltpu.VMEM((hp, seq // (2 * GLA_CHUNK), 256, 256), F32),
                        pltpu.VMEM((hp, seq // (2 * GLA_CHUNK), 256, 256), BF16)],
        compiler_params=pltpu.CompilerParams(
            dimension_semantics=("arbitrary",),
            vmem_limit_bytes=V7X_VMEM_LIMIT_BYTES),
        name="gla",
    )(gd, up_bd, gb_bd, gq, gk, gv, gr, norm_w.reshape(1, -1))


def _mix_mlp_kernel(d_ref, g_ref, x_ref, eg_ref, eb_ref, wo_ref, l1g_ref, l1b_ref, w1_ref, b1_ref, w2_ref, b2_ref,
                    l2g_ref, l2b_ref, o_ref, *, ff_chunk, emb_ln):
    tm = x_ref.shape[0]
    halves = tuple(slice(r0, r0 + MIX_SUB_ROWS) for r0 in range(0, tm, MIX_SUB_ROWS))
    x1s = []
    for rows in halves:
        mix = (_dot(d_ref[rows, :], wo_ref[0, 0:DIFF_WIDTH, :])
               + _dot(g_ref[rows, :], wo_ref[0, DIFF_WIDTH:D_MODEL, :]))
        x = x_ref[rows, :]
        if emb_ln:
            x = _layer_norm(x, eg_ref[...], eb_ref[...])
        x1s.append(_layer_norm(ALPHA * x + mix, l1g_ref[0], l1b_ref[0]))
    for rows, x1 in zip(halves, x1s):
        xb = x1.astype(BF16)
        acc = jnp.zeros(x1.shape, F32)
        for c0 in range(0, D_FF, ff_chunk):
            h = _dot(xb, w1_ref[0, :, c0:c0 + ff_chunk]) + b1_ref[0, :, c0:c0 + ff_chunk]
            h = jnp.square(jnp.maximum(h, 0.0)).astype(BF16)
            acc = acc + _dot(h, w2_ref[0, c0:c0 + ff_chunk, :])
        o_ref[rows, :] = _layer_norm(ALPHA * x1 + (acc + b2_ref[0]), l2g_ref[0], l2b_ref[0])


def _mix_mlp(li, d2d, g2d, x2d, emb_g, emb_b, w_o, ln1_g, ln1_b, w1, b1, w2, b2, ln2_g, ln2_b,
             *, emb_ln, tm=1024, ff_chunk=1024):
    t = x2d.shape[0]
    row = lambda n: pl.BlockSpec((tm, n), lambda i: (i, 0))
    const = lambda shape: pl.BlockSpec(shape, lambda i: (0,) * len(shape))
    layer = lambda a: pl.BlockSpec((1,) + a.shape[1:], lambda i: (li,) + (0,) * (a.ndim - 1),
                                   pipeline_mode=pl.Buffered(1))
    vecs = [v.reshape(DEPTH, 1, -1) for v in (ln1_g, ln1_b, b1, b2, ln2_g, ln2_b)]
    l1g, l1b, b1r, b2r, l2g, l2b = vecs
    args = (d2d, g2d, x2d, emb_g.reshape(1, -1), emb_b.reshape(1, -1),
            w_o, l1g, l1b, w1, b1r, w2, b2r, l2g, l2b)
    in_specs = ([row(DIFF_WIDTH), row(GLA_WIDTH), row(D_MODEL), const((1, D_MODEL)), const((1, D_MODEL))]
                + [layer(a) for a in args[5:]])
    return pl.pallas_call(
        functools.partial(_mix_mlp_kernel, ff_chunk=ff_chunk, emb_ln=emb_ln),
        out_shape=jax.ShapeDtypeStruct((t, D_MODEL), F32),
        grid=(t // tm,),
        in_specs=in_specs,
        out_specs=row(D_MODEL),
        compiler_params=pltpu.CompilerParams(
            dimension_semantics=("arbitrary",), vmem_limit_bytes=V7X_VMEM_LIMIT_BYTES),
        name="mix_mlp",
    )(*args)


def _gate_weights(gate_up, gate_bias):
    hp = GLA_HEADS // 2
    up = gate_up.reshape(2, GLA_GATE_RANK, hp, 128)
    z = jnp.zeros((GLA_GATE_RANK, hp, 128), gate_up.dtype)
    top = jnp.concatenate([up[0], z], axis=-1)
    bot = jnp.concatenate([z, up[1]], axis=-1)
    up_bd = jnp.concatenate([top, bot], axis=0)
    up_bd = up_bd.transpose(1, 0, 2)[:, None].astype(BF16)
    gb = gate_bias.reshape(2, hp, 128)
    gb_bd = jnp.concatenate([gb[0], gb[1]], axis=-1)[:, None, None]
    return up_bd, gb_bd


def kernel(x, ln_emb_g, ln_emb_b, rel_bias_table, w_in, lambda_q1, lambda_k1, lambda_q2, lambda_k2,
           diff_norm_w, gla_gate_up, gla_gate_bias, gla_norm_w, w_o, ln1_g, ln1_b,
           w_ffn1, b_ffn1, w_ffn2, b_ffn2, ln2_g, ln2_b):
    b, seq, _ = x.shape
    t = b * seq
    tq, sub = 2048, 256
    band = _bias_band(rel_bias_table, sub, seq)
    w_in_b = w_in.astype(BF16)
    w_o_b = w_o.astype(BF16)
    w1_b = w_ffn1.astype(BF16)
    w2_b = w_ffn2.astype(BF16)

    h = x.reshape(t, D_MODEL)
    for li in range(DEPTH):
        emb = (ln_emb_g, ln_emb_b) if li == 0 else ()
        qd, kd, vd, gq, gk, gv, gr, gd = _inproj(li, h, w_in_b, seq, *emb)
        r3 = lambda a: a.reshape(b, seq, a.shape[-1])
        lam_init = 0.8 - 0.6 * math.exp(-0.3 * li)
        d_out = _diff_attention(r3(qd), r3(kd), vd, band, lambda_q1[li], lambda_k1[li],
                                lambda_q2[li], lambda_k2[li], diff_norm_w[li],
                                lam_init=lam_init, tq=tq, sub=sub)
        up_bd, gb_bd = _gate_weights(gla_gate_up[li], gla_gate_bias[li])
        g_out = _gla(r3(gd), up_bd, gb_bd, r3(gq), r3(gk), r3(gv), r3(gr), gla_norm_w[li])
        h = _mix_mlp(li, d_out.reshape(t, DIFF_WIDTH), g_out.reshape(t, GLA_WIDTH), h, ln_emb_g, ln_emb_b,
                     w_o_b, ln1_g, ln1_b, w1_b, b_ffn1, w2_b, b_ffn2, ln2_g, ln2_b, emb_ln=(li == 0))
    return h.reshape(b, seq, D_MODEL)
```

```python
import functools
import math

import jax
import jax.numpy as jnp
from jax import lax
from jax.experimental import pallas as pl
from jax.experimental.pallas import tpu as pltpu

D_MODEL = 1024
DEPTH = 2
DIFF_HEADS = 4
DIFF_QK_DIM = 64
DIFF_V_DIM = 128
DIFF_WIDTH = 512
GLA_HEADS = 4
GLA_WIDTH = 512
GLA_V_DIM = 128
GLA_K_DIM = 64
GLA_KEY_WIDTH = 256
GLA_GATE_RANK = 16
GLA_GATE_TAU = 16.0
GLA_CHUNK = 64
D_FF = 4096
N_BUCKETS = 32
LN_EPS = 1e-5
RMS_EPS = 1e-5
ALPHA = (2.0 * DEPTH) ** 0.25
LOG2E = math.log2(math.e)
VT_ROWS = DIFF_V_DIM + 16
MIX_SUB_ROWS = 256
D_IN = 3104

V7X_VMEM_LIMIT_BYTES = 56 * 1024 * 1024

BF16 = jnp.bfloat16
F32 = jnp.float32

_NT = (((1,), (1,)), ((), ()))
_TN = (((0,), (0,)), ((), ()))


def _dot(a, b):
    return jnp.dot(a, b, preferred_element_type=F32)


def _layer_norm(y, g, b):
    mu = jnp.mean(y, axis=-1, keepdims=True)
    d = y - mu
    var = jnp.mean(d * d, axis=-1, keepdims=True)
    return d * lax.rsqrt(var + LN_EPS) * g + b


def _band_kernel(table_ref, band_ref, *, tq, seq):
    h = pl.program_id(0)
    width = 2 * seq - tq
    lo, hi = seq - tq - 128, seq + 128
    near = hi - lo
    r = lax.broadcasted_iota(jnp.int32, (near, tq), 1)
    m = lax.broadcasted_iota(jnp.int32, (near, tq), 0) + lo
    rel = m - (seq - tq) - r
    n = jnp.abs(rel)
    n2 = n * n
    large = jnp.full_like(n, 8)
    for p in range(7, 14):
        large = large + jnp.where(n2 >= (1 << p), 1, 0)
    bucket = jnp.where(n < 8, n, large) + jnp.where(rel > 0, 16, 0)
    acc = jnp.zeros((near, tq), F32)
    for i in range(N_BUCKETS):
        acc = jnp.where(bucket == i, table_ref[i * DIFF_HEADS + h], acc)
    half = N_BUCKETS // 2
    band_ref[0, 0:lo, :] = jnp.full((lo, tq), table_ref[(half - 1) * DIFF_HEADS + h] * LOG2E, F32)
    band_ref[0, lo:hi, :] = acc * LOG2E
    band_ref[0, hi:width, :] = jnp.full((width - hi, tq), table_ref[(N_BUCKETS - 1) * DIFF_HEADS + h] * LOG2E, F32)


def _bias_band(table, tq, seq):
    width = 2 * seq - tq
    return pl.pallas_call(
        functools.partial(_band_kernel, tq=tq, seq=seq),
        out_shape=jax.ShapeDtypeStruct((DIFF_HEADS, width, tq), F32),
        grid=(DIFF_HEADS,),
        in_specs=[pl.BlockSpec(memory_space=pltpu.SMEM)],
        out_specs=pl.BlockSpec((1, width, tq), lambda h: (h, 0, 0)),
        name="bias_band",
    )(table.reshape(-1))


def _inproj_kernel(*refs, apply_ln, n_cast):
    n_in = (4 if apply_ln else 2) + n_cast
    ins, outs = refs[:n_in], refs[n_in:]
    if n_cast:
        for src, dst in zip(ins[-n_cast:], outs[-n_cast:]):
            dst[...] = src[...].astype(BF16)
        ins, outs = ins[:-n_cast], outs[:-n_cast]
    refs = ins + outs
    if apply_ln:
        (x_ref, g_ref, b_ref, w_ref,
         qd_ref, kd_ref, vd_ref, gq_ref, gk_ref, gv_ref, gr_ref, gd_ref) = refs
        xn = _layer_norm(x_ref[...], g_ref[...], b_ref[...])
    else:
        (x_ref, w_ref,
         qd_ref, kd_ref, vd_ref, gq_ref, gk_ref, gv_ref, gr_ref, gd_ref) = refs
        xn = x_ref[...]
    xb = xn.astype(BF16)
    qd_ref[...] = (_dot(xb, w_ref[0, :,0:512]) * (DIFF_QK_DIM ** -0.5 * LOG2E)).astype(BF16)
    kd_ref[...] = _dot(xb, w_ref[0, :,512:1024]).astype(BF16)
    tm = xb.shape[0]
    pad_rows = VT_ROWS - DIFF_V_DIM
    ones_row = (lax.broadcasted_iota(jnp.int32, (pad_rows, tm), 0) == 0).astype(BF16)
    v = _dot(xb, w_ref[0, :,1024:1536])
    for hd in range(DIFF_HEADS):
        vd_ref[0, hd, 0:DIFF_V_DIM, :] = v[:, hd * DIFF_V_DIM:(hd + 1) * DIFF_V_DIM].T.astype(BF16)
        vd_ref[0, hd, DIFF_V_DIM:VT_ROWS, :] = ones_row
    gq_ref[...] = _dot(xb, w_ref[0, :,1536:1792]) * (GLA_K_DIM ** -0.5)
    gk_ref[...] = _dot(xb, w_ref[0, :,1792:2048])
    gv_ref[...] = _dot(xb, w_ref[0, :,2048:2560]).astype(BF16)
    gr_ref[...] = _dot(xb, w_ref[0, :,2560:3072])
    gd_ref[...] = _dot(xb, w_ref[0, :,3072:D_IN])


def _inproj(li, x2d, w_pad, seq, ln_g=None, ln_b=None, *, cast=(), tm=1024):
    t = x2d.shape[0]
    apply_ln = ln_g is not None
    row = lambda n: pl.BlockSpec((tm, n), lambda i: (i, 0))
    const = lambda shape: pl.BlockSpec(shape, lambda i: (0,) * len(shape))
    in_specs = [row(D_MODEL)]
    args = [x2d]
    if apply_ln:
        in_specs += [const((1, D_MODEL)), const((1, D_MODEL))]
        args += [ln_g.reshape(1, -1), ln_b.reshape(1, -1)]
    in_specs.append(pl.BlockSpec((1, D_MODEL, D_IN), lambda i: (li, 0, 0), pipeline_mode=pl.Buffered(1)))
    args.append(w_pad)
    widths = [(512, BF16), (512, BF16), None, (256, F32), (256, F32),
              (512, BF16), (512, F32), (2 * GLA_GATE_RANK, F32)]
    out_shape = [jax.ShapeDtypeStruct((t, w[0]), w[1]) if w else None for w in widths]
    out_specs = [row(w[0]) if w else None for w in widths]
    tiles = seq // tm
    out_shape[2] = jax.ShapeDtypeStruct((t // seq, DIFF_HEADS, VT_ROWS, seq), BF16)
    out_specs[2] = pl.BlockSpec((1, DIFF_HEADS, VT_ROWS, tm), lambda i: (i // tiles, 0, 0, i % tiles))
    steps = t // tm
    for wgt in cast:
        depth, r, c = wgt.shape
        slab = pl.BlockSpec((depth, r // steps, c), lambda i: (0, i, 0))
        in_specs.append(slab)
        args.append(wgt)
        out_shape.append(jax.ShapeDtypeStruct(wgt.shape, BF16))
        out_specs.append(slab)
    return pl.pallas_call(
        functools.partial(_inproj_kernel, apply_ln=apply_ln, n_cast=len(cast)),
        out_shape=out_shape,
        grid=(t // tm,),
        in_specs=in_specs,
        out_specs=out_specs,
        compiler_params=pltpu.CompilerParams(
            dimension_semantics=("arbitrary",), vmem_limit_bytes=V7X_VMEM_LIMIT_BYTES),
        name="ln_inproj" if apply_ln else "inproj",
    )(*args)


def _attn_kernel(lq1_ref, lk1_ref, lq2_ref, lk2_ref, nw_ref, q_ref, k_ref, v_ref, band_ref,
                 o_ref, s_s, e_s, *, lam_init, tq, sub, seq, kc):
    qi = pl.program_id(2)
    lam = (jnp.exp(jnp.sum(lq1_ref[...] * lk1_ref[...], axis=-1, keepdims=True))
           - jnp.exp(jnp.sum(lq2_ref[...] * lk2_ref[...], axis=-1, keepdims=True)) + lam_init)
    first = lax.broadcasted_iota(jnp.int32, (1, 2 * DIFF_QK_DIM), 1) < DIFF_QK_DIM
    nsb = tq // sub
    nslots = s_s.shape[0]

    def logits(sb):
        q = q_ref[0, sb * sub:(sb + 1) * sub, :]
        zero = jnp.zeros_like(q)
        qq = jnp.concatenate([jnp.where(first, q, zero), jnp.where(first, zero, q)], axis=0)
        off = seq - sub - (qi * tq + sb * sub)
        m = jnp.full((1, 2 * sub), -jnp.inf, F32)
        for c in range(seq // kc):
            rows = slice(c * kc, (c + 1) * kc)
            s = lax.dot_general(k_ref[0, rows, :], qq, _NT, preferred_element_type=F32)
            bias = band_ref[0, pl.ds(pl.multiple_of(off + c * kc, 128), kc), :]
            s = s + jnp.concatenate([bias, bias], axis=1)
            s_s[sb % nslots, rows, :] = s
            m = jnp.maximum(m, jnp.max(s, axis=0, keepdims=True))
        return m

    def softmax_pv(sb, m):
        for c in range(seq // kc):
            rows = slice(c * kc, (c + 1) * kc)
            e_s[sb % nslots, rows, :] = jnp.exp2(s_s[sb % nslots, rows, :] - m).astype(BF16)
        ot = _dot(v_ref[0, 0], e_s[sb % nslots])
        r1 = 1.0 / ot[DIFF_V_DIM:DIFF_V_DIM + 1, 0:sub]
        r2 = lam / ot[DIFF_V_DIM:DIFF_V_DIM + 1, sub:2 * sub]
        o = ot[0:DIFF_V_DIM, 0:sub] * r1 - ot[0:DIFF_V_DIM, sub:2 * sub] * r2
        y = o * lax.rsqrt(jnp.mean(o * o, axis=0, keepdims=True) + RMS_EPS) * nw_ref[...] * (1.0 - lam_init)
        o_ref[0, sb * sub:(sb + 1) * sub, :] = y.T.astype(o_ref.dtype)

    m_next = logits(0)
    for sb in range(nsb):
        m = m_next
        if sb + 1 < nsb:
            m_next = logits(sb + 1)
        softmax_pv(sb, m)


def _diff_attention(qd, kd, vd, band, lq1, lk1, lq2, lk2, norm_w, *, lam_init, tq, sub):
    b, seq, _ = qd.shape
    nq = seq // tq
    vec = lambda n: pl.BlockSpec((1, n), lambda h, bi, qi: (0, 0))
    return pl.pallas_call(
        functools.partial(_attn_kernel, lam_init=lam_init, tq=tq, sub=sub, seq=seq, kc=512),
        out_shape=jax.ShapeDtypeStruct((b, seq, DIFF_WIDTH), BF16),
        grid=(DIFF_HEADS, b, nq),
        in_specs=[vec(DIFF_QK_DIM), vec(DIFF_QK_DIM), vec(DIFF_QK_DIM), vec(DIFF_QK_DIM),
                  pl.BlockSpec((DIFF_V_DIM, 1), lambda h, bi, qi: (0, 0)),
                  pl.BlockSpec((1, tq, 128), lambda h, bi, qi: (bi, qi, h)),
                  pl.BlockSpec((1, seq, 128), lambda h, bi, qi: (bi, 0, h)),
                  pl.BlockSpec((1, 1, VT_ROWS, seq), lambda h, bi, qi: (bi, h, 0, 0)),
                  pl.BlockSpec((1, 2 * seq - sub, sub), lambda h, bi, qi: (h, 0, 0))],
        out_specs=pl.BlockSpec((1, tq, 128), lambda h, bi, qi: (bi, qi, h)),
        scratch_shapes=[pltpu.VMEM((min(tq // sub, 3), seq, 2 * sub), F32),
                        pltpu.VMEM((min(tq // sub, 3), seq, 2 * sub), BF16)],
        compiler_params=pltpu.CompilerParams(
            dimension_semantics=("arbitrary", "arbitrary", "arbitrary"),
            vmem_limit_bytes=V7X_VMEM_LIMIT_BYTES),
        name="diff_attn",
    )(lq1.reshape(1, -1), lk1.reshape(1, -1), lq2.reshape(1, -1), lk2.reshape(1, -1),
      norm_w.reshape(-1, 1), qd, kd, vd, band)


def _chunk_scan(x, row, *, reverse):
    n = x.shape[0]
    d = 1
    while d < GLA_CHUNK:
        if reverse:
            x = x + jnp.where(row < GLA_CHUNK - d, pltpu.roll(x, n - d, 0), 0.0)
        else:
            x = x + jnp.where(row >= d, pltpu.roll(x, d, 0), 0.0)
        d *= 2
    return x


def _gla_kernel(gd_ref, up_ref, gb_ref, q_ref, k_ref, v_ref, r_ref, nw_ref, o_ref,
                qq_s, kk_s, bf_s, bb_s, kv_s, st_s, *, seq):
    c = GLA_CHUNK
    blk = 2 * c
    nb = seq // blk

    row = lax.broadcasted_iota(jnp.int32, (blk, 1), 0) & (c - 1)
    first_half = lax.broadcasted_iota(jnp.int32, (blk, 1), 0) < c

    def gates(hp, n):
        rows = slice(n * blk, (n + 1) * blk)
        z = _dot(gd_ref[0, rows, :].astype(BF16), up_ref[hp, 0]) + gb_ref[hp, 0]
        g = (jnp.minimum(z, 0.0) - jnp.log(1.0 + jnp.exp(-jnp.abs(z)))) * (1.0 / GLA_GATE_TAU)
        pf = _chunk_scan(g[:, :128], row, reverse=False)
        ub = _chunk_scan(g[:, 128:], row, reverse=True)
        bf_s[hp, rows, :] = pf
        bb_s[hp, rows, :] = ub
        bfn = pf - jnp.where(first_half, pf[c - 1:c, :], 0.0)
        bbn = ub - jnp.where(first_half, 0.0, ub[c:c + 1, :])
        q = q_ref[0, rows, hp * 128:(hp + 1) * 128]
        k = k_ref[0, rows, hp * 128:(hp + 1) * 128]
        qq_s[hp, rows, 0:128] = (q * jnp.exp(bfn)).astype(BF16)
        kk_s[hp, rows, 0:128] = (k * jnp.exp(-bfn)).astype(BF16)
        qq_s[hp, rows, 128:256] = (q * jnp.exp(bbn)).astype(BF16)
        kk_s[hp, rows, 128:256] = (k * jnp.exp(-bbn)).astype(BF16)

    def kv_scan(hp):
        for n in range(nb):
            rows = slice(n * blk, (n + 1) * blk)
            kv_s[hp, n] = lax.dot_general(v_ref[0, rows, hp * 256:(hp + 1) * 256], kk_s[hp, rows, :], _TN,
                                          preferred_element_type=F32)
        sr = lax.broadcasted_iota(jnp.int32, (256, 128), 0) < GLA_V_DIM
        sc = lax.broadcasted_iota(jnp.int32, (256, 128), 1) < GLA_K_DIM
        same_head = sr == sc
        sf = jnp.zeros((256, 128), F32)
        sb = jnp.zeros((256, 128), F32)
        for i in range(nb):
            t1 = bf_s[hp, i * blk + c - 1:i * blk + c, :]
            p2 = bf_s[hp, i * blk + blk - 1:i * blk + blk, :]
            st_s[hp, i, :, 0:128] = (sf * jnp.exp(t1)).astype(BF16)
            sf = jnp.exp(t1 + p2) * sf + jnp.exp(p2) * jnp.where(same_head, kv_s[hp, i, :, 0:128], 0.0)
            n = nb - 1 - i
            t2 = bb_s[hp, n * blk + c:n * blk + c + 1, :]
            u1 = bb_s[hp, n * blk:n * blk + 1, :]
            st_s[hp, n, :, 128:256] = (sb * jnp.exp(t2)).astype(BF16)
            sb = jnp.exp(u1 + t2) * sb + jnp.exp(u1) * jnp.where(same_head, kv_s[hp, n, :, 128:256], 0.0)

    lane = lax.broadcasted_iota(jnp.int32, (1, 256), 1)
    head0_v = lane < GLA_V_DIM
    quarter = [(lane >= i * GLA_K_DIM) & (lane < (i + 1) * GLA_K_DIM) for i in range(4)]
    ci = lax.broadcasted_iota(jnp.int32, (blk, 2 * blk), 0)
    si = lax.broadcasted_iota(jnp.int32, (blk, 2 * blk), 1) & (blk - 1)
    causal = si <= ci
    anti = si > ci

    def scores_of(hp, n):
        rows = slice(n * blk, (n + 1) * blk)
        kn = kk_s[hp, rows, :]
        zk = jnp.zeros_like(kn)
        kbd = jnp.concatenate([jnp.where(m, kn, zk) for m in quarter], axis=0)
        return lax.dot_general(qq_s[hp, rows, :], kbd, _NT, preferred_element_type=F32)

    def outputs(hp, n, scores):
        rows = slice(n * blk, (n + 1) * blk)
        qn = qq_s[hp, rows, :]
        vn = v_ref[0, rows, hp * 256:(hp + 1) * 256]
        prob = (jnp.where(causal, scores[:, 0:2 * blk], 0.0)
                + jnp.where(anti, scores[:, 2 * blk:4 * blk], 0.0)).astype(BF16)
        zv = jnp.zeros_like(vn)
        vbd = jnp.concatenate([jnp.where(head0_v, vn, zv), jnp.where(head0_v, zv, vn)], axis=0)
        o = _dot(prob, vbd) + lax.dot_general(qn, st_s[hp, n], _NT, preferred_element_type=F32)
        nw = nw_ref[...]
        for hh in range(2):
            cols = slice(hp * 256 + hh * GLA_V_DIM, hp * 256 + (hh + 1) * GLA_V_DIM)
            oh = o[:, hh * GLA_V_DIM:(hh + 1) * GLA_V_DIM]
            y = oh * lax.rsqrt(jnp.mean(oh * oh, axis=-1, keepdims=True) + RMS_EPS) * nw
            gate = r_ref[0, rows, cols]
            o_ref[0, rows, cols] = (y * (gate * jax.nn.sigmoid(gate))).astype(o_ref.dtype)

    for n in range(nb):
        gates(0, n)
    kv_scan(0)
    for hp in range(2):
        sc_next = scores_of(hp, 0)
        for n in range(nb):
            sc = sc_next
            if n + 1 < nb:
                sc_next = scores_of(hp, n + 1)
            outputs(hp, n, sc)
            if hp == 0:
                gates(1, n)
        if hp == 0:
            kv_scan(1)


def _gla(gd, up_bd, gb_bd, gq, gk, gv, gr, norm_w):
    b, seq, _ = gq.shape
    hp = GLA_HEADS // 2
    return pl.pallas_call(
        functools.partial(_gla_kernel, seq=seq),
        out_shape=jax.ShapeDtypeStruct((b, seq, GLA_WIDTH), BF16),
        grid=(b,),
        in_specs=[pl.BlockSpec((1, seq, 2 * GLA_GATE_RANK), lambda bi: (bi, 0, 0)),
                  pl.BlockSpec((hp, 1, 2 * GLA_GATE_RANK, 256), lambda bi: (0, 0, 0, 0)),
                  pl.BlockSpec((hp, 1, 1, 256), lambda bi: (0, 0, 0, 0)),
                  pl.BlockSpec((1, seq, GLA_KEY_WIDTH), lambda bi: (bi, 0, 0)),
                  pl.BlockSpec((1, seq, GLA_KEY_WIDTH), lambda bi: (bi, 0, 0)),
                  pl.BlockSpec((1, seq, GLA_WIDTH), lambda bi: (bi, 0, 0)),
                  pl.BlockSpec((1, seq, GLA_WIDTH), lambda bi: (bi, 0, 0)),
                  pl.BlockSpec((1, GLA_V_DIM), lambda bi: (0, 0))],
        out_specs=pl.BlockSpec((1, seq, GLA_WIDTH), lambda bi: (bi, 0, 0)),
        scratch_shapes=[pltpu.VMEM((hp, seq, 256), BF16), pltpu.VMEM((hp, seq, 256), BF16),
                        pltpu.VMEM((hp, seq, 128), F32), pltpu.VMEM((hp, seq, 128), F32),
                        pltpu.VMEM((hp, seq // (2 * GLA_CHUNK), 256, 256), F32),
                        pltpu.VMEM((hp, seq // (2 * GLA_CHUNK), 256, 256), BF16)],
        compiler_params=pltpu.CompilerParams(
            dimension_semantics=("arbitrary",),
            vmem_limit_bytes=V7X_VMEM_LIMIT_BYTES),
        name="gla",
    )(gd, up_bd, gb_bd, gq, gk, gv, gr, norm_w.reshape(1, -1))


def _mix_mlp_kernel(d_ref, g_ref, x_ref, eg_ref, eb_ref, wo_ref, l1g_ref, l1b_ref, w1_ref, b1_ref, w2_ref, b2_ref,
                    l2g_ref, l2b_ref, o_ref, *, ff_chunk, emb_ln):
    tm = x_ref.shape[0]
    halves = tuple(slice(r0, r0 + MIX_SUB_ROWS) for r0 in range(0, tm, MIX_SUB_ROWS))
    x1s = []
    for rows in halves:
        mix = (_dot(d_ref[rows, :], wo_ref[0, 0:DIFF_WIDTH, :])
               + _dot(g_ref[rows, :], wo_ref[0, DIFF_WIDTH:D_MODEL, :]))
        x = x_ref[rows, :]
        if emb_ln:
            x = _layer_norm(x, eg_ref[...], eb_ref[...])
        x1s.append(_layer_norm(ALPHA * x + mix, l1g_ref[0], l1b_ref[0]))
    for rows, x1 in zip(halves, x1s):
        xb = x1.astype(BF16)
        acc = jnp.zeros(x1.shape, F32)
        for c0 in range(0, D_FF, ff_chunk):
            h = _dot(xb, w1_ref[0, :, c0:c0 + ff_chunk]) + b1_ref[0, :, c0:c0 + ff_chunk]
            h = jnp.square(jnp.maximum(h, 0.0)).astype(BF16)
            acc = acc + _dot(h, w2_ref[0, c0:c0 + ff_chunk, :])
        o_ref[rows, :] = _layer_norm(ALPHA * x1 + (acc + b2_ref[0]), l2g_ref[0], l2b_ref[0])


def _mix_mlp(li, d2d, g2d, x2d, emb_g, emb_b, w_o, ln1_g, ln1_b, w1, b1, w2, b2, ln2_g, ln2_b,
             *, emb_ln, tm=1024, ff_chunk=1024):
    t = x2d.shape[0]
    row = lambda n: pl.BlockSpec((tm, n), lambda i: (i, 0))
    const = lambda shape: pl.BlockSpec(shape, lambda i: (0,) * len(shape))
    layer = lambda a: pl.BlockSpec((1,) + a.shape[1:], lambda i: (li,) + (0,) * (a.ndim - 1),
                                   pipeline_mode=pl.Buffered(1))
    vecs = [v.reshape(DEPTH, 1, -1) for v in (ln1_g, ln1_b, b1, b2, ln2_g, ln2_b)]
    l1g, l1b, b1r, b2r, l2g, l2b = vecs
    args = (d2d, g2d, x2d, emb_g.reshape(1, -1), emb_b.reshape(1, -1),
            w_o, l1g, l1b, w1, b1r, w2, b2r, l2g, l2b)
    in_specs = ([row(DIFF_WIDTH), row(GLA_WIDTH), row(D_MODEL), const((1, D_MODEL)), const((1, D_MODEL))]
                + [layer(a) for a in args[5:]])
    return pl.pallas_call(
        functools.partial(_mix_mlp_kernel, ff_chunk=ff_chunk, emb_ln=emb_ln),
        out_shape=jax.ShapeDtypeStruct((t, D_MODEL), F32),
        grid=(t // tm,),
        in_specs=in_specs,
        out_specs=row(D_MODEL),
        compiler_params=pltpu.CompilerParams(
            dimension_semantics=("arbitrary",), vmem_limit_bytes=V7X_VMEM_LIMIT_BYTES),
        name="mix_mlp",
    )(*args)


def _gate_weights(gate_up, gate_bias):
    hp = GLA_HEADS // 2
    up = gate_up.reshape(2, GLA_GATE_RANK, hp, 128)
    z = jnp.zeros((GLA_GATE_RANK, hp, 128), gate_up.dtype)
    top = jnp.concatenate([up[0], z], axis=-1)
    bot = jnp.concatenate([z, up[1]], axis=-1)
    up_bd = jnp.concatenate([top, bot], axis=0)
    up_bd = up_bd.transpose(1, 0, 2)[:, None].astype(BF16)
    gb = gate_bias.reshape(2, hp, 128)
    gb_bd = jnp.concatenate([gb[0], gb[1]], axis=-1)[:, None, None]
    return up_bd, gb_bd


def kernel(x, ln_emb_g, ln_emb_b, rel_bias_table, w_in, lambda_q1, lambda_k1, lambda_q2, lambda_k2,
           diff_norm_w, gla_gate_up, gla_gate_bias, gla_norm_w, w_o, ln1_g, ln1_b,
           w_ffn1, b_ffn1, w_ffn2, b_ffn2, ln2_g, ln2_b):
    b, seq, _ = x.shape
    t = b * seq
    tq, sub = 2048, 256
    band = _bias_band(rel_bias_table, sub, seq)
    w_in_b = w_in.astype(BF16)

    h = x.reshape(t, D_MODEL)
    for li in range(DEPTH):
        if li == 0:
            (qd, kd, vd, gq, gk, gv, gr, gd,
             w_o_b, w1_b, w2_b) = _inproj(li, h, w_in_b, seq, ln_emb_g, ln_emb_b, cast=(w_o, w_ffn1, w_ffn2))
        else:
            qd, kd, vd, gq, gk, gv, gr, gd = _inproj(li, h, w_in_b, seq)
        r3 = lambda a: a.reshape(b, seq, a.shape[-1])
        lam_init = 0.8 - 0.6 * math.exp(-0.3 * li)
        d_out = _diff_attention(r3(qd), r3(kd), vd, band, lambda_q1[li], lambda_k1[li],
                                lambda_q2[li], lambda_k2[li], diff_norm_w[li],
                                lam_init=lam_init, tq=tq, sub=sub)
        up_bd, gb_bd = _gate_weights(gla_gate_up[li], gla_gate_bias[li])
        g_out = _gla(r3(gd), up_bd, gb_bd, r3(gq), r3(gk), r3(gv), r3(gr), gla_norm_w[li])
        h = _mix_mlp(li, d_out.reshape(t, DIFF_WIDTH), g_out.reshape(t, GLA_WIDTH), h, ln_emb_g, ln_emb_b,
                     w_o_b, ln1_g, ln1_b, w1_b, b_ffn1, w2_b, b_ffn2, ln2_g, ln2_b, emb_ln=(li == 0))
    return h.reshape(b, seq, D_MODEL)
```

```python
import functools
import math

import jax
import jax.numpy as jnp
from jax import lax
from jax.experimental import pallas as pl
from jax.experimental.pallas import tpu as pltpu

D_MODEL = 1024
DEPTH = 2
DIFF_HEADS = 4
DIFF_QK_DIM = 64
DIFF_V_DIM = 128
DIFF_WIDTH = 512
GLA_HEADS = 4
GLA_WIDTH = 512
GLA_V_DIM = 128
GLA_K_DIM = 64
GLA_KEY_WIDTH = 256
GLA_GATE_RANK = 16
GLA_GATE_TAU = 16.0
GLA_CHUNK = 64
D_FF = 4096
N_BUCKETS = 32
LN_EPS = 1e-5
RMS_EPS = 1e-5
ALPHA = (2.0 * DEPTH) ** 0.25
LOG2E = math.log2(math.e)
VT_ROWS = DIFF_V_DIM + 16
MIX_SUB_ROWS = 256
D_IN = 3104

V7X_VMEM_LIMIT_BYTES = 56 * 1024 * 1024

BF16 = jnp.bfloat16
F32 = jnp.float32

_NT = (((1,), (1,)), ((), ()))
_TN = (((0,), (0,)), ((), ()))


def _dot(a, b):
    return jnp.dot(a, b, preferred_element_type=F32)


def _layer_norm(y, g, b):
    mu = jnp.mean(y, axis=-1, keepdims=True)
    d = y - mu
    var = jnp.mean(d * d, axis=-1, keepdims=True)
    return d * lax.rsqrt(var + LN_EPS) * g + b


def _band_kernel(table_ref, band_ref, *, tq, seq):
    h = pl.program_id(0)
    width = 2 * seq - tq
    lo, hi = seq - tq - 128, seq + 128
    near = hi - lo
    r = lax.broadcasted_iota(jnp.int32, (near, tq), 1)
    m = lax.broadcasted_iota(jnp.int32, (near, tq), 0) + lo
    rel = m - (seq - tq) - r
    n = jnp.abs(rel)
    n2 = n * n
    large = jnp.full_like(n, 8)
    for p in range(7, 14):
        large = large + jnp.where(n2 >= (1 << p), 1, 0)
    bucket = jnp.where(n < 8, n, large) + jnp.where(rel > 0, 16, 0)
    acc = jnp.zeros((near, tq), F32)
    for i in range(N_BUCKETS):
        acc = jnp.where(bucket == i, table_ref[i * DIFF_HEADS + h], acc)
    half = N_BUCKETS // 2
    band_ref[0, 0:lo, :] = jnp.full((lo, tq), table_ref[(half - 1) * DIFF_HEADS + h] * LOG2E, F32)
    band_ref[0, lo:hi, :] = acc * LOG2E
    band_ref[0, hi:width, :] = jnp.full((width - hi, tq), table_ref[(N_BUCKETS - 1) * DIFF_HEADS + h] * LOG2E, F32)


def _bias_band(table, tq, seq):
    width = 2 * seq - tq
    return pl.pallas_call(
        functools.partial(_band_kernel, tq=tq, seq=seq),
        out_shape=jax.ShapeDtypeStruct((DIFF_HEADS, width, tq), F32),
        grid=(DIFF_HEADS,),
        in_specs=[pl.BlockSpec(memory_space=pltpu.SMEM)],
        out_specs=pl.BlockSpec((1, width, tq), lambda h: (h, 0, 0)),
        name="bias_band",
    )(table.reshape(-1))


def _inproj_kernel(*refs, apply_ln, n_cast):
    n_in = (4 if apply_ln else 2) + n_cast
    ins, outs = refs[:n_in], refs[n_in:]
    if n_cast:
        for src, dst in zip(ins[-n_cast:], outs[-n_cast:]):
            dst[...] = src[...].astype(BF16)
        ins, outs = ins[:-n_cast], outs[:-n_cast]
    refs = ins + outs
    if apply_ln:
        (x_ref, g_ref, b_ref, w_ref,
         qd_ref, kd_ref, vd_ref, gq_ref, gk_ref, gv_ref, gr_ref, gd_ref) = refs
    else:
        (x_ref, w_ref,
         qd_ref, kd_ref, vd_ref, gq_ref, gk_ref, gv_ref, gr_ref, gd_ref) = refs
    tm = x_ref.shape[0]
    halves = (slice(0, tm // 2), slice(tm // 2, tm))
    xbs = []
    for rows in halves:
        xn = x_ref[rows, :]
        if apply_ln:
            xn = _layer_norm(xn, g_ref[...], b_ref[...])
        xbs.append(xn.astype(BF16))
    pad_rows = VT_ROWS - DIFF_V_DIM
    ones_row = (lax.broadcasted_iota(jnp.int32, (pad_rows, tm // 2), 0) == 0).astype(BF16)
    for rows, xb in zip(halves, xbs):
        qd_ref[rows, :] = (_dot(xb, w_ref[0, :,0:512]) * (DIFF_QK_DIM ** -0.5 * LOG2E)).astype(BF16)
        kd_ref[rows, :] = _dot(xb, w_ref[0, :,512:1024]).astype(BF16)
        v = _dot(xb, w_ref[0, :,1024:1536])
        for hd in range(DIFF_HEADS):
            vd_ref[0, hd, 0:DIFF_V_DIM, rows] = v[:, hd * DIFF_V_DIM:(hd + 1) * DIFF_V_DIM].T.astype(BF16)
            vd_ref[0, hd, DIFF_V_DIM:VT_ROWS, rows] = ones_row
        gq_ref[rows, :] = _dot(xb, w_ref[0, :,1536:1792]) * (GLA_K_DIM ** -0.5)
        gk_ref[rows, :] = _dot(xb, w_ref[0, :,1792:2048])
        gv_ref[rows, :] = _dot(xb, w_ref[0, :,2048:2560]).astype(BF16)
        gr_ref[rows, :] = _dot(xb, w_ref[0, :,2560:3072])
        gd_ref[rows, :] = _dot(xb, w_ref[0, :,3072:D_IN])


def _inproj(li, x2d, w_pad, seq, ln_g=None, ln_b=None, *, cast=(), tm=1024):
    t = x2d.shape[0]
    apply_ln = ln_g is not None
    row = lambda n: pl.BlockSpec((tm, n), lambda i: (i, 0))
    const = lambda shape: pl.BlockSpec(shape, lambda i: (0,) * len(shape))
    in_specs = [row(D_MODEL)]
    args = [x2d]
    if apply_ln:
        in_specs += [const((1, D_MODEL)), const((1, D_MODEL))]
        args += [ln_g.reshape(1, -1), ln_b.reshape(1, -1)]
    in_specs.append(pl.BlockSpec((1, D_MODEL, D_IN), lambda i: (li, 0, 0), pipeline_mode=pl.Buffered(1)))
    args.append(w_pad)
    widths = [(512, BF16), (512, BF16), None, (256, F32), (256, F32),
              (512, BF16), (512, F32), (2 * GLA_GATE_RANK, F32)]
    out_shape = [jax.ShapeDtypeStruct((t, w[0]), w[1]) if w else None for w in widths]
    out_specs = [row(w[0]) if w else None for w in widths]
    tiles = seq // tm
    out_shape[2] = jax.ShapeDtypeStruct((t // seq, DIFF_HEADS, VT_ROWS, seq), BF16)
    out_specs[2] = pl.BlockSpec((1, DIFF_HEADS, VT_ROWS, tm), lambda i: (i // tiles, 0, 0, i % tiles))
    steps = t // tm
    for wgt in cast:
        depth, r, c = wgt.shape
        slab = pl.BlockSpec((depth, r // steps, c), lambda i: (0, i, 0))
        in_specs.append(slab)
        args.append(wgt)
        out_shape.append(jax.ShapeDtypeStruct(wgt.shape, BF16))
        out_specs.append(slab)
    return pl.pallas_call(
        functools.partial(_inproj_kernel, apply_ln=apply_ln, n_cast=len(cast)),
        out_shape=out_shape,
        grid=(t // tm,),
        in_specs=in_specs,
        out_specs=out_specs,
        compiler_params=pltpu.CompilerParams(
            dimension_semantics=("arbitrary",), vmem_limit_bytes=V7X_VMEM_LIMIT_BYTES),
        name="ln_inproj" if apply_ln else "inproj",
    )(*args)


def _attn_kernel(lq1_ref, lk1_ref, lq2_ref, lk2_ref, nw_ref, q_ref, k_ref, v_ref, band_ref,
                 o_ref, s_s, e_s, *, lam_init, tq, sub, seq, kc):
    qi = pl.program_id(2)
    lam = (jnp.exp(jnp.sum(lq1_ref[...] * lk1_ref[...], axis=-1, keepdims=True))
           - jnp.exp(jnp.sum(lq2_ref[...] * lk2_ref[...], axis=-1, keepdims=True)) + lam_init)
    first = lax.broadcasted_iota(jnp.int32, (1, 2 * DIFF_QK_DIM), 1) < DIFF_QK_DIM
    nsb = tq // sub
    nslots = s_s.shape[0]

    def logits(sb):
        q = q_ref[0, sb * sub:(sb + 1) * sub, :]
        zero = jnp.zeros_like(q)
        qq = jnp.concatenate([jnp.where(first, q, zero), jnp.where(first, zero, q)], axis=0)
        off = seq - sub - (qi * tq + sb * sub)
        m = jnp.full((1, 2 * sub), -jnp.inf, F32)
        for c in range(seq // kc):
            rows = slice(c * kc, (c + 1) * kc)
            s = lax.dot_general(k_ref[0, rows, :], qq, _NT, preferred_element_type=F32)
            bias = band_ref[0, pl.ds(pl.multiple_of(off + c * kc, 128), kc), :]
            s = s + jnp.concatenate([bias, bias], axis=1)
            s_s[sb % nslots, rows, :] = s
            m = jnp.maximum(m, jnp.max(s, axis=0, keepdims=True))
        return m

    def softmax_pv(sb, m):
        for c in range(seq // kc):
            rows = slice(c * kc, (c + 1) * kc)
            e_s[sb % nslots, rows, :] = jnp.exp2(s_s[sb % nslots, rows, :] - m).astype(BF16)
        ot = _dot(v_ref[0, 0], e_s[sb % nslots])
        r1 = 1.0 / ot[DIFF_V_DIM:DIFF_V_DIM + 1, 0:sub]
        r2 = lam / ot[DIFF_V_DIM:DIFF_V_DIM + 1, sub:2 * sub]
        o = ot[0:DIFF_V_DIM, 0:sub] * r1 - ot[0:DIFF_V_DIM, sub:2 * sub] * r2
        y = o * lax.rsqrt(jnp.mean(o * o, axis=0, keepdims=True) + RMS_EPS) * nw_ref[...] * (1.0 - lam_init)
        o_ref[0, sb * sub:(sb + 1) * sub, :] = y.T.astype(o_ref.dtype)

    m_next = logits(0)
    for sb in range(nsb):
        m = m_next
        if sb + 1 < nsb:
            m_next = logits(sb + 1)
        softmax_pv(sb, m)


def _diff_attention(qd, kd, vd, band, lq1, lk1, lq2, lk2, norm_w, *, lam_init, tq, sub):
    b, seq, _ = qd.shape
    nq = seq // tq
    vec = lambda n: pl.BlockSpec((1, n), lambda h, bi, qi: (0, 0))
    return pl.pallas_call(
        functools.partial(_attn_kernel, lam_init=lam_init, tq=tq, sub=sub, seq=seq, kc=512),
        out_shape=jax.ShapeDtypeStruct((b, seq, DIFF_WIDTH), BF16),
        grid=(DIFF_HEADS, b, nq),
        in_specs=[vec(DIFF_QK_DIM), vec(DIFF_QK_DIM), vec(DIFF_QK_DIM), vec(DIFF_QK_DIM),
                  pl.BlockSpec((DIFF_V_DIM, 1), lambda h, bi, qi: (0, 0)),
                  pl.BlockSpec((1, tq, 128), lambda h, bi, qi: (bi, qi, h)),
                  pl.BlockSpec((1, seq, 128), lambda h, bi, qi: (bi, 0, h)),
                  pl.BlockSpec((1, 1, VT_ROWS, seq), lambda h, bi, qi: (bi, h, 0, 0)),
                  pl.BlockSpec((1, 2 * seq - sub, sub), lambda h, bi, qi: (h, 0, 0))],
        out_specs=pl.BlockSpec((1, tq, 128), lambda h, bi, qi: (bi, qi, h)),
        scratch_shapes=[pltpu.VMEM((min(tq // sub, 3), seq, 2 * sub), F32),
                        pltpu.VMEM((min(tq // sub, 3), seq, 2 * sub), BF16)],
        compiler_params=pltpu.CompilerParams(
            dimension_semantics=("arbitrary", "arbitrary", "arbitrary"),
            vmem_limit_bytes=V7X_VMEM_LIMIT_BYTES),
        name="diff_attn",
    )(lq1.reshape(1, -1), lk1.reshape(1, -1), lq2.reshape(1, -1), lk2.reshape(1, -1),
      norm_w.reshape(-1, 1), qd, kd, vd, band)


def _chunk_scan(x, row, *, reverse):
    n = x.shape[0]
    d = 1
    while d < GLA_CHUNK:
        if reverse:
            x = x + jnp.where(row < GLA_CHUNK - d, pltpu.roll(x, n - d, 0), 0.0)
        else:
            x = x + jnp.where(row >= d, pltpu.roll(x, d, 0), 0.0)
        d *= 2
    return x


def _gla_kernel(gd_ref, up_ref, gb_ref, q_ref, k_ref, v_ref, r_ref, nw_ref, o_ref,
                qq_s, kk_s, bf_s, bb_s, kv_s, st_s, *, seq):
    c = GLA_CHUNK
    blk = 2 * c
    nb = seq // blk

    row = lax.broadcasted_iota(jnp.int32, (blk, 1), 0) & (c - 1)
    first_half = lax.broadcasted_iota(jnp.int32, (blk, 1), 0) < c

    def gates(hp, n):
        rows = slice(n * blk, (n + 1) * blk)
        z = _dot(gd_ref[0, rows, :].astype(BF16), up_ref[hp, 0]) + gb_ref[hp, 0]
        g = (jnp.minimum(z, 0.0) - jnp.log(1.0 + jnp.exp(-jnp.abs(z)))) * (1.0 / GLA_GATE_TAU)
        pf = _chunk_scan(g[:, :128], row, reverse=False)
        ub = _chunk_scan(g[:, 128:], row, reverse=True)
        bf_s[hp, rows, :] = pf
        bb_s[hp, rows, :] = ub
        bfn = pf - jnp.where(first_half, pf[c - 1:c, :], 0.0)
        bbn = ub - jnp.where(first_half, 0.0, ub[c:c + 1, :])
        q = q_ref[0, rows, hp * 128:(hp + 1) * 128]
        k = k_ref[0, rows, hp * 128:(hp + 1) * 128]
        qq_s[hp, rows, 0:128] = (q * jnp.exp(bfn)).astype(BF16)
        kk_s[hp, rows, 0:128] = (k * jnp.exp(-bfn)).astype(BF16)
        qq_s[hp, rows, 128:256] = (q * jnp.exp(bbn)).astype(BF16)
        kk_s[hp, rows, 128:256] = (k * jnp.exp(-bbn)).astype(BF16)

    def kv_scan(hp):
        for n in range(nb):
            rows = slice(n * blk, (n + 1) * blk)
            kv_s[hp, n] = lax.dot_general(v_ref[0, rows, hp * 256:(hp + 1) * 256], kk_s[hp, rows, :], _TN,
                                          preferred_element_type=F32)
        sr = lax.broadcasted_iota(jnp.int32, (256, 128), 0) < GLA_V_DIM
        sc = lax.broadcasted_iota(jnp.int32, (256, 128), 1) < GLA_K_DIM
        same_head = sr == sc
        sf = jnp.zeros((256, 128), F32)
        sb = jnp.zeros((256, 128), F32)
        for i in range(nb):
            t1 = bf_s[hp, i * blk + c - 1:i * blk + c, :]
            p2 = bf_s[hp, i * blk + blk - 1:i * blk + blk, :]
            st_s[hp, i, :, 0:128] = (sf * jnp.exp(t1)).astype(BF16)
            sf = jnp.exp(t1 + p2) * sf + jnp.exp(p2) * jnp.where(same_head, kv_s[hp, i, :, 0:128], 0.0)
            n = nb - 1 - i
            t2 = bb_s[hp, n * blk + c:n * blk + c + 1, :]
            u1 = bb_s[hp, n * blk:n * blk + 1, :]
            st_s[hp, n, :, 128:256] = (sb * jnp.exp(t2)).astype(BF16)
            sb = jnp.exp(u1 + t2) * sb + jnp.exp(u1) * jnp.where(same_head, kv_s[hp, n, :, 128:256], 0.0)

    lane = lax.broadcasted_iota(jnp.int32, (1, 256), 1)
    head0_v = lane < GLA_V_DIM
    quarter = [(lane >= i * GLA_K_DIM) & (lane < (i + 1) * GLA_K_DIM) for i in range(4)]
    ci = lax.broadcasted_iota(jnp.int32, (blk, 2 * blk), 0)
    si = lax.broadcasted_iota(jnp.int32, (blk, 2 * blk), 1) & (blk - 1)
    causal = si <= ci
    anti = si > ci

    def scores_of(hp, n):
        rows = slice(n * blk, (n + 1) * blk)
        kn = kk_s[hp, rows, :]
        zk = jnp.zeros_like(kn)
        kbd = jnp.concatenate([jnp.where(m, kn, zk) for m in quarter], axis=0)
        return lax.dot_general(qq_s[hp, rows, :], kbd, _NT, preferred_element_type=F32)

    def outputs(hp, n, scores):
        rows = slice(n * blk, (n + 1) * blk)
        qn = qq_s[hp, rows, :]
        vn = v_ref[0, rows, hp * 256:(hp + 1) * 256]
        prob = (jnp.where(causal, scores[:, 0:2 * blk], 0.0)
                + jnp.where(anti, scores[:, 2 * blk:4 * blk], 0.0)).astype(BF16)
        zv = jnp.zeros_like(vn)
        vbd = jnp.concatenate([jnp.where(head0_v, vn, zv), jnp.where(head0_v, zv, vn)], axis=0)
        o = _dot(prob, vbd) + lax.dot_general(qn, st_s[hp, n], _NT, preferred_element_type=F32)
        nw = nw_ref[...]
        for hh in range(2):
            cols = slice(hp * 256 + hh * GLA_V_DIM, hp * 256 + (hh + 1) * GLA_V_DIM)
            oh = o[:, hh * GLA_V_DIM:(hh + 1) * GLA_V_DIM]
            y = oh * lax.rsqrt(jnp.mean(oh * oh, axis=-1, keepdims=True) + RMS_EPS) * nw
            gate = r_ref[0, rows, cols]
            o_ref[0, rows, cols] = (y * (gate * jax.nn.sigmoid(gate))).astype(o_ref.dtype)

    for n in range(nb):
        gates(0, n)
    kv_scan(0)
    for hp in range(2):
        sc_next = scores_of(hp, 0)
        for n in range(nb):
            sc = sc_next
            if n + 1 < nb:
                sc_next = scores_of(hp, n + 1)
            outputs(hp, n, sc)
            if hp == 0:
                gates(1, n)
        if hp == 0:
            kv_scan(1)


def _gla(gd, up_bd, gb_bd, gq, gk, gv, gr, norm_w):
    b, seq, _ = gq.shape
    hp = GLA_HEADS // 2
    return pl.pallas_call(
        functools.partial(_gla_kernel, seq=seq),
        out_shape=jax.ShapeDtypeStruct((b, seq, GLA_WIDTH), BF16),
        grid=(b,),
        in_specs=[pl.BlockSpec((1, seq, 2 * GLA_GATE_RANK), lambda bi: (bi, 0, 0)),
                  pl.BlockSpec((hp, 1, 2 * GLA_GATE_RANK, 256), lambda bi: (0, 0, 0, 0)),
                  pl.BlockSpec((hp, 1, 1, 256), lambda bi: (0, 0, 0, 0)),
                  pl.BlockSpec((1, seq, GLA_KEY_WIDTH), lambda bi: (bi, 0, 0)),
                  pl.BlockSpec((1, seq, GLA_KEY_WIDTH), lambda bi: (bi, 0, 0)),
                  pl.BlockSpec((1, seq, GLA_WIDTH), lambda bi: (bi, 0, 0)),
                  pl.BlockSpec((1, seq, GLA_WIDTH), lambda bi: (bi, 0, 0)),
                  pl.BlockSpec((1, GLA_V_DIM), lambda bi: (0, 0))],
        out_specs=pl.BlockSpec((1, seq, GLA_WIDTH), lambda bi: (bi, 0, 0)),
        scratch_shapes=[pltpu.VMEM((hp, seq, 256), BF16), pltpu.VMEM((hp, seq, 256), BF16),
                        pltpu.VMEM((hp, seq, 128), F32), pltpu.VMEM((hp, seq, 128), F32),
                        pltpu.VMEM((hp, seq // (2 * GLA_CHUNK), 256, 256), F32),
                        pltpu.VMEM((hp, seq // (2 * GLA_CHUNK), 256, 256), BF16)],
        compiler_params=pltpu.CompilerParams(
            dimension_semantics=("arbitrary",),
            vmem_limit_bytes=V7X_VMEM_LIMIT_BYTES),
        name="gla",
    )(gd, up_bd, gb_bd, gq, gk, gv, gr, norm_w.reshape(1, -1))


def _mix_mlp_kernel(d_ref, g_ref, x_ref, eg_ref, eb_ref, wo_ref, l1g_ref, l1b_ref, w1_ref, b1_ref, w2_ref, b2_ref,
                    l2g_ref, l2b_ref, o_ref, *, ff_chunk, emb_ln):
    tm = x_ref.shape[0]
    halves = tuple(slice(r0, r0 + MIX_SUB_ROWS) for r0 in range(0, tm, MIX_SUB_ROWS))
    x1s = []
    for rows in halves:
        mix = (_dot(d_ref[rows, :], wo_ref[0, 0:DIFF_WIDTH, :])
               + _dot(g_ref[rows, :], wo_ref[0, DIFF_WIDTH:D_MODEL, :]))
        x = x_ref[rows, :]
        if emb_ln:
            x = _layer_norm(x, eg_ref[...], eb_ref[...])
        x1s.append(_layer_norm(ALPHA * x + mix, l1g_ref[0], l1b_ref[0]))
    for rows, x1 in zip(halves, x1s):
        xb = x1.astype(BF16)
        acc = jnp.zeros(x1.shape, F32)
        for c0 in range(0, D_FF, ff_chunk):
            h = _dot(xb, w1_ref[0, :, c0:c0 + ff_chunk]) + b1_ref[0, :, c0:c0 + ff_chunk]
            h = jnp.square(jnp.maximum(h, 0.0)).astype(BF16)
            acc = acc + _dot(h, w2_ref[0, c0:c0 + ff_chunk, :])
        o_ref[rows, :] = _layer_norm(ALPHA * x1 + (acc + b2_ref[0]), l2g_ref[0], l2b_ref[0])


def _mix_mlp(li, d2d, g2d, x2d, emb_g, emb_b, w_o, ln1_g, ln1_b, w1, b1, w2, b2, ln2_g, ln2_b,
             *, emb_ln, tm=1024, ff_chunk=1024):
    t = x2d.shape[0]
    row = lambda n: pl.BlockSpec((tm, n), lambda i: (i, 0))
    const = lambda shape: pl.BlockSpec(shape, lambda i: (0,) * len(shape))
    layer = lambda a: pl.BlockSpec((1,) + a.shape[1:], lambda i: (li,) + (0,) * (a.ndim - 1),
                                   pipeline_mode=pl.Buffered(1))
    vecs = [v.reshape(DEPTH, 1, -1) for v in (ln1_g, ln1_b, b1, b2, ln2_g, ln2_b)]
    l1g, l1b, b1r, b2r, l2g, l2b = vecs
    args = (d2d, g2d, x2d, emb_g.reshape(1, -1), emb_b.reshape(1, -1),
            w_o, l1g, l1b, w1, b1r, w2, b2r, l2g, l2b)
    in_specs = ([row(DIFF_WIDTH), row(GLA_WIDTH), row(D_MODEL), const((1, D_MODEL)), const((1, D_MODEL))]
                + [layer(a) for a in args[5:]])
    return pl.pallas_call(
        functools.partial(_mix_mlp_kernel, ff_chunk=ff_chunk, emb_ln=emb_ln),
        out_shape=jax.ShapeDtypeStruct((t, D_MODEL), F32),
        grid=(t // tm,),
        in_specs=in_specs,
        out_specs=row(D_MODEL),
        compiler_params=pltpu.CompilerParams(
            dimension_semantics=("arbitrary",), vmem_limit_bytes=V7X_VMEM_LIMIT_BYTES),
        name="mix_mlp",
    )(*args)


def _gate_weights(gate_up, gate_bias):
    hp = GLA_HEADS // 2
    up = gate_up.reshape(2, GLA_GATE_RANK, hp, 128)
    z = jnp.zeros((GLA_GATE_RANK, hp, 128), gate_up.dtype)
    top = jnp.concatenate([up[0], z], axis=-1)
    bot = jnp.concatenate([z, up[1]], axis=-1)
    up_bd = jnp.concatenate([top, bot], axis=0)
    up_bd = up_bd.transpose(1, 0, 2)[:, None].astype(BF16)
    gb = gate_bias.reshape(2, hp, 128)
    gb_bd = jnp.concatenate([gb[0], gb[1]], axis=-1)[:, None, None]
    return up_bd, gb_bd


def kernel(x, ln_emb_g, ln_emb_b, rel_bias_table, w_in, lambda_q1, lambda_k1, lambda_q2, lambda_k2,
           diff_norm_w, gla_gate_up, gla_gate_bias, gla_norm_w, w_o, ln1_g, ln1_b,
           w_ffn1, b_ffn1, w_ffn2, b_ffn2, ln2_g, ln2_b):
    b, seq, _ = x.shape
    t = b * seq
    tq, sub = 2048, 256
    band = _bias_band(rel_bias_table, sub, seq)
    w_in_b = w_in.astype(BF16)

    h = x.reshape(t, D_MODEL)
    for li in range(DEPTH):
        if li == 0:
            (qd, kd, vd, gq, gk, gv, gr, gd,
             w_o_b, w1_b, w2_b) = _inproj(li, h, w_in_b, seq, ln_emb_g, ln_emb_b, cast=(w_o, w_ffn1, w_ffn2))
        else:
            qd, kd, vd, gq, gk, gv, gr, gd = _inproj(li, h, w_in_b, seq)
        r3 = lambda a: a.reshape(b, seq, a.shape[-1])
        lam_init = 0.8 - 0.6 * math.exp(-0.3 * li)
        d_out = _diff_attention(r3(qd), r3(kd), vd, band, lambda_q1[li], lambda_k1[li],
                                lambda_q2[li], lambda_k2[li], diff_norm_w[li],
                                lam_init=lam_init, tq=tq, sub=sub)
        up_bd, gb_bd = _gate_weights(gla_gate_up[li], gla_gate_bias[li])
        g_out = _gla(r3(gd), up_bd, gb_bd, r3(gq), r3(gk), r3(gv), r3(gr), gla_norm_w[li])
        h = _mix_mlp(li, d_out.reshape(t, DIFF_WIDTH), g_out.reshape(t, GLA_WIDTH), h, ln_emb_g, ln_emb_b,
                     w_o_b, ln1_g, ln1_b, w1_b, b_ffn1, w2_b, b_ffn2, ln2_g, ln2_b, emb_ln=(li == 0))
    return h.reshape(b, seq, D_MODEL)
```
